```python
import math
import jax, jax.numpy as jnp
from jax import lax
import numpy as np

D_MODEL = 1024
BATCH = 4
SEQ = 8192
DEPTH = 1

HEAD_DIM = 64
HEADS_PER_GROUP = 4
DILATED_GROUPS = ((128, 1), (512, 4), (2048, 16))
N_ATTN_GROUPS = len(DILATED_GROUPS)
N_ATTN_HEADS = N_ATTN_GROUPS * HEADS_PER_GROUP
ATTN_WIDTH = N_ATTN_HEADS * HEAD_DIM
ATTN_OUT_WIDTH = HEADS_PER_GROUP * HEAD_DIM
BLOCK = 128
N_BUCKETS = 32
MAX_DISTANCE = 2048
NEG_INF = -1e30
SSM_GROUP = 16
SSM_WIDTH = 512
SSM_GROUPS = SSM_WIDTH // SSM_GROUP
SSM_STATE = 64
DT_MIN = 1e-3
DT_MAX = 1e-1
D_FF = 2816
EPS = 1e-6
IN_WIDTH = 3 * ATTN_WIDTH + SSM_WIDTH + 2 * D_MODEL

kernel_name = "hybrid_dilated_attn_s5_macaron"


def rms_norm(x, g):
    xf = x.astype(jnp.float32)
    y = xf * lax.rsqrt(jnp.mean(xf * xf, axis=-1, keepdims=True) + EPS)
    return (y * g.astype(jnp.float32)).astype(x.dtype)


def swiglu(h, w_gate, w_up, w_down):
    return (jax.nn.silu(h @ w_gate) * (h @ w_up)) @ w_down


def t5_bucket(dist):
    max_exact = N_BUCKETS // 2
    d = jnp.maximum(dist, 1).astype(jnp.float32)
    large = max_exact + (jnp.log(d / max_exact) / math.log(MAX_DISTANCE / max_exact)
                         * (N_BUCKETS - max_exact)).astype(jnp.int32)
    large = jnp.minimum(large, N_BUCKETS - 1)
    return jnp.where(dist < max_exact, dist, large)


def dilated_group_attention(q, k, v, bias_table_g, window, dilation):
    B, L, H, Dh = q.shape
    M = L // dilation
    n_steps = window // dilation
    nb = -(-M // BLOCK)
    Mp = nb * BLOCK

    def to_sub(t):
        t = t.reshape(B, M, dilation, H, Dh).transpose(0, 2, 1, 3, 4).reshape(B * dilation, M, H, Dh)
        t = jnp.pad(t, ((0, 0), (0, Mp - M), (0, 0), (0, 0)))
        return t.reshape(B * dilation, nb, BLOCK, H, Dh)

    def with_prev(t):
        prev = jnp.pad(t, ((0, 0), (1, 0), (0, 0), (0, 0), (0, 0)))[:, :-1]
        return jnp.concatenate([prev, t], axis=2)

    qs = to_sub(q).astype(jnp.float32)
    kb = with_prev(to_sub(k)).astype(jnp.float32)
    vb = with_prev(to_sub(v)).astype(jnp.float32)

    qi = jnp.arange(BLOCK)[:, None]
    kj = jnp.arange(2 * BLOCK)[None, :]
    steps = qi + BLOCK - kj
    band = (steps >= 0) & (steps <= n_steps)
    first_ok = (jnp.arange(nb)[:, None] > 0) | (kj >= BLOCK)
    mask = band[None] & first_ok[:, None, :]
    bucket = t5_bucket(jnp.maximum(steps, 0) * dilation)
    bias = bias_table_g.astype(jnp.float32)[bucket].transpose(2, 0, 1)

    logits = jnp.einsum('bnqhd,bnkhd->bhnqk', qs, kb) + bias[None, :, None]
    logits = jnp.where(mask[None, None], logits, NEG_INF)
    m = jnp.max(logits, axis=-1, keepdims=True)
    p = jnp.exp(logits - m)
    denom = jnp.sum(p, axis=-1)
    o = jnp.einsum('bhnqk,bnkhd->bnqhd', p, vb) / denom.transpose(0, 2, 3, 1)[..., None]
    lse = m[..., 0] + jnp.log(denom)

    o = o.reshape(B * dilation, Mp, H, Dh)[:, :M]
    o = o.reshape(B, dilation, M, H, Dh).transpose(0, 2, 1, 3, 4).reshape(B, L, H, Dh)
    lse = lse.transpose(0, 2, 3, 1).reshape(B * dilation, Mp, H)[:, :M]
    lse = lse.reshape(B, dilation, M, H).transpose(0, 2, 1, 3).reshape(B, L, H)
    return o, lse


def s5_mixer(u, a_re, a_im, log_dt, b_re, b_im, c_re, c_im, d_skip):
    B, L, _ = u.shape
    uf = u.astype(jnp.float32).reshape(B, L, SSM_GROUPS, SSM_GROUP)
    lam_re = a_re.astype(jnp.float32)
    lam_im = a_im.astype(jnp.float32)
    dt = jnp.exp(log_dt.astype(jnp.float32))[:, None]
    mag = jnp.exp(lam_re * dt)
    ab_re = mag * jnp.cos(lam_im * dt)
    ab_im = mag * jnp.sin(lam_im * dt)
    den = lam_re * lam_re + lam_im * lam_im
    xr = ab_re - 1.0
    coef_re = (xr * lam_re + ab_im * lam_im) / den
    coef_im = (ab_im * lam_re - xr * lam_im) / den
    br = b_re.astype(jnp.float32)
    bi = b_im.astype(jnp.float32)
    bb_re = coef_re[..., None] * br - coef_im[..., None] * bi
    bb_im = coef_re[..., None] * bi + coef_im[..., None] * br
    bu_re = jnp.einsum('gnc,blgc->lbgn', bb_re, uf)
    bu_im = jnp.einsum('gnc,blgc->lbgn', bb_im, uf)
    a_seq_re = jnp.broadcast_to(ab_re[None, None], (L, 1, SSM_GROUPS, SSM_STATE))
    a_seq_im = jnp.broadcast_to(ab_im[None, None], (L, 1, SSM_GROUPS, SSM_STATE))

    def combine(left, right):
        al_re, al_im, bl_re, bl_im = left
        ar_re, ar_im, brr, bri = right
        return (al_re * ar_re - al_im * ar_im,
                al_re * ar_im + al_im * ar_re,
                ar_re * bl_re - ar_im * bl_im + brr,
                ar_re * bl_im + ar_im * bl_re + bri)

    _, _, s_re, s_im = lax.associative_scan(combine, (a_seq_re, a_seq_im, bu_re, bu_im), axis=0)
    y = (jnp.einsum('gcn,lbgn->blgc', c_re.astype(jnp.float32), s_re)
         - jnp.einsum('gcn,lbgn->blgc', c_im.astype(jnp.float32), s_im)
         + d_skip.astype(jnp.float32).reshape(SSM_GROUPS, SSM_GROUP) * uf)
    return y.reshape(B, L, SSM_WIDTH).astype(u.dtype)


def setup_inputs(seed: int = 0) -> dict:
    key = jax.random.key(seed)
    ks = iter(jax.random.split(key, 32))
    f32 = jnp.float32

    def nrm(shape, scale):
        return jax.random.normal(next(ks), shape, f32) * scale

    def gain(shape):
        return 1.0 + 0.05 * jax.random.normal(next(ks), shape, f32)

    L_ = DEPTH
    n_idx = jnp.arange(SSM_STATE, dtype=f32)
    return {
        "x": jax.random.normal(next(ks), (BATCH, SEQ, D_MODEL), f32),
        "ffn1_norm": gain((L_, D_MODEL)),
        "ffn1_w_gate": nrm((L_, D_MODEL, D_FF), D_MODEL ** -0.5),
        "ffn1_w_up": nrm((L_, D_MODEL, D_FF), D_MODEL ** -0.5),
        "ffn1_w_down": nrm((L_, D_FF, D_MODEL), D_FF ** -0.5),
        "mix_norm": gain((L_, D_MODEL)),
        "w_in": nrm((L_, D_MODEL, IN_WIDTH), D_MODEL ** -0.5),
        "gate_bias": nrm((L_, 2 * D_MODEL), 0.1),
        "rel_bias_table": nrm((N_BUCKETS, N_ATTN_HEADS), 0.5),
        "ssm_a_re": -0.5 + nrm((L_, SSM_GROUPS, SSM_STATE), 0.01),
        "ssm_a_im": math.pi * n_idx + nrm((L_, SSM_GROUPS, SSM_STATE), 0.01),
        "ssm_log_dt": jax.random.uniform(next(ks), (L_, SSM_GROUPS), f32,
                                         math.log(DT_MIN), math.log(DT_MAX)),
        "ssm_b_re": nrm((L_, SSM_GROUPS, SSM_STATE, SSM_GROUP), (2 * SSM_GROUP) ** -0.5),
        "ssm_b_im": nrm((L_, SSM_GROUPS, SSM_STATE, SSM_GROUP), (2 * SSM_GROUP) ** -0.5),
        "ssm_c_re": nrm((L_, SSM_GROUPS, SSM_GROUP, SSM_STATE), (2 * SSM_STATE) ** -0.5),
        "ssm_c_im": nrm((L_, SSM_GROUPS, SSM_GROUP, SSM_STATE), (2 * SSM_STATE) ** -0.5),
        "ssm_d": nrm((L_, SSM_WIDTH), 1.0),
        "ssm_w_glu": nrm((L_, SSM_WIDTH, 2 * SSM_WIDTH), SSM_WIDTH ** -0.5),
        "w_attn_branch": nrm((L_, ATTN_OUT_WIDTH, D_MODEL), ATTN_OUT_WIDTH ** -0.5),
        "w_ssm_branch": nrm((L_, SSM_WIDTH, D_MODEL), SSM_WIDTH ** -0.5),
        "w_out": nrm((L_, D_MODEL, D_MODEL), D_MODEL ** -0.5),
        "ffn2_norm": gain((L_, D_MODEL)),
        "ffn2_w_gate": nrm((L_, D_MODEL, D_FF), D_MODEL ** -0.5),
        "ffn2_w_up": nrm((L_, D_MODEL, D_FF), D_MODEL ** -0.5),
        "ffn2_w_down": nrm((L_, D_FF, D_MODEL), D_FF ** -0.5),
        "final_norm": gain((D_MODEL,)),
    }


def reference(x, ffn1_norm, ffn1_w_gate, ffn1_w_up, ffn1_w_down, mix_norm, w_in, gate_bias,
              rel_bias_table, ssm_a_re, ssm_a_im, ssm_log_dt, ssm_b_re, ssm_b_im, ssm_c_re,
              ssm_c_im, ssm_d, ssm_w_glu, w_attn_branch, w_ssm_branch, w_out, ffn2_norm,
              ffn2_w_gate, ffn2_w_up, ffn2_w_down, final_norm):
    B, L, _ = x.shape
    scale = HEAD_DIM ** -0.5
    for l in range(DEPTH):
        x = x + 0.5 * swiglu(rms_norm(x, ffn1_norm[l]), ffn1_w_gate[l], ffn1_w_up[l], ffn1_w_down[l])

        h = rms_norm(x, mix_norm[l])
        z = h @ w_in[l]
        c0 = ATTN_WIDTH
        q = z[..., :c0].reshape(B, L, N_ATTN_HEADS, HEAD_DIM) * scale
        k = z[..., c0:2 * c0].reshape(B, L, N_ATTN_HEADS, HEAD_DIM)
        v = z[..., 2 * c0:3 * c0].reshape(B, L, N_ATTN_HEADS, HEAD_DIM)
        c1 = 3 * c0
        u = z[..., c1:c1 + SSM_WIDTH]
        c2 = c1 + SSM_WIDTH
        g_attn = jax.nn.sigmoid(z[..., c2:c2 + D_MODEL] + gate_bias[l, :D_MODEL])
        g_ssm = jax.nn.sigmoid(z[..., c2 + D_MODEL:] + gate_bias[l, D_MODEL:])

        outs, lses = [], []
        for g, (window, dilation) in enumerate(DILATED_GROUPS):
            hs = slice(g * HEADS_PER_GROUP, (g + 1) * HEADS_PER_GROUP)
            o_g, lse_g = dilated_group_attention(q[:, :, hs], k[:, :, hs], v[:, :, hs],
                                                 rel_bias_table[:, hs], window, dilation)
            outs.append(o_g)
            lses.append(lse_g)
        o_stack = jnp.stack(outs, axis=2)
        w_grp = jax.nn.softmax(jnp.stack(lses, axis=2), axis=2)
        o_attn = jnp.sum(w_grp[..., None] * o_stack, axis=2).reshape(B, L, ATTN_OUT_WIDTH)
        y_attn = o_attn.astype(x.dtype) @ w_attn_branch[l]

        y_s = jax.nn.gelu(s5_mixer(u, ssm_a_re[l], ssm_a_im[l], ssm_log_dt[l], ssm_b_re[l],
                                   ssm_b_im[l], ssm_c_re[l], ssm_c_im[l], ssm_d[l]))
        glu = y_s @ ssm_w_glu[l]
        y_s = glu[..., :SSM_WIDTH] * jax.nn.sigmoid(glu[..., SSM_WIDTH:])
        y_ssm = y_s @ w_ssm_branch[l]

        x = x + (g_attn * y_attn + g_ssm * y_ssm) @ w_out[l]

        x = x + 0.5 * swiglu(rms_norm(x, ffn2_norm[l]), ffn2_w_gate[l], ffn2_w_up[l], ffn2_w_down[l])
    return rms_norm(x, final_norm)
```

```python
import functools
import math

import jax
import jax.numpy as jnp
import numpy as np
from jax import lax
from jax.experimental import pallas as pl
from jax.experimental.pallas import tpu as pltpu

F32 = jnp.float32
BF16 = jnp.bfloat16

D_MODEL = 1024
D_FF = 2816
EPS = 1e-6
HEAD_DIM = 64
HEADS_PER_GROUP = 4
GROUP_WIDTH = HEADS_PER_GROUP * HEAD_DIM
DILATIONS = (1, 4, 16)
WINDOWS = (128, 512, 2048)
N_GROUPS = len(DILATIONS)
ATTN_WIDTH = N_GROUPS * GROUP_WIDTH
BLOCK = 128
N_BUCKETS = 32
MAX_DISTANCE = 2048
NEG_INF = -1e30
SSM_GROUP = 16
SSM_WIDTH = 512
SSM_GROUPS = SSM_WIDTH // SSM_GROUP
SSM_STATE = 64
IN_WIDTH = 3 * ATTN_WIDTH + SSM_WIDTH + 2 * D_MODEL

LANES = 128
MXU_WIDTH = 256
VMEM_LIMIT_BYTES = 56 * 1024 * 1024

FFN_TILE = 512
PROJ_TILE = 512
ATTN_TILE = BLOCK * DILATIONS[-1]
SSM_CHUNK = 16
SSM_OCTET = LANES // SSM_GROUP
N_OCTETS = SSM_GROUPS // SSM_OCTET
OCTET_STATE = SSM_OCTET * SSM_STATE


def _rms(x, g):
    return x * lax.rsqrt(jnp.mean(x * x, axis=-1, keepdims=True) + EPS) * g


def _const_spec(shape):
    return pl.BlockSpec(shape, lambda *_: (0,) * len(shape), pipeline_mode=pl.Buffered(1))


def _swiglu_residual(x, g, wg_ref, wu_ref, wd_ref):
    h = _rms(x, g).astype(BF16)
    acc = jnp.zeros(x.shape, F32)
    for c in range(D_FF // MXU_WIDTH):
        sl = slice(c * MXU_WIDTH, (c + 1) * MXU_WIDTH)
        a = jnp.dot(h, wg_ref[:, sl], preferred_element_type=F32)
        b = jnp.dot(h, wu_ref[:, sl], preferred_element_type=F32)
        t = (a * jax.nn.sigmoid(a) * b).astype(BF16)
        acc = acc + jnp.dot(t, wd_ref[sl, :], preferred_element_type=F32)
    return x + 0.5 * acc


def _ffn_kernel(x_ref, g_ref, wg_ref, wu_ref, wd_ref, o_ref):
    o_ref[...] = _swiglu_residual(x_ref[...], g_ref[...], wg_ref, wu_ref, wd_ref)


def _ffn_final_kernel(x_ref, g_ref, wg_ref, wu_ref, wd_ref, fn_ref, o_ref):
    y = _swiglu_residual(x_ref[...], g_ref[...], wg_ref, wu_ref, wd_ref)
    o_ref[...] = _rms(y, fn_ref[...])


def _ffn(x, g, wg, wu, wd, final_g=None):
    n_tok = x.shape[0]
    tm = FFN_TILE
    tile = pl.BlockSpec((tm, D_MODEL), lambda i: (i, 0))
    in_specs = [tile, _const_spec((1, D_MODEL)), _const_spec((D_MODEL, D_FF)),
                _const_spec((D_MODEL, D_FF)), _const_spec((D_FF, D_MODEL))]
    args = [x, g.reshape(1, D_MODEL), wg, wu, wd]
    kern = _ffn_kernel
    if final_g is not None:
        in_specs.append(_const_spec((1, D_MODEL)))
        args.append(final_g.reshape(1, D_MODEL))
        kern = _ffn_final_kernel
    return pl.pallas_call(
        kern,
        grid=(n_tok // tm,),
        in_specs=in_specs,
        out_specs=tile,
        out_shape=jax.ShapeDtypeStruct((n_tok, D_MODEL), F32),
        compiler_params=pltpu.CompilerParams(
            dimension_semantics=("arbitrary",), vmem_limit_bytes=VMEM_LIMIT_BYTES),
        name="ffn_final" if final_g is not None else "ffn",
    )(*args)


N_QKV_CHUNKS = 3 * N_GROUPS
N_U_CHUNKS = SSM_WIDTH // MXU_WIDTH
N_GATE_CHUNKS = 2 * D_MODEL // MXU_WIDTH
N_STRIDED = 3 * (N_GROUPS - 1)


def _inproj_kernel(x_ref, g_ref, w_ref, gb_ref, *refs):
    qkv_refs = refs[:N_QKV_CHUNKS]
    u_ref, gate_ref, slab_ref = refs[N_QKV_CHUNKS:]
    tm = x_ref.shape[0]
    h = _rms(x_ref[...], g_ref[...]).astype(BF16)
    strided_idx = 0
    for c in range(IN_WIDTH // MXU_WIDTH):
        z = jnp.dot(h, w_ref[:, c * MXU_WIDTH:(c + 1) * MXU_WIDTH], preferred_element_type=F32)
        if c < N_QKV_CHUNKS:
            kind, grp = divmod(c, N_GROUPS)
            if kind == 0:
                z = z * (HEAD_DIM ** -0.5)
            d = DILATIONS[grp]
            o_ref = qkv_refs[c]
            if d == 1:
                o_ref[0, 0] = z.astype(BF16)
            else:
                base = 2 * strided_idx
                strided_idx += 1
                for s in range(2):
                    slab_ref[base + s] = z[:, s * LANES:(s + 1) * LANES]
                for r in range(d):
                    for s in range(2):
                        o_ref[0, r, :, s * LANES:(s + 1) * LANES] = (
                            slab_ref[base + s, pl.ds(r, tm // d, stride=d), :].astype(BF16))
        elif c < N_QKV_CHUNKS + N_U_CHUNKS:
            j = c - N_QKV_CHUNKS
            u_ref[:, j * MXU_WIDTH:(j + 1) * MXU_WIDTH] = z
        else:
            j = c - N_QKV_CHUNKS - N_U_CHUNKS
            sl = slice(j * MXU_WIDTH, (j + 1) * MXU_WIDTH)
            gate_ref[:, sl] = jax.nn.sigmoid(z + gb_ref[:, sl]).astype(BF16)


def _in_proj(x, g, w_in, gate_bias, batch, seq):
    n_tok = x.shape[0]
    tm = PROJ_TILE
    tiles_per_seq = seq // tm
    out_shapes, out_specs = [], []
    for _ in range(3):
        for d in DILATIONS:
            out_shapes.append(jax.ShapeDtypeStruct((batch, d, seq // d, GROUP_WIDTH), BF16))
            out_specs.append(pl.BlockSpec(
                (1, d, tm // d, GROUP_WIDTH),
                lambda i: (i // tiles_per_seq, 0, i % tiles_per_seq, 0)))
    out_shapes.append(jax.ShapeDtypeStruct((n_tok, SSM_WIDTH), F32))
    out_specs.append(pl.BlockSpec((tm, SSM_WIDTH), lambda i: (i, 0)))
    out_shapes.append(jax.ShapeDtypeStruct((n_tok, 2 * D_MODEL), BF16))
    out_specs.append(pl.BlockSpec((tm, 2 * D_MODEL), lambda i: (i, 0)))
    outs = pl.pallas_call(
        _inproj_kernel,
        grid=(n_tok // tm,),
        in_specs=[pl.BlockSpec((tm, D_MODEL), lambda i: (i, 0)),
                  _const_spec((1, D_MODEL)),
                  _const_spec((D_MODEL, IN_WIDTH)),
                  _const_spec((1, 2 * D_MODEL))],
        out_specs=out_specs,
        out_shape=out_shapes,
        scratch_shapes=[pltpu.VMEM((2 * N_STRIDED, tm, LANES), F32)],
        compiler_params=pltpu.CompilerParams(
            dimension_semantics=("arbitrary",), vmem_limit_bytes=VMEM_LIMIT_BYTES),
        name="in_proj",
    )(x, g.reshape(1, D_MODEL), w_in, gate_bias.reshape(1, 2 * D_MODEL))
    q, k, v = outs[0:3], outs[3:6], outs[6:9]
    return q, k, v, outs[9], outs[10]


def _t5_bucket_np(dist):
    max_exact = N_BUCKETS // 2
    d = np.maximum(dist, 1).astype(np.float32)
    ratio = np.log(d / np.float32(max_exact)) / np.float32(math.log(MAX_DISTANCE / max_exact))
    large = max_exact + (ratio * np.float32(N_BUCKETS - max_exact)).astype(np.int32)
    large = np.minimum(large, N_BUCKETS - 1)
    return np.where(dist < max_exact, dist, large)


def _attn_bias(rel_bias_table):
    qi = np.arange(BLOCK)[:, None]
    kj = np.arange(2 * BLOCK)[None, :]
    steps = qi + BLOCK - kj
    full, first = [], []
    for grp, (window, d) in enumerate(zip(WINDOWS, DILATIONS)):
        band = (steps >= 0) & (steps <= window // d)
        bucket = _t5_bucket_np(np.maximum(steps, 0) * d)
        tab = rel_bias_table[:, grp * HEADS_PER_GROUP:(grp + 1) * HEADS_PER_GROUP].astype(F32)
        bias = jnp.transpose(tab[bucket], (2, 0, 1))
        b_full = jnp.where(band[None], bias, NEG_INF).reshape(HEADS_PER_GROUP * BLOCK, 2 * BLOCK)
        b_first = jnp.where((band & (kj >= BLOCK))[None], bias, NEG_INF).reshape(
            HEADS_PER_GROUP * BLOCK, 2 * BLOCK)
        full.append(b_full)
        first.append(b_first)
    return full, first


def _attn_block(q_blk, k_blk, v_blk, bias):
    lane_head = lax.broadcasted_iota(jnp.int32, (BLOCK, GROUP_WIDTH), 1) // HEAD_DIM
    zero = jnp.zeros_like(q_blk)
    qs = jnp.concatenate(
        [jnp.where(lane_head == h, q_blk, zero) for h in range(HEADS_PER_GROUP)], axis=0)
    logits = lax.dot_general(qs, k_blk, (((1,), (1,)), ((), ())), preferred_element_type=F32) + bias
    m = jnp.max(logits, axis=-1, keepdims=True)
    p = jnp.exp(logits - m)
    l = jnp.sum(p, axis=-1, keepdims=True)
    pv = jnp.dot(p.astype(BF16), v_blk, preferred_element_type=F32)
    o = jnp.zeros((BLOCK, GROUP_WIDTH), F32)
    den = jnp.ones((BLOCK, GROUP_WIDTH), F32)
    mx = jnp.zeros((BLOCK, GROUP_WIDTH), F32)
    for h in range(HEADS_PER_GROUP):
        rows = slice(h * BLOCK, (h + 1) * BLOCK)
        sel = lane_head == h
        o = jnp.where(sel, pv[rows], o)
        den = jnp.where(sel, l[rows], den)
        mx = jnp.where(sel, m[rows], mx)
    return o / den, mx + jnp.log(den)


def _attn_kernel(*refs):
    n = N_GROUPS
    q_refs, kc_refs, kp_refs = refs[0:n], refs[n:2 * n], refs[2 * n:3 * n]
    vc_refs, vp_refs = refs[3 * n:4 * n], refs[4 * n:5 * n]
    bias_refs, biasf_refs = refs[5 * n:6 * n], refs[6 * n:7 * n]
    o_ref, o_scr, l_scr = refs[7 * n:]
    tile = pl.program_id(1)

    for grp, d in enumerate(DILATIONS):
        q_ref, kc_ref, kp_ref = q_refs[grp], kc_refs[grp], kp_refs[grp]
        vc_ref, vp_ref = vc_refs[grp], vp_refs[grp]
        bias_ref, biasf_ref = bias_refs[grp], biasf_refs[grp]
        blocks_per_residue = ATTN_TILE // d // BLOCK

        def store(r, blk, o, lse, grp=grp, d=d):
            start = blk * (BLOCK * d) + r
            for s in range(2):
                rows = pl.ds(start, BLOCK, stride=d) if d > 1 else pl.ds(start, BLOCK)
                o_scr[grp, s, rows, :] = o[:, s * LANES:(s + 1) * LANES]
                l_scr[grp, s, rows, :] = lse[:, s * LANES:(s + 1) * LANES]

        def residue(r, carry, q_ref=q_ref, kc_ref=kc_ref, kp_ref=kp_ref, vc_ref=vc_ref,
                    vp_ref=vp_ref, bias_ref=bias_ref, biasf_ref=biasf_ref, store=store,
                    blocks_per_residue=blocks_per_residue):
            k0 = jnp.concatenate([kp_ref[0, r], kc_ref[0, r, 0:BLOCK, :]], axis=0)
            v0 = jnp.concatenate([vp_ref[0, r], vc_ref[0, r, 0:BLOCK, :]], axis=0)
            bias0 = jnp.where(tile == 0, biasf_ref[...], bias_ref[...])
            o, lse = _attn_block(q_ref[0, r, 0:BLOCK, :], k0, v0, bias0)
            store(r, 0, o, lse)

            def block(blk, c):
                kv_rows = pl.ds(pl.multiple_of((blk - 1) * BLOCK, BLOCK), 2 * BLOCK)
                q_rows = pl.ds(pl.multiple_of(blk * BLOCK, BLOCK), BLOCK)
                o, lse = _attn_block(q_ref[0, r, q_rows, :], kc_ref[0, r, kv_rows, :],
                                     vc_ref[0, r, kv_rows, :], bias_ref[...])
                store(r, blk, o, lse)
                return c

            if blocks_per_residue > 1:
                lax.fori_loop(1, blocks_per_residue, block, 0)
            return carry

        if d > 1:
            lax.fori_loop(0, d, residue, 0)
        else:
            residue(0, 0)

    merge_rows = 256

    def merge(c, carry):
        rows = pl.ds(pl.multiple_of(c * merge_rows, merge_rows), merge_rows)
        for s in range(2):
            lses = [l_scr[grp, s, rows, :] for grp in range(N_GROUPS)]
            mx = functools.reduce(jnp.maximum, lses)
            es = [jnp.exp(x - mx) for x in lses]
            num = sum(e * o_scr[grp, s, rows, :] for grp, e in enumerate(es))
            o_ref[rows, s * LANES:(s + 1) * LANES] = (num / sum(es)).astype(BF16)
        return carry

    lax.fori_loop(0, ATTN_TILE // merge_rows, merge, 0)


def _attention(q, k, v, rel_bias_table, batch, seq):
    tiles = seq // ATTN_TILE
    bias_full, bias_first = _attn_bias(rel_bias_table)

    def cur_spec(d):
        return pl.BlockSpec((1, d, ATTN_TILE // d, GROUP_WIDTH), lambda b, j: (b, 0, j, 0))

    def prev_spec(d):
        per_tile = ATTN_TILE // d // BLOCK
        return pl.BlockSpec((1, d, BLOCK, GROUP_WIDTH),
                            lambda b, j: (b, 0, jnp.maximum(j * per_tile - 1, 0), 0))

    bias_shape = (HEADS_PER_GROUP * BLOCK, 2 * BLOCK)
    in_specs = ([cur_spec(d) for d in DILATIONS] + [cur_spec(d) for d in DILATIONS]
                + [prev_spec(d) for d in DILATIONS] + [cur_spec(d) for d in DILATIONS]
                + [prev_spec(d) for d in DILATIONS] + [_const_spec(bias_shape)] * (2 * N_GROUPS))
    return pl.pallas_call(
        _attn_kernel,
        grid=(batch, tiles),
        in_specs=in_specs,
        out_specs=pl.BlockSpec((ATTN_TILE, GROUP_WIDTH), lambda b, j: (b * tiles + j, 0)),
        out_shape=jax.ShapeDtypeStruct((batch * seq, GROUP_WIDTH), BF16),
        scratch_shapes=[pltpu.VMEM((N_GROUPS, 2, ATTN_TILE, LANES), F32),
                        pltpu.VMEM((N_GROUPS, 2, ATTN_TILE, LANES), F32)],
        compiler_params=pltpu.CompilerParams(
            dimension_semantics=("arbitrary", "arbitrary"), vmem_limit_bytes=VMEM_LIMIT_BYTES),
        name="dilated_attn",
    )(*q, *k, *k, *v, *v, *bias_full, *bias_first)


def _s5_matrices(a_re, a_im, log_dt, b_re, b_im, c_re, c_im, d_skip):
    hi = lax.Precision.HIGHEST
    tc = SSM_CHUNK
    lam_re = a_re.astype(F32)
    lam_im = a_im.astype(F32)
    dt = jnp.exp(log_dt.astype(F32))[:, None]
    mag = jnp.exp(lam_re * dt)
    ab_re = mag * jnp.cos(lam_im * dt)
    ab_im = mag * jnp.sin(lam_im * dt)
    den = lam_re * lam_re + lam_im * lam_im
    xr = ab_re - 1.0
    coef_re = (xr * lam_re + ab_im * lam_im) / den
    coef_im = (ab_im * lam_re - xr * lam_im) / den
    br = b_re.astype(F32)
    bi = b_im.astype(F32)
    bb_re = coef_re[..., None] * br - coef_im[..., None] * bi
    bb_im = coef_re[..., None] * bi + coef_im[..., None] * br
    cr = c_re.astype(F32)
    ci = c_im.astype(F32)

    def a_pow(k):
        kk = k.astype(F32)[:, None, None]
        pm = jnp.exp(lam_re * dt * kk)
        return pm * jnp.cos(lam_im * dt * kk), pm * jnp.sin(lam_im * dt * kk)

    lags = jnp.arange(tc)
    pr, pi = a_pow(lags)
    abr = pr[..., None] * bb_re - pi[..., None] * bb_im
    abi = pr[..., None] * bb_im + pi[..., None] * bb_re
    kern = (jnp.einsum('gon,lgnc->lgoc', cr, abr, precision=hi)
            - jnp.einsum('gon,lgnc->lgoc', ci, abi, precision=hi))
    skip = d_skip.astype(F32).reshape(SSM_GROUPS, SSM_GROUP)
    kern = kern.at[0].add(skip[:, :, None] * jnp.eye(SSM_GROUP, dtype=F32))

    eye8 = jnp.eye(SSM_OCTET, dtype=F32)
    lag_idx = lags[None, :] - lags[:, None]
    ksel = jnp.where((lag_idx >= 0)[:, :, None, None, None], kern[jnp.maximum(lag_idx, 0)], 0.0)
    ksel = ksel.reshape(tc, tc, N_OCTETS, SSM_OCTET, SSM_GROUP, SSM_GROUP)
    kmat = jnp.einsum('stogxc,gh->osgcthx', ksel, eye8).reshape(
        N_OCTETS, tc * LANES, tc * LANES)

    abr_o = abr[::-1].reshape(tc, N_OCTETS, SSM_OCTET, SSM_STATE, SSM_GROUP)
    abi_o = abi[::-1].reshape(tc, N_OCTETS, SSM_OCTET, SSM_STATE, SSM_GROUP)
    p_re = jnp.einsum('sognc,gh->osgchn', abr_o, eye8).reshape(N_OCTETS, tc * LANES, OCTET_STATE)
    p_im = jnp.einsum('sognc,gh->osgchn', abi_o, eye8).reshape(N_OCTETS, tc * LANES, OCTET_STATE)
    pmat = jnp.concatenate([p_re, p_im], axis=-1)

    qr_pow, qi_pow = a_pow(lags + 1)
    q_sr = cr[None] * qr_pow[:, :, None, :] - ci[None] * qi_pow[:, :, None, :]
    q_si = -(cr[None] * qi_pow[:, :, None, :] + ci[None] * qr_pow[:, :, None, :])
    q_sr = q_sr.reshape(tc, N_OCTETS, SSM_OCTET, SSM_GROUP, SSM_STATE)
    q_si = q_si.reshape(tc, N_OCTETS, SSM_OCTET, SSM_GROUP, SSM_STATE)
    q_re = jnp.einsum('togxn,gh->ognthx', q_sr, eye8).reshape(N_OCTETS, OCTET_STATE, tc * LANES)
    q_im = jnp.einsum('togxn,gh->ognthx', q_si, eye8).reshape(N_OCTETS, OCTET_STATE, tc * LANES)
    qmat = jnp.concatenate([q_re, q_im], axis=1)

    ac_re, ac_im = a_pow(jnp.array([tc]))
    a_chunk = jnp.concatenate([ac_re.reshape(N_OCTETS, 1, OCTET_STATE),
                               ac_im.reshape(N_OCTETS, 1, OCTET_STATE)], axis=-1)
    return kmat.astype(BF16), pmat.astype(BF16), qmat.astype(BF16), a_chunk


def _s5_kernel(u_ref, k_ref, p_ref, q_ref, a_ref, o_ref, x_scr, s_scr, z_scr, y_scr):
    n_chunks = x_scr.shape[0]
    tc = SSM_CHUNK
    for t in range(tc):
        x_scr[:, t * LANES:(t + 1) * LANES] = u_ref[pl.ds(t, n_chunks, stride=tc), :].astype(BF16)
    s_scr[...] = jnp.dot(x_scr[...], p_ref[0], preferred_element_type=F32)

    a_re = a_ref[0, :, 0:OCTET_STATE]
    a_im = a_ref[0, :, OCTET_STATE:2 * OCTET_STATE]

    def step(j, carry):
        c_re, c_im = carry
        row = pl.ds(j, 1)
        z_scr[row, 0:OCTET_STATE] = c_re
        z_scr[row, OCTET_STATE:2 * OCTET_STATE] = c_im
        inc_re = s_scr[row, 0:OCTET_STATE]
        inc_im = s_scr[row, OCTET_STATE:2 * OCTET_STATE]
        return (a_re * c_re - a_im * c_im + inc_re, a_re * c_im + a_im * c_re + inc_im)

    zero = jnp.zeros((1, OCTET_STATE), F32)
    lax.fori_loop(0, n_chunks, step, (zero, zero))

    zb = z_scr[...].astype(BF16)
    for pair in range(tc // 2):
        cols = slice(pair * MXU_WIDTH, (pair + 1) * MXU_WIDTH)
        live = (2 * pair + 2) * LANES
        y = (jnp.dot(x_scr[:, 0:live], k_ref[0, 0:live, cols], preferred_element_type=F32)
             + jnp.dot(zb, q_ref[0, :, cols], preferred_element_type=F32))
        y_scr[pl.ds(2 * pair, n_chunks, stride=tc), :] = y[:, 0:LANES]
        y_scr[pl.ds(2 * pair + 1, n_chunks, stride=tc), :] = y[:, LANES:2 * LANES]
    o_ref[...] = jax.nn.gelu(y_scr[...]).astype(BF16)


def _s5(u, mats, batch, seq):
    kmat, pmat, qmat, a_chunk = mats
    n_chunks = seq // SSM_CHUNK
    width = SSM_CHUNK * LANES

    def per_octet(shape):
        return pl.BlockSpec((1,) + shape, lambda o, b: (o, 0, 0), pipeline_mode=pl.Buffered(1))

    return pl.pallas_call(
        _s5_kernel,
        grid=(N_OCTETS, batch),
        in_specs=[pl.BlockSpec((seq, LANES), lambda o, b: (b, o)),
                  per_octet((width, width)),
                  per_octet((width, 2 * OCTET_STATE)),
                  per_octet((2 * OCTET_STATE, width)),
                  per_octet((1, 2 * OCTET_STATE))],
        out_specs=pl.BlockSpec((seq, LANES), lambda o, b: (b, o)),
        out_shape=jax.ShapeDtypeStruct((batch * seq, SSM_WIDTH), BF16),
        scratch_shapes=[pltpu.VMEM((n_chunks, width), BF16),
                        pltpu.VMEM((n_chunks, 2 * OCTET_STATE), F32),
                        pltpu.VMEM((n_chunks, 2 * OCTET_STATE), F32),
                        pltpu.VMEM((seq, LANES), F32)],
        compiler_params=pltpu.CompilerParams(
            dimension_semantics=("arbitrary", "arbitrary"), vmem_limit_bytes=VMEM_LIMIT_BYTES),
        name="s5_mixer",
    )(u, kmat, pmat, qmat, a_chunk)


def _merge_kernel(y_ref, oa_ref, gate_ref, x_ref, wglu_ref, wssm_ref, wattn_ref, wout_ref, o_ref):
    glu = jnp.dot(y_ref[...], wglu_ref[...], preferred_element_type=F32)
    ys = (glu[:, 0:SSM_WIDTH] * jax.nn.sigmoid(glu[:, SSM_WIDTH:2 * SSM_WIDTH])).astype(BF16)
    y_ssm = jnp.dot(ys, wssm_ref[...], preferred_element_type=F32)
    y_attn = jnp.dot(oa_ref[...], wattn_ref[...], preferred_element_type=F32)
    g_attn = gate_ref[:, 0:D_MODEL].astype(F32)
    g_ssm = gate_ref[:, D_MODEL:2 * D_MODEL].astype(F32)
    mixed = (g_attn * y_attn + g_ssm * y_ssm).astype(BF16)
    o_ref[...] = x_ref[...] + jnp.dot(mixed, wout_ref[...], preferred_element_type=F32)


def _merge(y_gelu, o_attn, gates, x, w_glu, w_ssm, w_attn, w_out):
    n_tok = x.shape[0]
    tm = PROJ_TILE

    def rows(width):
        return pl.BlockSpec((tm, width), lambda i: (i, 0))

    return pl.pallas_call(
        _merge_kernel,
        grid=(n_tok // tm,),
        in_specs=[rows(SSM_WIDTH), rows(GROUP_WIDTH), rows(2 * D_MODEL), rows(D_MODEL),
                  _const_spec((SSM_WIDTH, 2 * SSM_WIDTH)), _const_spec((SSM_WIDTH, D_MODEL)),
                  _const_spec((GROUP_WIDTH, D_MODEL)), _const_spec((D_MODEL, D_MODEL))],
        out_specs=rows(D_MODEL),
        out_shape=jax.ShapeDtypeStruct((n_tok, D_MODEL), F32),
        compiler_params=pltpu.CompilerParams(
            dimension_semantics=("arbitrary",), vmem_limit_bytes=VMEM_LIMIT_BYTES),
        name="merge",
    )(y_gelu, o_attn, gates, x, w_glu, w_ssm, w_attn, w_out)


def kernel(x, ffn1_norm, ffn1_w_gate, ffn1_w_up, ffn1_w_down, mix_norm, w_in, gate_bias,
           rel_bias_table, ssm_a_re, ssm_a_im, ssm_log_dt, ssm_b_re, ssm_b_im, ssm_c_re,
           ssm_c_im, ssm_d, ssm_w_glu, w_attn_branch, w_ssm_branch, w_out, ffn2_norm,
           ffn2_w_gate, ffn2_w_up, ffn2_w_down, final_norm):
    batch, seq, _ = x.shape
    depth = ffn1_norm.shape[0]
    h = x.reshape(batch * seq, D_MODEL)
    for l in range(depth):
        last = l == depth - 1
        h = _ffn(h, ffn1_norm[l], ffn1_w_gate[l].astype(BF16), ffn1_w_up[l].astype(BF16),
                 ffn1_w_down[l].astype(BF16))
        q, k, v, u, gates = _in_proj(h, mix_norm[l], w_in[l].astype(BF16), gate_bias[l], batch, seq)
        o_attn = _attention(q, k, v, rel_bias_table, batch, seq)
        mats = _s5_matrices(ssm_a_re[l], ssm_a_im[l], ssm_log_dt[l], ssm_b_re[l], ssm_b_im[l],
                            ssm_c_re[l], ssm_c_im[l], ssm_d[l])
        y_gelu = _s5(u, mats, batch, seq)
        h = _merge(y_gelu, o_attn, gates, h, ssm_w_glu[l].astype(BF16), w_ssm_branch[l].astype(BF16),
                   w_attn_branch[l].astype(BF16), w_out[l].astype(BF16))
        h = _ffn(h, ffn2_norm[l], ffn2_w_gate[l].astype(BF16), ffn2_w_up[l].astype(BF16),
                 ffn2_w_down[l].astype(BF16), final_g=final_norm if last else None)
    if depth == 0:
        h = _rms(h, final_norm)
    return h.reshape(batch, seq, D_MODEL)
```

```python
import functools
import math

import jax
import jax.numpy as jnp
import numpy as np
from jax import lax
from jax.experimental import pallas as pl
from jax.experimental.pallas import tpu as pltpu

F32 = jnp.float32
BF16 = jnp.bfloat16

D_MODEL = 1024
D_FF = 2816
EPS = 1e-6
HEAD_DIM = 64
HEADS_PER_GROUP = 4
GROUP_WIDTH = HEADS_PER_GROUP * HEAD_DIM
DILATIONS = (1, 4, 16)
WINDOWS = (128, 512, 2048)
N_GROUPS = len(DILATIONS)
ATTN_WIDTH = N_GROUPS * GROUP_WIDTH
BLOCK = 128
N_BUCKETS = 32
MAX_DISTANCE = 2048
NEG_INF = -1e30
SSM_GROUP = 16
SSM_WIDTH = 512
SSM_GROUPS = SSM_WIDTH // SSM_GROUP
SSM_STATE = 64
IN_WIDTH = 3 * ATTN_WIDTH + SSM_WIDTH + 2 * D_MODEL

LANES = 128
MXU_WIDTH = 256
VMEM_LIMIT_BYTES = 56 * 1024 * 1024

FFN_TILE = 512
PROJ_TILE = 512
ATTN_TILE = BLOCK * DILATIONS[-1]
SSM_CHUNK = 16
SSM_OCTET = LANES // SSM_GROUP
N_OCTETS = SSM_GROUPS // SSM_OCTET
OCTET_STATE = SSM_OCTET * SSM_STATE


def _rms(x, g):
    return x * lax.rsqrt(jnp.mean(x * x, axis=-1, keepdims=True) + EPS) * g


def _const_spec(shape):
    return pl.BlockSpec(shape, lambda *_: (0,) * len(shape), pipeline_mode=pl.Buffered(1))


def _swiglu_residual(x, g, wg_ref, wu_ref, wd_ref):
    h = _rms(x, g).astype(BF16)
    acc = jnp.zeros(x.shape, F32)
    for c in range(D_FF // MXU_WIDTH):
        sl = slice(c * MXU_WIDTH, (c + 1) * MXU_WIDTH)
        a = jnp.dot(h, wg_ref[:, sl], preferred_element_type=F32)
        b = jnp.dot(h, wu_ref[:, sl], preferred_element_type=F32)
        t = (a * jax.nn.sigmoid(a) * b).astype(BF16)
        acc = acc + jnp.dot(t, wd_ref[sl, :], preferred_element_type=F32)
    return x + 0.5 * acc


def _ffn_kernel(x_ref, g_ref, wg_ref, wu_ref, wd_ref, o_ref):
    o_ref[...] = _swiglu_residual(x_ref[...], g_ref[...], wg_ref, wu_ref, wd_ref)


def _ffn_final_kernel(x_ref, g_ref, wg_ref, wu_ref, wd_ref, fn_ref, o_ref):
    y = _swiglu_residual(x_ref[...], g_ref[...], wg_ref, wu_ref, wd_ref)
    o_ref[...] = _rms(y, fn_ref[...])


def _ffn(x, g, wg, wu, wd, final_g=None):
    n_tok = x.shape[0]
    tm = FFN_TILE
    tile = pl.BlockSpec((tm, D_MODEL), lambda i: (i, 0))
    in_specs = [tile, _const_spec((1, D_MODEL)), _const_spec((D_MODEL, D_FF)),
                _const_spec((D_MODEL, D_FF)), _const_spec((D_FF, D_MODEL))]
    args = [x, g.reshape(1, D_MODEL), wg, wu, wd]
    kern = _ffn_kernel
    if final_g is not None:
        in_specs.append(_const_spec((1, D_MODEL)))
        args.append(final_g.reshape(1, D_MODEL))
        kern = _ffn_final_kernel
    return pl.pallas_call(
        kern,
        grid=(n_tok // tm,),
        in_specs=in_specs,
        out_specs=tile,
        out_shape=jax.ShapeDtypeStruct((n_tok, D_MODEL), F32),
        compiler_params=pltpu.CompilerParams(
            dimension_semantics=("arbitrary",), vmem_limit_bytes=VMEM_LIMIT_BYTES),
        name="ffn_final" if final_g is not None else "ffn",
    )(*args)


N_QKV_CHUNKS = 3 * N_GROUPS
N_U_CHUNKS = SSM_WIDTH // MXU_WIDTH
N_GATE_CHUNKS = 2 * D_MODEL // MXU_WIDTH
N_STRIDED = 3 * (N_GROUPS - 1)


def _inproj_kernel(x_ref, g_ref, w_ref, gb_ref, *refs):
    qkv_refs = refs[:N_QKV_CHUNKS]
    u_ref, gate_ref, slab_ref = refs[N_QKV_CHUNKS:]
    tm = x_ref.shape[0]
    h = _rms(x_ref[...], g_ref[...]).astype(BF16)
    strided_idx = 0
    for c in range(IN_WIDTH // MXU_WIDTH):
        z = jnp.dot(h, w_ref[:, c * MXU_WIDTH:(c + 1) * MXU_WIDTH], preferred_element_type=F32)
        if c < N_QKV_CHUNKS:
            kind, grp = divmod(c, N_GROUPS)
            if kind == 0:
                z = z * (HEAD_DIM ** -0.5)
            d = DILATIONS[grp]
            o_ref = qkv_refs[c]
            if d == 1:
                o_ref[0, 0] = z.astype(BF16)
            else:
                base = 2 * strided_idx
                strided_idx += 1
                for s in range(2):
                    slab_ref[base + s] = z[:, s * LANES:(s + 1) * LANES]
                for r in range(d):
                    for s in range(2):
                        o_ref[0, r, :, s * LANES:(s + 1) * LANES] = (
                            slab_ref[base + s, pl.ds(r, tm // d, stride=d), :].astype(BF16))
        elif c < N_QKV_CHUNKS + N_U_CHUNKS:
            j = c - N_QKV_CHUNKS
            u_ref[:, j * MXU_WIDTH:(j + 1) * MXU_WIDTH] = z
        else:
            j = c - N_QKV_CHUNKS - N_U_CHUNKS
            sl = slice(j * MXU_WIDTH, (j + 1) * MXU_WIDTH)
            gate_ref[:, sl] = jax.nn.sigmoid(z + gb_ref[:, sl]).astype(BF16)


def _in_proj(x, g, w_in, gate_bias, batch, seq):
    n_tok = x.shape[0]
    tm = PROJ_TILE
    tiles_per_seq = seq // tm
    out_shapes, out_specs = [], []
    for _ in range(3):
        for d in DILATIONS:
            out_shapes.append(jax.ShapeDtypeStruct((batch, d, seq // d, GROUP_WIDTH), BF16))
            out_specs.append(pl.BlockSpec(
                (1, d, tm // d, GROUP_WIDTH),
                lambda i: (i // tiles_per_seq, 0, i % tiles_per_seq, 0)))
    out_shapes.append(jax.ShapeDtypeStruct((n_tok, SSM_WIDTH), F32))
    out_specs.append(pl.BlockSpec((tm, SSM_WIDTH), lambda i: (i, 0)))
    out_shapes.append(jax.ShapeDtypeStruct((n_tok, 2 * D_MODEL), BF16))
    out_specs.append(pl.BlockSpec((tm, 2 * D_MODEL), lambda i: (i, 0)))
    outs = pl.pallas_call(
        _inproj_kernel,
        grid=(n_tok // tm,),
        in_specs=[pl.BlockSpec((tm, D_MODEL), lambda i: (i, 0)),
                  _const_spec((1, D_MODEL)),
                  _const_spec((D_MODEL, IN_WIDTH)),
                  _const_spec((1, 2 * D_MODEL))],
        out_specs=out_specs,
        out_shape=out_shapes,
        scratch_shapes=[pltpu.VMEM((2 * N_STRIDED, tm, LANES), F32)],
        compiler_params=pltpu.CompilerParams(
            dimension_semantics=("arbitrary",), vmem_limit_bytes=VMEM_LIMIT_BYTES),
        name="in_proj",
    )(x, g.reshape(1, D_MODEL), w_in, gate_bias.reshape(1, 2 * D_MODEL))
    q, k, v = outs[0:3], outs[3:6], outs[6:9]
    return q, k, v, outs[9], outs[10]


def _t5_bucket_np(dist):
    max_exact = N_BUCKETS // 2
    d = np.maximum(dist, 1).astype(np.float32)
    ratio = np.log(d / np.float32(max_exact)) / np.float32(math.log(MAX_DISTANCE / max_exact))
    large = max_exact + (ratio * np.float32(N_BUCKETS - max_exact)).astype(np.int32)
    large = np.minimum(large, N_BUCKETS - 1)
    return np.where(dist < max_exact, dist, large)


def _attn_bias(rel_bias_table):
    qi = np.arange(BLOCK)[:, None]
    kj = np.arange(2 * BLOCK)[None, :]
    steps = qi + BLOCK - kj
    full, first = [], []
    for grp, (window, d) in enumerate(zip(WINDOWS, DILATIONS)):
        band = (steps >= 0) & (steps <= window // d)
        bucket = _t5_bucket_np(np.maximum(steps, 0) * d)
        tab = rel_bias_table[:, grp * HEADS_PER_GROUP:(grp + 1) * HEADS_PER_GROUP].astype(F32)
        bias = jnp.transpose(tab[bucket], (2, 0, 1))
        b_full = jnp.where(band[None], bias, NEG_INF).reshape(HEADS_PER_GROUP * BLOCK, 2 * BLOCK)
        b_first = jnp.where((band & (kj >= BLOCK))[None], bias, NEG_INF).reshape(
            HEADS_PER_GROUP * BLOCK, 2 * BLOCK)
        full.append(b_full)
        first.append(b_first)
    return full, first


def _attn_block(q_blk, k_blk, v_blk, bias):
    lane_head = lax.broadcasted_iota(jnp.int32, (BLOCK, GROUP_WIDTH), 1) // HEAD_DIM
    zero = jnp.zeros_like(q_blk)
    qs = jnp.concatenate(
        [jnp.where(lane_head == h, q_blk, zero) for h in range(HEADS_PER_GROUP)], axis=0)
    logits = lax.dot_general(qs, k_blk, (((1,), (1,)), ((), ())), preferred_element_type=F32) + bias
    m = jnp.max(logits, axis=-1, keepdims=True)
    p = jnp.exp(logits - m)
    l = jnp.sum(p, axis=-1, keepdims=True)
    pv = jnp.dot(p.astype(BF16), v_blk, preferred_element_type=F32)
    o = jnp.zeros((BLOCK, GROUP_WIDTH), F32)
    den = jnp.ones((BLOCK, GROUP_WIDTH), F32)
    mx = jnp.zeros((BLOCK, GROUP_WIDTH), F32)
    for h in range(HEADS_PER_GROUP):
        rows = slice(h * BLOCK, (h + 1) * BLOCK)
        sel = lane_head == h
        o = jnp.where(sel, pv[rows], o)
        den = jnp.where(sel, l[rows], den)
        mx = jnp.where(sel, m[rows], mx)
    return o / den, mx + jnp.log(den)


def _attn_kernel(*refs):
    n = N_GROUPS
    q_refs, kc_refs, kp_refs = refs[0:n], refs[n:2 * n], refs[2 * n:3 * n]
    vc_refs, vp_refs = refs[3 * n:4 * n], refs[4 * n:5 * n]
    bias_refs, biasf_refs = refs[5 * n:6 * n], refs[6 * n:7 * n]
    o_ref, o_scr, l_scr = refs[7 * n:]
    tile = pl.program_id(1)

    for grp, d in enumerate(DILATIONS):
        q_ref, kc_ref, kp_ref = q_refs[grp], kc_refs[grp], kp_refs[grp]
        vc_ref, vp_ref = vc_refs[grp], vp_refs[grp]
        bias_ref, biasf_ref = bias_refs[grp], biasf_refs[grp]
        blocks_per_residue = ATTN_TILE // d // BLOCK

        def store(r, blk, o, lse, grp=grp, d=d):
            start = blk * (BLOCK * d) + r
            for s in range(2):
                rows = pl.ds(start, BLOCK, stride=d) if d > 1 else pl.ds(start, BLOCK)
                o_scr[grp, s, rows, :] = o[:, s * LANES:(s + 1) * LANES]
                l_scr[grp, s, rows, :] = lse[:, s * LANES:(s + 1) * LANES]

        def residue(r, carry, q_ref=q_ref, kc_ref=kc_ref, kp_ref=kp_ref, vc_ref=vc_ref,
                    vp_ref=vp_ref, bias_ref=bias_ref, biasf_ref=biasf_ref, store=store,
                    blocks_per_residue=blocks_per_residue):
            k0 = jnp.concatenate([kp_ref[0, r], kc_ref[0, r, 0:BLOCK, :]], axis=0)
            v0 = jnp.concatenate([vp_ref[0, r], vc_ref[0, r, 0:BLOCK, :]], axis=0)
            bias0 = jnp.where(tile == 0, biasf_ref[...], bias_ref[...])
            o, lse = _attn_block(q_ref[0, r, 0:BLOCK, :], k0, v0, bias0)
            store(r, 0, o, lse)

            def block(blk, c):
                kv_rows = pl.ds(pl.multiple_of((blk - 1) * BLOCK, BLOCK), 2 * BLOCK)
                q_rows = pl.ds(pl.multiple_of(blk * BLOCK, BLOCK), BLOCK)
                o, lse = _attn_block(q_ref[0, r, q_rows, :], kc_ref[0, r, kv_rows, :],
                                     vc_ref[0, r, kv_rows, :], bias_ref[...])
                store(r, blk, o, lse)
                return c

            if blocks_per_residue > 1:
                lax.fori_loop(1, blocks_per_residue, block, 0)
            return carry

        if d > 1:
            lax.fori_loop(0, d, residue, 0)
        else:
            residue(0, 0)

    merge_rows = 256

    def merge(c, carry):
        rows = pl.ds(pl.multiple_of(c * merge_rows, merge_rows), merge_rows)
        for s in range(2):
            lses = [l_scr[grp, s, rows, :] for grp in range(N_GROUPS)]
            mx = functools.reduce(jnp.maximum, lses)
            es = [jnp.exp(x - mx) for x in lses]
            num = sum(e * o_scr[grp, s, rows, :] for grp, e in enumerate(es))
            o_ref[rows, s * LANES:(s + 1) * LANES] = (num / sum(es)).astype(BF16)
        return carry

    lax.fori_loop(0, ATTN_TILE // merge_rows, merge, 0)


def _attention(q, k, v, rel_bias_table, batch, seq):
    tiles = seq // ATTN_TILE
    bias_full, bias_first = _attn_bias(rel_bias_table)

    def cur_spec(d):
        return pl.BlockSpec((1, d, ATTN_TILE // d, GROUP_WIDTH), lambda b, j: (b, 0, j, 0))

    def prev_spec(d):
        per_tile = ATTN_TILE // d // BLOCK
        return pl.BlockSpec((1, d, BLOCK, GROUP_WIDTH),
                            lambda b, j: (b, 0, jnp.maximum(j * per_tile - 1, 0), 0))

    bias_shape = (HEADS_PER_GROUP * BLOCK, 2 * BLOCK)
    in_specs = ([cur_spec(d) for d in DILATIONS] + [cur_spec(d) for d in DILATIONS]
                + [prev_spec(d) for d in DILATIONS] + [cur_spec(d) for d in DILATIONS]
                + [prev_spec(d) for d in DILATIONS] + [_const_spec(bias_shape)] * (2 * N_GROUPS))
    return pl.pallas_call(
        _attn_kernel,
        grid=(batch, tiles),
        in_specs=in_specs,
        out_specs=pl.BlockSpec((ATTN_TILE, GROUP_WIDTH), lambda b, j: (b * tiles + j, 0)),
        out_shape=jax.ShapeDtypeStruct((batch * seq, GROUP_WIDTH), BF16),
        scratch_shapes=[pltpu.VMEM((N_GROUPS, 2, ATTN_TILE, LANES), F32),
                        pltpu.VMEM((N_GROUPS, 2, ATTN_TILE, LANES), F32)],
        compiler_params=pltpu.CompilerParams(
            dimension_semantics=("arbitrary", "arbitrary"), vmem_limit_bytes=VMEM_LIMIT_BYTES),
        name="dilated_attn",
    )(*q, *k, *k, *v, *v, *bias_full, *bias_first)


def _octet_mask(row_group, col_group, rows, cols):
    r = np.arange(rows)[:, None] // row_group
    c = np.arange(cols)[None, :] // col_group
    return jnp.asarray(r == c)


def _s5_matrices(a_re, a_im, log_dt, b_re, b_im, c_re, c_im, d_skip):
    hi = lax.Precision.HIGHEST
    tc = SSM_CHUNK
    lam_re = a_re.astype(F32)
    lam_im = a_im.astype(F32)
    dt = jnp.exp(log_dt.astype(F32))[:, None]
    mag = jnp.exp(lam_re * dt)
    ab_re = mag * jnp.cos(lam_im * dt)
    ab_im = mag * jnp.sin(lam_im * dt)
    den = lam_re * lam_re + lam_im * lam_im
    xr = ab_re - 1.0
    coef_re = (xr * lam_re + ab_im * lam_im) / den
    coef_im = (ab_im * lam_re - xr * lam_im) / den
    br = b_re.astype(F32)
    bi = b_im.astype(F32)
    bb_re = coef_re[..., None] * br - coef_im[..., None] * bi
    bb_im = coef_re[..., None] * bi + coef_im[..., None] * br
    cr = c_re.astype(F32)
    ci = c_im.astype(F32)

    def a_pow(k):
        kk = k.astype(F32)[:, None, None]
        pm = jnp.exp(lam_re * dt * kk)
        return pm * jnp.cos(lam_im * dt * kk), pm * jnp.sin(lam_im * dt * kk)

    lags = jnp.arange(tc)
    pr, pi = a_pow(lags)
    abr = pr[..., None] * bb_re - pi[..., None] * bb_im
    abi = pr[..., None] * bb_im + pi[..., None] * bb_re
    kern = (jnp.einsum('gon,lgnc->lgoc', cr, abr, precision=hi)
            - jnp.einsum('gon,lgnc->lgoc', ci, abi, precision=hi))
    skip = d_skip.astype(F32).reshape(SSM_GROUPS, SSM_GROUP)
    kern = kern.at[0].add(skip[:, :, None] * jnp.eye(SSM_GROUP, dtype=F32))

    kt = jnp.transpose(kern, (0, 1, 3, 2)).reshape(tc, N_OCTETS, LANES, SSM_GROUP)
    kt = jnp.transpose(kt, (1, 0, 2, 3))
    kblk = jnp.where(_octet_mask(SSM_GROUP, SSM_GROUP, LANES, LANES),
                     jnp.tile(kt, (1, 1, 1, SSM_OCTET)), 0.0)

    def p_half(ab):
        src = ab[::-1].reshape(tc, N_OCTETS, SSM_OCTET, SSM_STATE, SSM_GROUP)
        src = jnp.transpose(src, (1, 0, 2, 4, 3)).reshape(N_OCTETS, tc, LANES, SSM_STATE)
        return jnp.where(_octet_mask(SSM_GROUP, SSM_STATE, LANES, OCTET_STATE),
                         jnp.tile(src, (1, 1, 1, SSM_OCTET)), 0.0)

    pmat = jnp.concatenate([p_half(abr), p_half(abi)], axis=-1).reshape(
        N_OCTETS, tc * LANES, 2 * OCTET_STATE)

    qr_pow, qi_pow = a_pow(lags + 1)
    q_sr = cr[None] * qr_pow[:, :, None, :] - ci[None] * qi_pow[:, :, None, :]
    q_si = -(cr[None] * qi_pow[:, :, None, :] + ci[None] * qr_pow[:, :, None, :])

    def q_half(qq):
        src = qq.reshape(tc, N_OCTETS, SSM_OCTET, SSM_GROUP, SSM_STATE)
        src = jnp.transpose(src, (1, 2, 4, 0, 3)).reshape(N_OCTETS, OCTET_STATE, tc, SSM_GROUP)
        mask = _octet_mask(SSM_STATE, SSM_GROUP, OCTET_STATE, LANES)[None, :, None, :]
        return jnp.where(mask, jnp.tile(src, (1, 1, 1, SSM_OCTET)), 0.0).reshape(
            N_OCTETS, OCTET_STATE, tc * LANES)

    qmat = jnp.concatenate([q_half(q_sr), q_half(q_si)], axis=1)

    ac_re, ac_im = a_pow(jnp.array([tc]))
    a_chunk = jnp.concatenate([ac_re.reshape(N_OCTETS, 1, OCTET_STATE),
                               ac_im.reshape(N_OCTETS, 1, OCTET_STATE)], axis=-1)
    return kblk.astype(BF16), pmat.astype(BF16), qmat.astype(BF16), a_chunk


def _s5_kernel(u_ref, kblk_ref, p_ref, q_ref, a_ref, o_ref, k_scr, x_scr, s_scr, z_scr, y_scr):
    n_chunks = x_scr.shape[0]
    tc = SSM_CHUNK

    @pl.when(pl.program_id(1) == 0)
    def _():
        k_scr[...] = jnp.zeros(k_scr.shape, BF16)
        for s in range(tc):
            for t in range(s, tc):
                k_scr[s * LANES:(s + 1) * LANES, t * LANES:(t + 1) * LANES] = kblk_ref[0, t - s]

    for t in range(tc):
        x_scr[:, t * LANES:(t + 1) * LANES] = u_ref[pl.ds(t, n_chunks, stride=tc), :].astype(BF16)
    s_scr[...] = jnp.dot(x_scr[...], p_ref[0], preferred_element_type=F32)

    a_re = a_ref[0, :, 0:OCTET_STATE]
    a_im = a_ref[0, :, OCTET_STATE:2 * OCTET_STATE]

    def step(j, carry):
        c_re, c_im = carry
        row = pl.ds(j, 1)
        z_scr[row, 0:OCTET_STATE] = c_re
        z_scr[row, OCTET_STATE:2 * OCTET_STATE] = c_im
        inc_re = s_scr[row, 0:OCTET_STATE]
        inc_im = s_scr[row, OCTET_STATE:2 * OCTET_STATE]
        return (a_re * c_re - a_im * c_im + inc_re, a_re * c_im + a_im * c_re + inc_im)

    zero = jnp.zeros((1, OCTET_STATE), F32)
    lax.fori_loop(0, n_chunks, step, (zero, zero))

    zb = z_scr[...].astype(BF16)
    for pair in range(tc // 2):
        cols = slice(pair * MXU_WIDTH, (pair + 1) * MXU_WIDTH)
        live = (2 * pair + 2) * LANES
        y = (jnp.dot(x_scr[:, 0:live], k_scr[0:live, cols], preferred_element_type=F32)
             + jnp.dot(zb, q_ref[0, :, cols], preferred_element_type=F32))
        y_scr[pl.ds(2 * pair, n_chunks, stride=tc), :] = y[:, 0:LANES]
        y_scr[pl.ds(2 * pair + 1, n_chunks, stride=tc), :] = y[:, LANES:2 * LANES]
    o_ref[...] = jax.nn.gelu(y_scr[...]).astype(BF16)


def _s5(u, mats, batch, seq):
    kblk, pmat, qmat, a_chunk = mats
    n_chunks = seq // SSM_CHUNK
    width = SSM_CHUNK * LANES

    def per_octet(shape):
        return pl.BlockSpec((1,) + shape, lambda o, b: (o,) + (0,) * len(shape),
                            pipeline_mode=pl.Buffered(1))

    return pl.pallas_call(
        _s5_kernel,
        grid=(N_OCTETS, batch),
        in_specs=[pl.BlockSpec((seq, LANES), lambda o, b: (b, o)),
                  per_octet((SSM_CHUNK, LANES, LANES)),
                  per_octet((width, 2 * OCTET_STATE)),
                  per_octet((2 * OCTET_STATE, width)),
                  per_octet((1, 2 * OCTET_STATE))],
        out_specs=pl.BlockSpec((seq, LANES), lambda o, b: (b, o)),
        out_shape=jax.ShapeDtypeStruct((batch * seq, SSM_WIDTH), BF16),
        scratch_shapes=[pltpu.VMEM((width, width), BF16),
                        pltpu.VMEM((n_chunks, width), BF16),
                        pltpu.VMEM((n_chunks, 2 * OCTET_STATE), F32),
                        pltpu.VMEM((n_chunks, 2 * OCTET_STATE), F32),
                        pltpu.VMEM((seq, LANES), F32)],
        compiler_params=pltpu.CompilerParams(
            dimension_semantics=("arbitrary", "arbitrary"), vmem_limit_bytes=VMEM_LIMIT_BYTES),
        name="s5_mixer",
    )(u, kblk, pmat, qmat, a_chunk)


def _merge_kernel(y_ref, oa_ref, gate_ref, x_ref, wglu_ref, wssm_ref, wattn_ref, wout_ref, o_ref):
    glu = jnp.dot(y_ref[...], wglu_ref[...], preferred_element_type=F32)
    ys = (glu[:, 0:SSM_WIDTH] * jax.nn.sigmoid(glu[:, SSM_WIDTH:2 * SSM_WIDTH])).astype(BF16)
    y_ssm = jnp.dot(ys, wssm_ref[...], preferred_element_type=F32)
    y_attn = jnp.dot(oa_ref[...], wattn_ref[...], preferred_element_type=F32)
    g_attn = gate_ref[:, 0:D_MODEL].astype(F32)
    g_ssm = gate_ref[:, D_MODEL:2 * D_MODEL].astype(F32)
    mixed = (g_attn * y_attn + g_ssm * y_ssm).astype(BF16)
    o_ref[...] = x_ref[...] + jnp.dot(mixed, wout_ref[...], preferred_element_type=F32)


def _merge(y_gelu, o_attn, gates, x, w_glu, w_ssm, w_attn, w_out):
    n_tok = x.shape[0]
    tm = PROJ_TILE

    def rows(width):
        return pl.BlockSpec((tm, width), lambda i: (i, 0))

    return pl.pallas_call(
        _merge_kernel,
        grid=(n_tok // tm,),
        in_specs=[rows(SSM_WIDTH), rows(GROUP_WIDTH), rows(2 * D_MODEL), rows(D_MODEL),
                  _const_spec((SSM_WIDTH, 2 * SSM_WIDTH)), _const_spec((SSM_WIDTH, D_MODEL)),
                  _const_spec((GROUP_WIDTH, D_MODEL)), _const_spec((D_MODEL, D_MODEL))],
        out_specs=rows(D_MODEL),
        out_shape=jax.ShapeDtypeStruct((n_tok, D_MODEL), F32),
        compiler_params=pltpu.CompilerParams(
            dimension_semantics=("arbitrary",), vmem_limit_bytes=VMEM_LIMIT_BYTES),
        name="merge",
    )(y_gelu, o_attn, gates, x, w_glu, w_ssm, w_attn, w_out)


def kernel(x, ffn1_norm, ffn1_w_gate, ffn1_w_up, ffn1_w_down, mix_norm, w_in, gate_bias,
           rel_bias_table, ssm_a_re, ssm_a_im, ssm_log_dt, ssm_b_re, ssm_b_im, ssm_c_re,
           ssm_c_im, ssm_d, ssm_w_glu, w_attn_branch, w_ssm_branch, w_out, ffn2_norm,
           ffn2_w_gate, ffn2_w_up, ffn2_w_down, final_norm):
    batch, seq, _ = x.shape
    depth = ffn1_norm.shape[0]
    h = x.reshape(batch * seq, D_MODEL)
    for l in range(depth):
        last = l == depth - 1
        h = _ffn(h, ffn1_norm[l], ffn1_w_gate[l].astype(BF16), ffn1_w_up[l].astype(BF16),
                 ffn1_w_down[l].astype(BF16))
        q, k, v, u, gates = _in_proj(h, mix_norm[l], w_in[l].astype(BF16), gate_bias[l], batch, seq)
        o_attn = _attention(q, k, v, rel_bias_table, batch, seq)
        mats = _s5_matrices(ssm_a_re[l], ssm_a_im[l], ssm_log_dt[l], ssm_b_re[l], ssm_b_im[l],
                            ssm_c_re[l], ssm_c_im[l], ssm_d[l])
        y_gelu = _s5(u, mats, batch, seq)
        h = _merge(y_gelu, o_attn, gates, h, ssm_w_glu[l].astype(BF16), w_ssm_branch[l].astype(BF16),
                   w_attn_branch[l].astype(BF16), w_out[l].astype(BF16))
        h = _ffn(h, ffn2_norm[l], ffn2_w_gate[l].astype(BF16), ffn2_w_up[l].astype(BF16),
                 ffn2_w_down[l].astype(BF16), final_g=final_norm if last else None)
    if depth == 0:
        h = _rms(h, final_norm)
    return h.reshape(batch, seq, D_MODEL)
```

```python
import functools
import math

import jax
import jax.numpy as jnp
import numpy as np
from jax import lax
from jax.experimental import pallas as pl
from jax.experimental.pallas import tpu as pltpu

F32 = jnp.float32
BF16 = jnp.bfloat16

D_MODEL = 1024
D_FF = 2816
EPS = 1e-6
HEAD_DIM = 64
HEADS_PER_GROUP = 4
GROUP_WIDTH = HEADS_PER_GROUP * HEAD_DIM
DILATIONS = (1, 4, 16)
WINDOWS = (128, 512, 2048)
N_GROUPS = len(DILATIONS)
ATTN_WIDTH = N_GROUPS * GROUP_WIDTH
BLOCK = 128
N_BUCKETS = 32
MAX_DISTANCE = 2048
NEG_INF = -1e30
SSM_GROUP = 16
SSM_WIDTH = 512
SSM_GROUPS = SSM_WIDTH // SSM_GROUP
SSM_STATE = 64
IN_WIDTH = 3 * ATTN_WIDTH + SSM_WIDTH + 2 * D_MODEL

LANES = 128
MXU_WIDTH = 256
VMEM_LIMIT_BYTES = 56 * 1024 * 1024

FFN_TILE = 512
PROJ_TILE = 512
ATTN_TILE = BLOCK * DILATIONS[-1]
ATTN_INTERLEAVE = 4
SSM_CHUNK = 16
SSM_OCTET = LANES // SSM_GROUP
N_OCTETS = SSM_GROUPS // SSM_OCTET
OCTET_STATE = SSM_OCTET * SSM_STATE


def _rms(x, g):
    return x * lax.rsqrt(jnp.mean(x * x, axis=-1, keepdims=True) + EPS) * g


def _const_spec(shape):
    return pl.BlockSpec(shape, lambda *_: (0,) * len(shape), pipeline_mode=pl.Buffered(1))


def _swiglu_residual(x, g, wg_ref, wu_ref, wd_ref):
    h = _rms(x, g).astype(BF16)
    acc = jnp.zeros(x.shape, F32)
    for c in range(D_FF // MXU_WIDTH):
        sl = slice(c * MXU_WIDTH, (c + 1) * MXU_WIDTH)
        a = jnp.dot(h, wg_ref[:, sl], preferred_element_type=F32)
        b = jnp.dot(h, wu_ref[:, sl], preferred_element_type=F32)
        t = (a * jax.nn.sigmoid(a) * b).astype(BF16)
        acc = acc + jnp.dot(t, wd_ref[sl, :], preferred_element_type=F32)
    return x + 0.5 * acc


def _ffn_kernel(x_ref, g_ref, wg_ref, wu_ref, wd_ref, o_ref):
    o_ref[...] = _swiglu_residual(x_ref[...], g_ref[...], wg_ref, wu_ref, wd_ref)


def _ffn_final_kernel(x_ref, g_ref, wg_ref, wu_ref, wd_ref, fn_ref, o_ref):
    y = _swiglu_residual(x_ref[...], g_ref[...], wg_ref, wu_ref, wd_ref)
    o_ref[...] = _rms(y, fn_ref[...])


def _ffn(x, g, wg, wu, wd, final_g=None):
    n_tok = x.shape[0]
    tm = FFN_TILE
    tile = pl.BlockSpec((tm, D_MODEL), lambda i: (i, 0))
    in_specs = [tile, _const_spec((1, D_MODEL)), _const_spec((D_MODEL, D_FF)),
                _const_spec((D_MODEL, D_FF)), _const_spec((D_FF, D_MODEL))]
    args = [x, g.reshape(1, D_MODEL), wg, wu, wd]
    kern = _ffn_kernel
    if final_g is not None:
        in_specs.append(_const_spec((1, D_MODEL)))
        args.append(final_g.reshape(1, D_MODEL))
        kern = _ffn_final_kernel
    return pl.pallas_call(
        kern,
        grid=(n_tok // tm,),
        in_specs=in_specs,
        out_specs=tile,
        out_shape=jax.ShapeDtypeStruct((n_tok, D_MODEL), F32),
        compiler_params=pltpu.CompilerParams(
            dimension_semantics=("arbitrary",), vmem_limit_bytes=VMEM_LIMIT_BYTES),
        name="ffn_final" if final_g is not None else "ffn",
    )(*args)


N_QKV_CHUNKS = 3 * N_GROUPS
N_U_CHUNKS = SSM_WIDTH // MXU_WIDTH
N_GATE_CHUNKS = 2 * D_MODEL // MXU_WIDTH
N_STRIDED = 3 * (N_GROUPS - 1)


def _inproj_kernel(x_ref, g_ref, w_ref, gb_ref, *refs):
    qkv_refs = refs[:N_QKV_CHUNKS]
    u_ref, gate_ref, slab_ref = refs[N_QKV_CHUNKS:]
    tm = x_ref.shape[0]
    h = _rms(x_ref[...], g_ref[...]).astype(BF16)
    strided_idx = 0
    for c in range(IN_WIDTH // MXU_WIDTH):
        z = jnp.dot(h, w_ref[:, c * MXU_WIDTH:(c + 1) * MXU_WIDTH], preferred_element_type=F32)
        if c < N_QKV_CHUNKS:
            kind, grp = divmod(c, N_GROUPS)
            if kind == 0:
                z = z * (HEAD_DIM ** -0.5)
            d = DILATIONS[grp]
            o_ref = qkv_refs[c]
            if d == 1:
                o_ref[0, 0] = z.astype(BF16)
            else:
                base = 2 * strided_idx
                strided_idx += 1
                for s in range(2):
                    slab_ref[base + s] = z[:, s * LANES:(s + 1) * LANES]
                for r in range(d):
                    for s in range(2):
                        o_ref[0, r, :, s * LANES:(s + 1) * LANES] = (
                            slab_ref[base + s, pl.ds(r, tm // d, stride=d), :].astype(BF16))
        elif c < N_QKV_CHUNKS + N_U_CHUNKS:
            j = c - N_QKV_CHUNKS
            u_ref[:, j * MXU_WIDTH:(j + 1) * MXU_WIDTH] = z
        else:
            j = c - N_QKV_CHUNKS - N_U_CHUNKS
            sl = slice(j * MXU_WIDTH, (j + 1) * MXU_WIDTH)
            gate_ref[:, sl] = jax.nn.sigmoid(z + gb_ref[:, sl]).astype(BF16)


def _in_proj(x, g, w_in, gate_bias, batch, seq):
    n_tok = x.shape[0]
    tm = PROJ_TILE
    tiles_per_seq = seq // tm
    out_shapes, out_specs = [], []
    for _ in range(3):
        for d in DILATIONS:
            out_shapes.append(jax.ShapeDtypeStruct((batch, d, seq // d, GROUP_WIDTH), BF16))
            out_specs.append(pl.BlockSpec(
                (1, d, tm // d, GROUP_WIDTH),
                lambda i: (i // tiles_per_seq, 0, i % tiles_per_seq, 0)))
    out_shapes.append(jax.ShapeDtypeStruct((n_tok, SSM_WIDTH), F32))
    out_specs.append(pl.BlockSpec((tm, SSM_WIDTH), lambda i: (i, 0)))
    out_shapes.append(jax.ShapeDtypeStruct((n_tok, 2 * D_MODEL), BF16))
    out_specs.append(pl.BlockSpec((tm, 2 * D_MODEL), lambda i: (i, 0)))
    outs = pl.pallas_call(
        _inproj_kernel,
        grid=(n_tok // tm,),
        in_specs=[pl.BlockSpec((tm, D_MODEL), lambda i: (i, 0)),
                  _const_spec((1, D_MODEL)),
                  _const_spec((D_MODEL, IN_WIDTH)),
                  _const_spec((1, 2 * D_MODEL))],
        out_specs=out_specs,
        out_shape=out_shapes,
        scratch_shapes=[pltpu.VMEM((2 * N_STRIDED, tm, LANES), F32)],
        compiler_params=pltpu.CompilerParams(
            dimension_semantics=("arbitrary",), vmem_limit_bytes=VMEM_LIMIT_BYTES),
        name="in_proj",
    )(x, g.reshape(1, D_MODEL), w_in, gate_bias.reshape(1, 2 * D_MODEL))
    q, k, v = outs[0:3], outs[3:6], outs[6:9]
    return q, k, v, outs[9], outs[10]


def _t5_bucket_np(dist):
    max_exact = N_BUCKETS // 2
    d = np.maximum(dist, 1).astype(np.float32)
    ratio = np.log(d / np.float32(max_exact)) / np.float32(math.log(MAX_DISTANCE / max_exact))
    large = max_exact + (ratio * np.float32(N_BUCKETS - max_exact)).astype(np.int32)
    large = np.minimum(large, N_BUCKETS - 1)
    return np.where(dist < max_exact, dist, large)


def _bucket_tables():
    qi = np.arange(BLOCK)[:, None]
    kj = np.arange(2 * BLOCK)[None, :]
    steps = qi + BLOCK - kj
    tables = []
    for window, d in zip(WINDOWS, DILATIONS):
        band = (steps >= 0) & (steps <= window // d)
        bucket = _t5_bucket_np(np.maximum(steps, 0) * d)
        tables.append(np.where(band, bucket, -1).astype(np.int32))
    return jnp.asarray(np.stack(tables))


def _attn_block(q_blk, k_blk, v_blk, bias):
    lane_head = lax.broadcasted_iota(jnp.int32, (BLOCK, GROUP_WIDTH), 1) // HEAD_DIM
    zero = jnp.zeros_like(q_blk)
    qs = jnp.concatenate(
        [jnp.where(lane_head == h, q_blk, zero) for h in range(HEADS_PER_GROUP)], axis=0)
    logits = lax.dot_general(qs, k_blk, (((1,), (1,)), ((), ())), preferred_element_type=F32) + bias
    m = jnp.max(logits, axis=-1, keepdims=True)
    p = jnp.exp(logits - m)
    l = jnp.sum(p, axis=-1, keepdims=True)
    pv = jnp.dot(p.astype(BF16), v_blk, preferred_element_type=F32)
    o = jnp.zeros((BLOCK, GROUP_WIDTH), F32)
    den = jnp.ones((BLOCK, GROUP_WIDTH), F32)
    mx = jnp.zeros((BLOCK, GROUP_WIDTH), F32)
    for h in range(HEADS_PER_GROUP):
        rows = slice(h * BLOCK, (h + 1) * BLOCK)
        sel = lane_head == h
        o = jnp.where(sel, pv[rows], o)
        den = jnp.where(sel, l[rows], den)
        mx = jnp.where(sel, m[rows], mx)
    return o / den, mx + jnp.log(den)


def _build_bias(tab_ref, bucket_ref, bias_scr):
    own_block = lax.broadcasted_iota(jnp.int32, (BLOCK, 2 * BLOCK), 1) >= BLOCK
    for grp in range(N_GROUPS):
        bucket = bucket_ref[grp]
        for h in range(HEADS_PER_GROUP):
            col = grp * HEADS_PER_GROUP + h

            def pick(b, acc, bucket=bucket, col=col):
                return jnp.where(bucket == b, tab_ref[b, col], acc)

            bias = lax.fori_loop(0, N_BUCKETS, pick, jnp.full((BLOCK, 2 * BLOCK), NEG_INF, F32))
            rows = slice(h * BLOCK, (h + 1) * BLOCK)
            bias_scr[grp, 0, rows, :] = bias
            bias_scr[grp, 1, rows, :] = jnp.where(own_block, bias, NEG_INF)


def _block_aligned(row):
    return row if isinstance(row, int) else pl.multiple_of(row, BLOCK)


def _attn_kernel(*refs):
    n = N_GROUPS
    tab_ref, bucket_ref = refs[0:2]
    refs = refs[2:]
    q_refs, kc_refs, kp_refs = refs[0:n], refs[n:2 * n], refs[2 * n:3 * n]
    vc_refs, vp_refs = refs[3 * n:4 * n], refs[4 * n:5 * n]
    o_ref, bias_scr, o_scr, l_scr = refs[5 * n:]
    tile = pl.program_id(1)

    @pl.when((pl.program_id(0) == 0) & (tile == 0))
    def _():
        _build_bias(tab_ref, bucket_ref, bias_scr)

    for grp, d in enumerate(DILATIONS):
        q_ref, kc_ref, kp_ref = q_refs[grp], kc_refs[grp], kp_refs[grp]
        vc_ref, vp_ref = vc_refs[grp], vp_refs[grp]
        blocks_per_residue = ATTN_TILE // d // BLOCK

        def run_block(r, blk, grp=grp, d=d, q_ref=q_ref, kc_ref=kc_ref, kp_ref=kp_ref,
                      vc_ref=vc_ref, vp_ref=vp_ref):
            if isinstance(blk, int) and blk == 0:
                k_blk = jnp.concatenate([kp_ref[0, r], kc_ref[0, r, 0:BLOCK, :]], axis=0)
                v_blk = jnp.concatenate([vp_ref[0, r], vc_ref[0, r, 0:BLOCK, :]], axis=0)
                bias = jnp.where(tile == 0, bias_scr[grp, 1], bias_scr[grp, 0])
                q_blk = q_ref[0, r, 0:BLOCK, :]
            else:
                kv_rows = pl.ds(_block_aligned((blk - 1) * BLOCK), 2 * BLOCK)
                k_blk = kc_ref[0, r, kv_rows, :]
                v_blk = vc_ref[0, r, kv_rows, :]
                bias = bias_scr[grp, 0]
                q_blk = q_ref[0, r, pl.ds(_block_aligned(blk * BLOCK), BLOCK), :]
            o, lse = _attn_block(q_blk, k_blk, v_blk, bias)
            start = blk * (BLOCK * d) + r
            for s in range(2):
                rows = pl.ds(start, BLOCK, stride=d) if d > 1 else pl.ds(start, BLOCK)
                o_scr[grp, s, rows, :] = o[:, s * LANES:(s + 1) * LANES]
                l_scr[grp, s, rows, :] = lse[:, s * LANES:(s + 1) * LANES]

        if blocks_per_residue == 1:
            def residue_set(i, c, run_block=run_block):
                for k in range(ATTN_INTERLEAVE):
                    run_block(ATTN_INTERLEAVE * i + k, 0)
                return c
            lax.fori_loop(0, d // ATTN_INTERLEAVE, residue_set, 0)
        elif d > 1:
            def residue(r, c, run_block=run_block, blocks_per_residue=blocks_per_residue):
                run_block(r, 0)
                for blk in range(1, blocks_per_residue):
                    run_block(r, blk)
                return c
            lax.fori_loop(0, d, residue, 0)
        else:
            for blk in range(ATTN_INTERLEAVE):
                run_block(0, blk)

            def block_set(i, c, run_block=run_block):
                for k in range(ATTN_INTERLEAVE):
                    run_block(0, ATTN_INTERLEAVE * i + k)
                return c
            lax.fori_loop(1, blocks_per_residue // ATTN_INTERLEAVE, block_set, 0)

    merge_rows = 256

    def merge(c, carry):
        rows = pl.ds(pl.multiple_of(c * merge_rows, merge_rows), merge_rows)
        for s in range(2):
            lses = [l_scr[grp, s, rows, :] for grp in range(N_GROUPS)]
            mx = functools.reduce(jnp.maximum, lses)
            es = [jnp.exp(x - mx) for x in lses]
            num = sum(e * o_scr[grp, s, rows, :] for grp, e in enumerate(es))
            o_ref[rows, s * LANES:(s + 1) * LANES] = (num / sum(es)).astype(BF16)
        return carry

    lax.fori_loop(0, ATTN_TILE // merge_rows, merge, 0)


def _attention(q, k, v, rel_bias_table, batch, seq):
    tiles = seq // ATTN_TILE

    def cur_spec(d):
        return pl.BlockSpec((1, d, ATTN_TILE // d, GROUP_WIDTH), lambda b, j: (b, 0, j, 0))

    def prev_spec(d):
        per_tile = ATTN_TILE // d // BLOCK
        return pl.BlockSpec((1, d, BLOCK, GROUP_WIDTH),
                            lambda b, j: (b, 0, jnp.maximum(j * per_tile - 1, 0), 0))

    in_specs = ([pl.BlockSpec(memory_space=pltpu.SMEM),
                 _const_spec((N_GROUPS, BLOCK, 2 * BLOCK))]
                + [cur_spec(d) for d in DILATIONS] + [cur_spec(d) for d in DILATIONS]
                + [prev_spec(d) for d in DILATIONS] + [cur_spec(d) for d in DILATIONS]
                + [prev_spec(d) for d in DILATIONS])
    return pl.pallas_call(
        _attn_kernel,
        grid=(batch, tiles),
        in_specs=in_specs,
        out_specs=pl.BlockSpec((ATTN_TILE, GROUP_WIDTH), lambda b, j: (b * tiles + j, 0)),
        out_shape=jax.ShapeDtypeStruct((batch * seq, GROUP_WIDTH), BF16),
        scratch_shapes=[pltpu.VMEM((N_GROUPS, 2, HEADS_PER_GROUP * BLOCK, 2 * BLOCK), F32),
                        pltpu.VMEM((N_GROUPS, 2, ATTN_TILE, LANES), F32),
                        pltpu.VMEM((N_GROUPS, 2, ATTN_TILE, LANES), F32)],
        compiler_params=pltpu.CompilerParams(
            dimension_semantics=("arbitrary", "arbitrary"), vmem_limit_bytes=VMEM_LIMIT_BYTES),
        name="dilated_attn",
    )(rel_bias_table.astype(F32), _bucket_tables(), *q, *k, *k, *v, *v)


def _octet_mask(row_group, col_group, rows, cols):
    r = np.arange(rows)[:, None] // row_group
    c = np.arange(cols)[None, :] // col_group
    return jnp.asarray(r == c)


def _s5_matrices(a_re, a_im, log_dt, b_re, b_im, c_re, c_im, d_skip):
    hi = lax.Precision.HIGHEST
    tc = SSM_CHUNK
    lam_re = a_re.astype(F32)
    lam_im = a_im.astype(F32)
    dt = jnp.exp(log_dt.astype(F32))[:, None]
    mag = jnp.exp(lam_re * dt)
    ab_re = mag * jnp.cos(lam_im * dt)
    ab_im = mag * jnp.sin(lam_im * dt)
    den = lam_re * lam_re + lam_im * lam_im
    xr = ab_re - 1.0
    coef_re = (xr * lam_re + ab_im * lam_im) / den
    coef_im = (ab_im * lam_re - xr * lam_im) / den
    br = b_re.astype(F32)
    bi = b_im.astype(F32)
    bb_re = coef_re[..., None] * br - coef_im[..., None] * bi
    bb_im = coef_re[..., None] * bi + coef_im[..., None] * br
    cr = c_re.astype(F32)
    ci = c_im.astype(F32)

    def a_pow(k):
        kk = k.astype(F32)[:, None, None]
        pm = jnp.exp(lam_re * dt * kk)
        return pm * jnp.cos(lam_im * dt * kk), pm * jnp.sin(lam_im * dt * kk)

    lags = jnp.arange(tc)
    pr, pi = a_pow(lags)
    abr = pr[..., None] * bb_re - pi[..., None] * bb_im
    abi = pr[..., None] * bb_im + pi[..., None] * bb_re
    kern = (jnp.einsum('gon,lgnc->lgoc', cr, abr, precision=hi)
            - jnp.einsum('gon,lgnc->lgoc', ci, abi, precision=hi))
    skip = d_skip.astype(F32).reshape(SSM_GROUPS, SSM_GROUP)
    kern = kern.at[0].add(skip[:, :, None] * jnp.eye(SSM_GROUP, dtype=F32))

    kt = jnp.transpose(kern, (0, 1, 3, 2)).reshape(tc, N_OCTETS, LANES, SSM_GROUP)
    kt = jnp.transpose(kt, (1, 0, 2, 3))
    kblk = jnp.where(_octet_mask(SSM_GROUP, SSM_GROUP, LANES, LANES),
                     jnp.tile(kt, (1, 1, 1, SSM_OCTET)), 0.0)

    def p_half(ab):
        src = ab[::-1].reshape(tc, N_OCTETS, SSM_OCTET, SSM_STATE, SSM_GROUP)
        src = jnp.transpose(src, (1, 0, 2, 4, 3)).reshape(N_OCTETS, tc, LANES, SSM_STATE)
        return jnp.where(_octet_mask(SSM_GROUP, SSM_STATE, LANES, OCTET_STATE),
                         jnp.tile(src, (1, 1, 1, SSM_OCTET)), 0.0)

    pmat = jnp.concatenate([p_half(abr), p_half(abi)], axis=-1).reshape(
        N_OCTETS, tc * LANES, 2 * OCTET_STATE)

    qr_pow, qi_pow = a_pow(lags + 1)
    q_sr = cr[None] * qr_pow[:, :, None, :] - ci[None] * qi_pow[:, :, None, :]
    q_si = -(cr[None] * qi_pow[:, :, None, :] + ci[None] * qr_pow[:, :, None, :])

    def q_half(qq):
        src = qq.reshape(tc, N_OCTETS, SSM_OCTET, SSM_GROUP, SSM_STATE)
        src = jnp.transpose(src, (1, 2, 4, 0, 3)).reshape(N_OCTETS, OCTET_STATE, tc, SSM_GROUP)
        mask = _octet_mask(SSM_STATE, SSM_GROUP, OCTET_STATE, LANES)[None, :, None, :]
        return jnp.where(mask, jnp.tile(src, (1, 1, 1, SSM_OCTET)), 0.0).reshape(
            N_OCTETS, OCTET_STATE, tc * LANES)

    qmat = jnp.concatenate([q_half(q_sr), q_half(q_si)], axis=1)

    ac_re, ac_im = a_pow(jnp.array([tc]))
    a_chunk = jnp.concatenate([ac_re.reshape(N_OCTETS, 1, OCTET_STATE),
                               ac_im.reshape(N_OCTETS, 1, OCTET_STATE)], axis=-1)
    return kblk.astype(BF16), pmat.astype(BF16), qmat.astype(BF16), a_chunk


def _s5_kernel(u_ref, kblk_ref, p_ref, q_ref, a_ref, o_ref, k_scr, x_scr, s_scr, z_scr, y_scr):
    n_chunks = x_scr.shape[0]
    tc = SSM_CHUNK

    @pl.when(pl.program_id(1) == 0)
    def _():
        k_scr[...] = jnp.zeros(k_scr.shape, BF16)
        for s in range(tc):
            for t in range(s, tc):
                k_scr[s * LANES:(s + 1) * LANES, t * LANES:(t + 1) * LANES] = kblk_ref[0, t - s]

    for t in range(tc):
        x_scr[:, t * LANES:(t + 1) * LANES] = u_ref[pl.ds(t, n_chunks, stride=tc), :].astype(BF16)
    s_scr[...] = jnp.dot(x_scr[...], p_ref[0], preferred_element_type=F32)

    a_re = a_ref[0, :, 0:OCTET_STATE]
    a_im = a_ref[0, :, OCTET_STATE:2 * OCTET_STATE]

    def step(j, carry):
        c_re, c_im = carry
        row = pl.ds(j, 1)
        z_scr[row, 0:OCTET_STATE] = c_re
        z_scr[row, OCTET_STATE:2 * OCTET_STATE] = c_im
        inc_re = s_scr[row, 0:OCTET_STATE]
        inc_im = s_scr[row, OCTET_STATE:2 * OCTET_STATE]
        return (a_re * c_re - a_im * c_im + inc_re, a_re * c_im + a_im * c_re + inc_im)

    zero = jnp.zeros((1, OCTET_STATE), F32)
    lax.fori_loop(0, n_chunks, step, (zero, zero))

    zb = z_scr[...].astype(BF16)
    for pair in range(tc // 2):
        cols = slice(pair * MXU_WIDTH, (pair + 1) * MXU_WIDTH)
        live = (2 * pair + 2) * LANES
        y = (jnp.dot(x_scr[:, 0:live], k_scr[0:live, cols], preferred_element_type=F32)
             + jnp.dot(zb, q_ref[0, :, cols], preferred_element_type=F32))
        y_scr[pl.ds(2 * pair, n_chunks, stride=tc), :] = y[:, 0:LANES]
        y_scr[pl.ds(2 * pair + 1, n_chunks, stride=tc), :] = y[:, LANES:2 * LANES]
    o_ref[...] = jax.nn.gelu(y_scr[...]).astype(BF16)


def _s5(u, mats, batch, seq):
    kblk, pmat, qmat, a_chunk = mats
    n_chunks = seq // SSM_CHUNK
    width = SSM_CHUNK * LANES

    def per_octet(shape):
        return pl.BlockSpec((1,) + shape, lambda o, b: (o,) + (0,) * len(shape),
                            pipeline_mode=pl.Buffered(1))

    return pl.pallas_call(
        _s5_kernel,
        grid=(N_OCTETS, batch),
        in_specs=[pl.BlockSpec((seq, LANES), lambda o, b: (b, o)),
                  per_octet((SSM_CHUNK, LANES, LANES)),
                  per_octet((width, 2 * OCTET_STATE)),
                  per_octet((2 * OCTET_STATE, width)),
                  per_octet((1, 2 * OCTET_STATE))],
        out_specs=pl.BlockSpec((seq, LANES), lambda o, b: (b, o)),
        out_shape=jax.ShapeDtypeStruct((batch * seq, SSM_WIDTH), BF16),
        scratch_shapes=[pltpu.VMEM((width, width), BF16),
                        pltpu.VMEM((n_chunks, width), BF16),
                        pltpu.VMEM((n_chunks, 2 * OCTET_STATE), F32),
                        pltpu.VMEM((n_chunks, 2 * OCTET_STATE), F32),
                        pltpu.VMEM((seq, LANES), F32)],
        compiler_params=pltpu.CompilerParams(
            dimension_semantics=("arbitrary", "arbitrary"), vmem_limit_bytes=VMEM_LIMIT_BYTES),
        name="s5_mixer",
    )(u, kblk, pmat, qmat, a_chunk)


def _merge_kernel(y_ref, oa_ref, gate_ref, x_ref, wglu_ref, wssm_ref, wattn_ref, wout_ref, o_ref):
    glu = jnp.dot(y_ref[...], wglu_ref[...], preferred_element_type=F32)
    ys = (glu[:, 0:SSM_WIDTH] * jax.nn.sigmoid(glu[:, SSM_WIDTH:2 * SSM_WIDTH])).astype(BF16)
    y_ssm = jnp.dot(ys, wssm_ref[...], preferred_element_type=F32)
    y_attn = jnp.dot(oa_ref[...], wattn_ref[...], preferred_element_type=F32)
    g_attn = gate_ref[:, 0:D_MODEL].astype(F32)
    g_ssm = gate_ref[:, D_MODEL:2 * D_MODEL].astype(F32)
    mixed = (g_attn * y_attn + g_ssm * y_ssm).astype(BF16)
    o_ref[...] = x_ref[...] + jnp.dot(mixed, wout_ref[...], preferred_element_type=F32)


def _merge(y_gelu, o_attn, gates, x, w_glu, w_ssm, w_attn, w_out):
    n_tok = x.shape[0]
    tm = PROJ_TILE

    def rows(width):
        return pl.BlockSpec((tm, width), lambda i: (i, 0))

    return pl.pallas_call(
        _merge_kernel,
        grid=(n_tok // tm,),
        in_specs=[rows(SSM_WIDTH), rows(GROUP_WIDTH), rows(2 * D_MODEL), rows(D_MODEL),
                  _const_spec((SSM_WIDTH, 2 * SSM_WIDTH)), _const_spec((SSM_WIDTH, D_MODEL)),
                  _const_spec((GROUP_WIDTH, D_MODEL)), _const_spec((D_MODEL, D_MODEL))],
        out_specs=rows(D_MODEL),
        out_shape=jax.ShapeDtypeStruct((n_tok, D_MODEL), F32),
        compiler_params=pltpu.CompilerParams(
            dimension_semantics=("arbitrary",), vmem_limit_bytes=VMEM_LIMIT_BYTES),
        name="merge",
    )(y_gelu, o_attn, gates, x, w_glu, w_ssm, w_attn, w_out)


def kernel(x, ffn1_norm, ffn1_w_gate, ffn1_w_up, ffn1_w_down, mix_norm, w_in, gate_bias,
           rel_bias_table, ssm_a_re, ssm_a_im, ssm_log_dt, ssm_b_re, ssm_b_im, ssm_c_re,
           ssm_c_im, ssm_d, ssm_w_glu, w_attn_branch, w_ssm_branch, w_out, ffn2_norm,
           ffn2_w_gate, ffn2_w_up, ffn2_w_down, final_norm):
    batch, seq, _ = x.shape
    depth = ffn1_norm.shape[0]
    h = x.reshape(batch * seq, D_MODEL)
    for l in range(depth):
        last = l == depth - 1
        h = _ffn(h, ffn1_norm[l], ffn1_w_gate[l].astype(BF16), ffn1_w_up[l].astype(BF16),
                 ffn1_w_down[l].astype(BF16))
        q, k, v, u, gates = _in_proj(h, mix_norm[l], w_in[l].astype(BF16), gate_bias[l], batch, seq)
        o_attn = _attention(q, k, v, rel_bias_table, batch, seq)
        mats = _s5_matrices(ssm_a_re[l], ssm_a_im[l], ssm_log_dt[l], ssm_b_re[l], ssm_b_im[l],
                            ssm_c_re[l], ssm_c_im[l], ssm_d[l])
        y_gelu = _s5(u, mats, batch, seq)
        h = _merge(y_gelu, o_attn, gates, h, ssm_w_glu[l].astype(BF16), w_ssm_branch[l].astype(BF16),
                   w_attn_branch[l].astype(BF16), w_out[l].astype(BF16))
        h = _ffn(h, ffn2_norm[l], ffn2_w_gate[l].astype(BF16), ffn2_w_up[l].astype(BF16),
                 ffn2_w_down[l].astype(BF16), final_g=final_norm if last else None)
    if depth == 0:
        h = _rms(h, final_norm)
    return h.reshape(batch, seq, D_MODEL)
```

```python
import functools
import math

import jax
import jax.numpy as jnp
import numpy as np
from jax import lax
from jax.experimental import pallas as pl
from jax.experimental.pallas import tpu as pltpu

F32 = jnp.float32
BF16 = jnp.bfloat16

D_MODEL = 1024
D_FF = 2816
EPS = 1e-6
HEAD_DIM = 64
HEADS_PER_GROUP = 4
GROUP_WIDTH = HEADS_PER_GROUP * HEAD_DIM
DILATIONS = (1, 4, 16)
WINDOWS = (128, 512, 2048)
N_GROUPS = len(DILATIONS)
ATTN_WIDTH = N_GROUPS * GROUP_WIDTH
BLOCK = 128
N_BUCKETS = 32
MAX_DISTANCE = 2048
NEG_INF = -1e30
SSM_GROUP = 16
SSM_WIDTH = 512
SSM_GROUPS = SSM_WIDTH // SSM_GROUP
SSM_STATE = 64
IN_WIDTH = 3 * ATTN_WIDTH + SSM_WIDTH + 2 * D_MODEL

LANES = 128
MXU_WIDTH = 256
VMEM_LIMIT_BYTES = 56 * 1024 * 1024

FFN_TILE = 512
PROJ_TILE = 512
ATTN_TILE = BLOCK * DILATIONS[-1]
ATTN_INTERLEAVE = 4
SSM_CHUNK = 16
SSM_OCTET = LANES // SSM_GROUP
N_OCTETS = SSM_GROUPS // SSM_OCTET
OCTET_STATE = SSM_OCTET * SSM_STATE


def _rms(x, g):
    return x * lax.rsqrt(jnp.mean(x * x, axis=-1, keepdims=True) + EPS) * g


def _const_spec(shape):
    return pl.BlockSpec(shape, lambda *_: (0,) * len(shape), pipeline_mode=pl.Buffered(1))


def _swiglu_residual(x, g, wg_ref, wu_ref, wd_ref):
    h = _rms(x, g).astype(BF16)
    acc = jnp.zeros(x.shape, F32)
    for c in range(D_FF // MXU_WIDTH):
        sl = slice(c * MXU_WIDTH, (c + 1) * MXU_WIDTH)
        a = jnp.dot(h, wg_ref[:, sl], preferred_element_type=F32)
        b = jnp.dot(h, wu_ref[:, sl], preferred_element_type=F32)
        t = (a * jax.nn.sigmoid(a) * b).astype(BF16)
        acc = acc + jnp.dot(t, wd_ref[sl, :], preferred_element_type=F32)
    return x + 0.5 * acc


def _ffn_kernel(x_ref, g_ref, wg_ref, wu_ref, wd_ref, o_ref):
    o_ref[...] = _swiglu_residual(x_ref[...], g_ref[...], wg_ref, wu_ref, wd_ref)


def _ffn_weight_specs():
    return [_const_spec((1, D_MODEL)), _const_spec((D_MODEL, D_FF)),
            _const_spec((D_MODEL, D_FF)), _const_spec((D_FF, D_MODEL))]


def _ffn(x, g, wg, wu, wd):
    n_tok = x.shape[0]
    tm = FFN_TILE
    tile = pl.BlockSpec((tm, D_MODEL), lambda i: (i, 0))
    return pl.pallas_call(
        _ffn_kernel,
        grid=(n_tok // tm,),
        in_specs=[tile] + _ffn_weight_specs(),
        out_specs=tile,
        out_shape=jax.ShapeDtypeStruct((n_tok, D_MODEL), F32),
        compiler_params=pltpu.CompilerParams(
            dimension_semantics=("arbitrary",), vmem_limit_bytes=VMEM_LIMIT_BYTES),
        name="ffn",
    )(x, g.reshape(1, D_MODEL), wg, wu, wd)


N_QKV_CHUNKS = 3 * N_GROUPS
N_U_CHUNKS = SSM_WIDTH // MXU_WIDTH
N_GATE_CHUNKS = 2 * D_MODEL // MXU_WIDTH
N_STRIDED = 3 * (N_GROUPS - 1)


def _inproj_kernel(x_ref, g_ref, w_ref, gb_ref, *refs):
    qkv_refs = refs[:N_QKV_CHUNKS]
    u_ref, gate_ref, slab_ref = refs[N_QKV_CHUNKS:]
    tm = x_ref.shape[0]
    h = _rms(x_ref[...], g_ref[...]).astype(BF16)
    strided_idx = 0
    for c in range(IN_WIDTH // MXU_WIDTH):
        z = jnp.dot(h, w_ref[:, c * MXU_WIDTH:(c + 1) * MXU_WIDTH], preferred_element_type=F32)
        if c < N_QKV_CHUNKS:
            kind, grp = divmod(c, N_GROUPS)
            if kind == 0:
                z = z * (HEAD_DIM ** -0.5)
            d = DILATIONS[grp]
            o_ref = qkv_refs[c]
            if d == 1:
                o_ref[0, 0] = z.astype(BF16)
            else:
                base = 2 * strided_idx
                strided_idx += 1
                for s in range(2):
                    slab_ref[base + s] = z[:, s * LANES:(s + 1) * LANES]
                for r in range(d):
                    for s in range(2):
                        o_ref[0, r, :, s * LANES:(s + 1) * LANES] = (
                            slab_ref[base + s, pl.ds(r, tm // d, stride=d), :].astype(BF16))
        elif c < N_QKV_CHUNKS + N_U_CHUNKS:
            j = c - N_QKV_CHUNKS
            u_ref[:, j * MXU_WIDTH:(j + 1) * MXU_WIDTH] = z
        else:
            j = c - N_QKV_CHUNKS - N_U_CHUNKS
            sl = slice(j * MXU_WIDTH, (j + 1) * MXU_WIDTH)
            gate_ref[:, sl] = jax.nn.sigmoid(z + gb_ref[:, sl]).astype(BF16)


def _in_proj(x, g, w_in, gate_bias, batch, seq):
    n_tok = x.shape[0]
    tm = PROJ_TILE
    tiles_per_seq = seq // tm
    out_shapes, out_specs = [], []
    for _ in range(3):
        for d in DILATIONS:
            out_shapes.append(jax.ShapeDtypeStruct((batch, d, seq // d, GROUP_WIDTH), BF16))
            out_specs.append(pl.BlockSpec(
                (1, d, tm // d, GROUP_WIDTH),
                lambda i: (i // tiles_per_seq, 0, i % tiles_per_seq, 0)))
    out_shapes.append(jax.ShapeDtypeStruct((n_tok, SSM_WIDTH), F32))
    out_specs.append(pl.BlockSpec((tm, SSM_WIDTH), lambda i: (i, 0)))
    out_shapes.append(jax.ShapeDtypeStruct((n_tok, 2 * D_MODEL), BF16))
    out_specs.append(pl.BlockSpec((tm, 2 * D_MODEL), lambda i: (i, 0)))
    outs = pl.pallas_call(
        _inproj_kernel,
        grid=(n_tok // tm,),
        in_specs=[pl.BlockSpec((tm, D_MODEL), lambda i: (i, 0)),
                  _const_spec((1, D_MODEL)),
                  _const_spec((D_MODEL, IN_WIDTH)),
                  _const_spec((1, 2 * D_MODEL))],
        out_specs=out_specs,
        out_shape=out_shapes,
        scratch_shapes=[pltpu.VMEM((2 * N_STRIDED, tm, LANES), F32)],
        compiler_params=pltpu.CompilerParams(
            dimension_semantics=("arbitrary",), vmem_limit_bytes=VMEM_LIMIT_BYTES),
        name="in_proj",
    )(x, g.reshape(1, D_MODEL), w_in, gate_bias.reshape(1, 2 * D_MODEL))
    q, k, v = outs[0:3], outs[3:6], outs[6:9]
    return q, k, v, outs[9], outs[10]


def _t5_bucket_np(dist):
    max_exact = N_BUCKETS // 2
    d = np.maximum(dist, 1).astype(np.float32)
    ratio = np.log(d / np.float32(max_exact)) / np.float32(math.log(MAX_DISTANCE / max_exact))
    large = max_exact + (ratio * np.float32(N_BUCKETS - max_exact)).astype(np.int32)
    large = np.minimum(large, N_BUCKETS - 1)
    return np.where(dist < max_exact, dist, large)


def _bucket_tables():
    qi = np.arange(BLOCK)[:, None]
    kj = np.arange(2 * BLOCK)[None, :]
    steps = qi + BLOCK - kj
    tables = []
    for window, d in zip(WINDOWS, DILATIONS):
        band = (steps >= 0) & (steps <= window // d)
        bucket = _t5_bucket_np(np.maximum(steps, 0) * d)
        tables.append(np.where(band, bucket, -1).astype(np.int32))
    return jnp.asarray(np.stack(tables))


def _attn_block(q_blk, k_blk, v_blk, bias):
    lane_head = lax.broadcasted_iota(jnp.int32, (BLOCK, GROUP_WIDTH), 1) // HEAD_DIM
    zero = jnp.zeros_like(q_blk)
    qs = jnp.concatenate(
        [jnp.where(lane_head == h, q_blk, zero) for h in range(HEADS_PER_GROUP)], axis=0)
    logits = lax.dot_general(qs, k_blk, (((1,), (1,)), ((), ())), preferred_element_type=F32) + bias
    m = jnp.max(logits, axis=-1, keepdims=True)
    p = jnp.exp(logits - m)
    l = jnp.sum(p, axis=-1, keepdims=True)
    pv = jnp.dot(p.astype(BF16), v_blk, preferred_element_type=F32)
    o = jnp.zeros((BLOCK, GROUP_WIDTH), F32)
    den = jnp.ones((BLOCK, GROUP_WIDTH), F32)
    mx = jnp.zeros((BLOCK, GROUP_WIDTH), F32)
    for h in range(HEADS_PER_GROUP):
        rows = slice(h * BLOCK, (h + 1) * BLOCK)
        sel = lane_head == h
        o = jnp.where(sel, pv[rows], o)
        den = jnp.where(sel, l[rows], den)
        mx = jnp.where(sel, m[rows], mx)
    return o / den, mx + jnp.log(den)


def _build_bias(tab_ref, bucket_ref, bias_scr):
    own_block = lax.broadcasted_iota(jnp.int32, (BLOCK, 2 * BLOCK), 1) >= BLOCK
    for grp in range(N_GROUPS):
        bucket = bucket_ref[grp]
        for h in range(HEADS_PER_GROUP):
            col = grp * HEADS_PER_GROUP + h

            def pick(b, acc, bucket=bucket, col=col):
                return jnp.where(bucket == b, tab_ref[b, col], acc)

            bias = lax.fori_loop(0, N_BUCKETS, pick, jnp.full((BLOCK, 2 * BLOCK), NEG_INF, F32))
            rows = slice(h * BLOCK, (h + 1) * BLOCK)
            bias_scr[grp, 0, rows, :] = bias
            bias_scr[grp, 1, rows, :] = jnp.where(own_block, bias, NEG_INF)


def _block_aligned(row):
    return row if isinstance(row, int) else pl.multiple_of(row, BLOCK)


def _attn_kernel(*refs):
    n = N_GROUPS
    tab_ref, bucket_ref = refs[0:2]
    refs = refs[2:]
    q_refs, kc_refs, kp_refs = refs[0:n], refs[n:2 * n], refs[2 * n:3 * n]
    vc_refs, vp_refs = refs[3 * n:4 * n], refs[4 * n:5 * n]
    o_ref, bias_scr, o_scr, l_scr = refs[5 * n:]
    tile = pl.program_id(1)

    @pl.when((pl.program_id(0) == 0) & (tile == 0))
    def _():
        _build_bias(tab_ref, bucket_ref, bias_scr)

    for grp, d in enumerate(DILATIONS):
        q_ref, kc_ref, kp_ref = q_refs[grp], kc_refs[grp], kp_refs[grp]
        vc_ref, vp_ref = vc_refs[grp], vp_refs[grp]
        blocks_per_residue = ATTN_TILE // d // BLOCK

        def run_block(r, blk, grp=grp, d=d, q_ref=q_ref, kc_ref=kc_ref, kp_ref=kp_ref,
                      vc_ref=vc_ref, vp_ref=vp_ref):
            if isinstance(blk, int) and blk == 0:
                k_blk = jnp.concatenate([kp_ref[0, r], kc_ref[0, r, 0:BLOCK, :]], axis=0)
                v_blk = jnp.concatenate([vp_ref[0, r], vc_ref[0, r, 0:BLOCK, :]], axis=0)
                bias = jnp.where(tile == 0, bias_scr[grp, 1], bias_scr[grp, 0])
                q_blk = q_ref[0, r, 0:BLOCK, :]
            else:
                kv_rows = pl.ds(_block_aligned((blk - 1) * BLOCK), 2 * BLOCK)
                k_blk = kc_ref[0, r, kv_rows, :]
                v_blk = vc_ref[0, r, kv_rows, :]
                bias = bias_scr[grp, 0]
                q_blk = q_ref[0, r, pl.ds(_block_aligned(blk * BLOCK), BLOCK), :]
            o, lse = _attn_block(q_blk, k_blk, v_blk, bias)
            start = blk * (BLOCK * d) + r
            for s in range(2):
                rows = pl.ds(start, BLOCK, stride=d) if d > 1 else pl.ds(start, BLOCK)
                o_scr[grp, s, rows, :] = o[:, s * LANES:(s + 1) * LANES]
                l_scr[grp, s, rows, :] = lse[:, s * LANES:(s + 1) * LANES]

        if blocks_per_residue == 1:
            def residue_set(i, c, run_block=run_block):
                for k in range(ATTN_INTERLEAVE):
                    run_block(ATTN_INTERLEAVE * i + k, 0)
                return c
            lax.fori_loop(0, d // ATTN_INTERLEAVE, residue_set, 0)
        elif d > 1:
            def residue(r, c, run_block=run_block, blocks_per_residue=blocks_per_residue):
                run_block(r, 0)
                for blk in range(1, blocks_per_residue):
                    run_block(r, blk)
                return c
            lax.fori_loop(0, d, residue, 0)
        else:
            for blk in range(ATTN_INTERLEAVE):
                run_block(0, blk)

            def block_set(i, c, run_block=run_block):
                for k in range(ATTN_INTERLEAVE):
                    run_block(0, ATTN_INTERLEAVE * i + k)
                return c
            lax.fori_loop(1, blocks_per_residue // ATTN_INTERLEAVE, block_set, 0)

    merge_rows = 256

    def merge(c, carry):
        rows = pl.ds(pl.multiple_of(c * merge_rows, merge_rows), merge_rows)
        for s in range(2):
            lses = [l_scr[grp, s, rows, :] for grp in range(N_GROUPS)]
            mx = functools.reduce(jnp.maximum, lses)
            es = [jnp.exp(x - mx) for x in lses]
            num = sum(e * o_scr[grp, s, rows, :] for grp, e in enumerate(es))
            o_ref[rows, s * LANES:(s + 1) * LANES] = (num / sum(es)).astype(BF16)
        return carry

    lax.fori_loop(0, ATTN_TILE // merge_rows, merge, 0)


def _attention(q, k, v, rel_bias_table, batch, seq):
    tiles = seq // ATTN_TILE

    def cur_spec(d):
        return pl.BlockSpec((1, d, ATTN_TILE // d, GROUP_WIDTH), lambda b, j: (b, 0, j, 0))

    def prev_spec(d):
        per_tile = ATTN_TILE // d // BLOCK
        return pl.BlockSpec((1, d, BLOCK, GROUP_WIDTH),
                            lambda b, j: (b, 0, jnp.maximum(j * per_tile - 1, 0), 0))

    in_specs = ([pl.BlockSpec(memory_space=pltpu.SMEM),
                 _const_spec((N_GROUPS, BLOCK, 2 * BLOCK))]
                + [cur_spec(d) for d in DILATIONS] + [cur_spec(d) for d in DILATIONS]
                + [prev_spec(d) for d in DILATIONS] + [cur_spec(d) for d in DILATIONS]
                + [prev_spec(d) for d in DILATIONS])
    return pl.pallas_call(
        _attn_kernel,
        grid=(batch, tiles),
        in_specs=in_specs,
        out_specs=pl.BlockSpec((ATTN_TILE, GROUP_WIDTH), lambda b, j: (b * tiles + j, 0)),
        out_shape=jax.ShapeDtypeStruct((batch * seq, GROUP_WIDTH), BF16),
        scratch_shapes=[pltpu.VMEM((N_GROUPS, 2, HEADS_PER_GROUP * BLOCK, 2 * BLOCK), F32),
                        pltpu.VMEM((N_GROUPS, 2, ATTN_TILE, LANES), F32),
                        pltpu.VMEM((N_GROUPS, 2, ATTN_TILE, LANES), F32)],
        compiler_params=pltpu.CompilerParams(
            dimension_semantics=("arbitrary", "arbitrary"), vmem_limit_bytes=VMEM_LIMIT_BYTES),
        name="dilated_attn",
    )(rel_bias_table.astype(F32), _bucket_tables(), *q, *k, *k, *v, *v)


def _octet_mask(row_group, col_group, rows, cols):
    r = np.arange(rows)[:, None] // row_group
    c = np.arange(cols)[None, :] // col_group
    return jnp.asarray(r == c)


def _s5_matrices(a_re, a_im, log_dt, b_re, b_im, c_re, c_im, d_skip):
    hi = lax.Precision.HIGHEST
    tc = SSM_CHUNK
    lam_re = a_re.astype(F32)
    lam_im = a_im.astype(F32)
    dt = jnp.exp(log_dt.astype(F32))[:, None]
    mag = jnp.exp(lam_re * dt)
    ab_re = mag * jnp.cos(lam_im * dt)
    ab_im = mag * jnp.sin(lam_im * dt)
    den = lam_re * lam_re + lam_im * lam_im
    xr = ab_re - 1.0
    coef_re = (xr * lam_re + ab_im * lam_im) / den
    coef_im = (ab_im * lam_re - xr * lam_im) / den
    br = b_re.astype(F32)
    bi = b_im.astype(F32)
    bb_re = coef_re[..., None] * br - coef_im[..., None] * bi
    bb_im = coef_re[..., None] * bi + coef_im[..., None] * br
    cr = c_re.astype(F32)
    ci = c_im.astype(F32)

    def a_pow(k):
        kk = k.astype(F32)[:, None, None]
        pm = jnp.exp(lam_re * dt * kk)
        return pm * jnp.cos(lam_im * dt * kk), pm * jnp.sin(lam_im * dt * kk)

    lags = jnp.arange(tc)
    pr, pi = a_pow(lags)
    abr = pr[..., None] * bb_re - pi[..., None] * bb_im
    abi = pr[..., None] * bb_im + pi[..., None] * bb_re
    kern = (jnp.einsum('gon,lgnc->lgoc', cr, abr, precision=hi)
            - jnp.einsum('gon,lgnc->lgoc', ci, abi, precision=hi))
    skip = d_skip.astype(F32).reshape(SSM_GROUPS, SSM_GROUP)
    kern = kern.at[0].add(skip[:, :, None] * jnp.eye(SSM_GROUP, dtype=F32))

    kt = jnp.transpose(kern, (0, 1, 3, 2)).reshape(tc, N_OCTETS, LANES, SSM_GROUP)
    kt = jnp.transpose(kt, (1, 0, 2, 3))
    kblk = jnp.where(_octet_mask(SSM_GROUP, SSM_GROUP, LANES, LANES),
                     jnp.tile(kt, (1, 1, 1, SSM_OCTET)), 0.0)

    def p_half(ab):
        src = ab[::-1].reshape(tc, N_OCTETS, SSM_OCTET, SSM_STATE, SSM_GROUP)
        src = jnp.transpose(src, (1, 0, 2, 4, 3)).reshape(N_OCTETS, tc, LANES, SSM_STATE)
        return jnp.where(_octet_mask(SSM_GROUP, SSM_STATE, LANES, OCTET_STATE),
                         jnp.tile(src, (1, 1, 1, SSM_OCTET)), 0.0)

    pmat = jnp.concatenate([p_half(abr), p_half(abi)], axis=-1).reshape(
        N_OCTETS, tc * LANES, 2 * OCTET_STATE)

    qr_pow, qi_pow = a_pow(lags + 1)
    q_sr = cr[None] * qr_pow[:, :, None, :] - ci[None] * qi_pow[:, :, None, :]
    q_si = -(cr[None] * qi_pow[:, :, None, :] + ci[None] * qr_pow[:, :, None, :])

    def q_half(qq):
        src = qq.reshape(tc, N_OCTETS, SSM_OCTET, SSM_GROUP, SSM_STATE)
        src = jnp.transpose(src, (1, 2, 4, 0, 3)).reshape(N_OCTETS, OCTET_STATE, tc, SSM_GROUP)
        mask = _octet_mask(SSM_STATE, SSM_GROUP, OCTET_STATE, LANES)[None, :, None, :]
        return jnp.where(mask, jnp.tile(src, (1, 1, 1, SSM_OCTET)), 0.0).reshape(
            N_OCTETS, OCTET_STATE, tc * LANES)

    qmat = jnp.concatenate([q_half(q_sr), q_half(q_si)], axis=1)

    ac_re, ac_im = a_pow(jnp.array([tc]))
    a_chunk = jnp.concatenate([ac_re.reshape(N_OCTETS, 1, OCTET_STATE),
                               ac_im.reshape(N_OCTETS, 1, OCTET_STATE)], axis=-1)
    return kblk.astype(BF16), pmat.astype(BF16), qmat.astype(BF16), a_chunk


def _s5_kernel(u_ref, kblk_ref, p_ref, q_ref, a_ref, o_ref, k_scr, x_scr, s_scr, z_scr):
    n_chunks = x_scr.shape[0]
    tc = SSM_CHUNK

    @pl.when(pl.program_id(1) == 0)
    def _():
        k_scr[...] = jnp.zeros(k_scr.shape, BF16)
        for s in range(tc):
            for t in range(s, tc):
                k_scr[s * LANES:(s + 1) * LANES, t * LANES:(t + 1) * LANES] = kblk_ref[0, t - s]

    inc = None
    for pair in range(tc // 2):
        for t in (2 * pair, 2 * pair + 1):
            x_scr[:, t * LANES:(t + 1) * LANES] = (
                u_ref[pl.ds(t, n_chunks, stride=tc), :].astype(BF16))
        rows = slice(pair * MXU_WIDTH, (pair + 1) * MXU_WIDTH)
        part = jnp.dot(x_scr[:, rows], p_ref[0, rows, :], preferred_element_type=F32)
        inc = part if inc is None else inc + part
    s_scr[...] = inc

    a_re = a_ref[0, :, 0:OCTET_STATE]
    a_im = a_ref[0, :, OCTET_STATE:2 * OCTET_STATE]

    def step(j, carry):
        c_re, c_im = carry
        row = pl.ds(j, 1)
        z_scr[row, 0:OCTET_STATE] = c_re
        z_scr[row, OCTET_STATE:2 * OCTET_STATE] = c_im
        inc_re = s_scr[row, 0:OCTET_STATE]
        inc_im = s_scr[row, OCTET_STATE:2 * OCTET_STATE]
        return (a_re * c_re - a_im * c_im + inc_re, a_re * c_im + a_im * c_re + inc_im)

    zero = jnp.zeros((1, OCTET_STATE), F32)
    lax.fori_loop(0, n_chunks, step, (zero, zero))

    zb = z_scr[...].astype(BF16)
    for pair in range(tc // 2):
        cols = slice(pair * MXU_WIDTH, (pair + 1) * MXU_WIDTH)
        live = (2 * pair + 2) * LANES
        y = (jnp.dot(x_scr[:, 0:live], k_scr[0:live, cols], preferred_element_type=F32)
             + jnp.dot(zb, q_ref[0, :, cols], preferred_element_type=F32))
        y = jax.nn.gelu(y)
        o_ref[pl.ds(2 * pair, n_chunks, stride=tc), :] = y[:, 0:LANES]
        o_ref[pl.ds(2 * pair + 1, n_chunks, stride=tc), :] = y[:, LANES:2 * LANES]


def _s5(u, mats, batch, seq):
    kblk, pmat, qmat, a_chunk = mats
    n_chunks = seq // SSM_CHUNK
    width = SSM_CHUNK * LANES

    def per_octet(shape):
        return pl.BlockSpec((1,) + shape, lambda o, b: (o,) + (0,) * len(shape),
                            pipeline_mode=pl.Buffered(1))

    return pl.pallas_call(
        _s5_kernel,
        grid=(N_OCTETS, batch),
        in_specs=[pl.BlockSpec((seq, LANES), lambda o, b: (b, o)),
                  per_octet((SSM_CHUNK, LANES, LANES)),
                  per_octet((width, 2 * OCTET_STATE)),
                  per_octet((2 * OCTET_STATE, width)),
                  per_octet((1, 2 * OCTET_STATE))],
        out_specs=pl.BlockSpec((seq, LANES), lambda o, b: (b, o)),
        out_shape=jax.ShapeDtypeStruct((batch * seq, SSM_WIDTH), F32),
        scratch_shapes=[pltpu.VMEM((width, width), BF16),
                        pltpu.VMEM((n_chunks, width), BF16),
                        pltpu.VMEM((n_chunks, 2 * OCTET_STATE), F32),
                        pltpu.VMEM((n_chunks, 2 * OCTET_STATE), F32)],
        compiler_params=pltpu.CompilerParams(
            dimension_semantics=("arbitrary", "arbitrary"), vmem_limit_bytes=VMEM_LIMIT_BYTES),
        name="s5_mixer",
    )(u, kblk, pmat, qmat, a_chunk)


def _merge_ffn_kernel(y_ref, oa_ref, gate_ref, x_ref, wglu_ref, wssm_ref, wattn_ref, wout_ref,
                      g_ref, wg_ref, wu_ref, wd_ref, fn_ref, o_ref, *, final):
    glu = jnp.dot(y_ref[...].astype(BF16), wglu_ref[...], preferred_element_type=F32)
    ys = (glu[:, 0:SSM_WIDTH] * jax.nn.sigmoid(glu[:, SSM_WIDTH:2 * SSM_WIDTH])).astype(BF16)
    y_ssm = jnp.dot(ys, wssm_ref[...], preferred_element_type=F32)
    y_attn = jnp.dot(oa_ref[...], wattn_ref[...], preferred_element_type=F32)
    g_attn = gate_ref[:, 0:D_MODEL].astype(F32)
    g_ssm = gate_ref[:, D_MODEL:2 * D_MODEL].astype(F32)
    mixed = (g_attn * y_attn + g_ssm * y_ssm).astype(BF16)
    x = x_ref[...] + jnp.dot(mixed, wout_ref[...], preferred_element_type=F32)
    y = _swiglu_residual(x, g_ref[...], wg_ref, wu_ref, wd_ref)
    o_ref[...] = _rms(y, fn_ref[...]) if final else y


def _merge_ffn(y_gelu, o_attn, gates, x, w_glu, w_ssm, w_attn, w_out, g, wg, wu, wd, final_g, final):
    n_tok = x.shape[0]
    tm = FFN_TILE

    def rows(width):
        return pl.BlockSpec((tm, width), lambda i: (i, 0))

    return pl.pallas_call(
        functools.partial(_merge_ffn_kernel, final=final),
        grid=(n_tok // tm,),
        in_specs=[rows(SSM_WIDTH), rows(GROUP_WIDTH), rows(2 * D_MODEL), rows(D_MODEL),
                  _const_spec((SSM_WIDTH, 2 * SSM_WIDTH)), _const_spec((SSM_WIDTH, D_MODEL)),
                  _const_spec((GROUP_WIDTH, D_MODEL)), _const_spec((D_MODEL, D_MODEL))]
                 + _ffn_weight_specs() + [_const_spec((1, D_MODEL))],
        out_specs=rows(D_MODEL),
        out_shape=jax.ShapeDtypeStruct((n_tok, D_MODEL), F32),
        compiler_params=pltpu.CompilerParams(
            dimension_semantics=("arbitrary",), vmem_limit_bytes=VMEM_LIMIT_BYTES),
        name="merge_ffn",
    )(y_gelu, o_attn, gates, x, w_glu, w_ssm, w_attn, w_out,
      g.reshape(1, D_MODEL), wg, wu, wd, final_g.reshape(1, D_MODEL))


def kernel(x, ffn1_norm, ffn1_w_gate, ffn1_w_up, ffn1_w_down, mix_norm, w_in, gate_bias,
           rel_bias_table, ssm_a_re, ssm_a_im, ssm_log_dt, ssm_b_re, ssm_b_im, ssm_c_re,
           ssm_c_im, ssm_d, ssm_w_glu, w_attn_branch, w_ssm_branch, w_out, ffn2_norm,
           ffn2_w_gate, ffn2_w_up, ffn2_w_down, final_norm):
    batch, seq, _ = x.shape
    depth = ffn1_norm.shape[0]
    h = x.reshape(batch * seq, D_MODEL)
    for l in range(depth):
        h = _ffn(h, ffn1_norm[l], ffn1_w_gate[l].astype(BF16), ffn1_w_up[l].astype(BF16),
                 ffn1_w_down[l].astype(BF16))
        q, k, v, u, gates = _in_proj(h, mix_norm[l], w_in[l].astype(BF16), gate_bias[l], batch, seq)
        o_attn = _attention(q, k, v, rel_bias_table, batch, seq)
        mats = _s5_matrices(ssm_a_re[l], ssm_a_im[l], ssm_log_dt[l], ssm_b_re[l], ssm_b_im[l],
                            ssm_c_re[l], ssm_c_im[l], ssm_d[l])
        y_gelu = _s5(u, mats, batch, seq)
        h = _merge_ffn(y_gelu, o_attn, gates, h, ssm_w_glu[l].astype(BF16),
                       w_ssm_branch[l].astype(BF16), w_attn_branch[l].astype(BF16),
                       w_out[l].astype(BF16), ffn2_norm[l], ffn2_w_gate[l].astype(BF16),
                       ffn2_w_up[l].astype(BF16), ffn2_w_down[l].astype(BF16), final_norm,
                       final=(l == depth - 1))
    return h.reshape(batch, seq, D_MODEL)
```

```python
import functools
import math

import jax
import jax.numpy as jnp
import numpy as np
from jax import lax
from jax.experimental import pallas as pl
from jax.experimental.pallas import tpu as pltpu

F32 = jnp.float32
BF16 = jnp.bfloat16

D_MODEL = 1024
D_FF = 2816
EPS = 1e-6
HEAD_DIM = 64
HEADS_PER_GROUP = 4
GROUP_WIDTH = HEADS_PER_GROUP * HEAD_DIM
DILATIONS = (1, 4, 16)
WINDOWS = (128, 512, 2048)
N_GROUPS = len(DILATIONS)
ATTN_WIDTH = N_GROUPS * GROUP_WIDTH
BLOCK = 128
N_BUCKETS = 32
MAX_DISTANCE = 2048
NEG_INF = -1e30
SSM_GROUP = 16
SSM_WIDTH = 512
SSM_GROUPS = SSM_WIDTH // SSM_GROUP
SSM_STATE = 64
IN_WIDTH = 3 * ATTN_WIDTH + SSM_WIDTH + 2 * D_MODEL

LANES = 128
MXU_WIDTH = 256
VMEM_LIMIT_BYTES = 56 * 1024 * 1024

FFN_TILE = 1024
PROJ_TILE = 1024
MERGE_TILE = 512
ATTN_TILE = BLOCK * DILATIONS[-1]
ATTN_INTERLEAVE = 4
SSM_CHUNK = 16
SSM_OCTET = LANES // SSM_GROUP
N_OCTETS = SSM_GROUPS // SSM_OCTET
OCTET_STATE = SSM_OCTET * SSM_STATE


def _rms(x, g):
    return x * lax.rsqrt(jnp.mean(x * x, axis=-1, keepdims=True) + EPS) * g


def _const_spec(shape):
    return pl.BlockSpec(shape, lambda *_: (0,) * len(shape), pipeline_mode=pl.Buffered(1))


def _swiglu_residual(x, g, wg_ref, wu_ref, wd_ref):
    h = _rms(x, g).astype(BF16)
    acc = jnp.zeros(x.shape, F32)
    for c in range(D_FF // MXU_WIDTH):
        sl = slice(c * MXU_WIDTH, (c + 1) * MXU_WIDTH)
        a = jnp.dot(h, wg_ref[:, sl], preferred_element_type=F32)
        b = jnp.dot(h, wu_ref[:, sl], preferred_element_type=F32)
        t = (a * jax.nn.sigmoid(a) * b).astype(BF16)
        acc = acc + jnp.dot(t, wd_ref[sl, :], preferred_element_type=F32)
    return x + 0.5 * acc


def _ffn_kernel(x_ref, g_ref, wg_ref, wu_ref, wd_ref, o_ref):
    o_ref[...] = _swiglu_residual(x_ref[...], g_ref[...], wg_ref, wu_ref, wd_ref)


def _ffn_weight_specs():
    return [_const_spec((1, D_MODEL)), _const_spec((D_MODEL, D_FF)),
            _const_spec((D_MODEL, D_FF)), _const_spec((D_FF, D_MODEL))]


def _ffn(x, g, wg, wu, wd):
    n_tok = x.shape[0]
    tm = FFN_TILE
    tile = pl.BlockSpec((tm, D_MODEL), lambda i: (i, 0))
    return pl.pallas_call(
        _ffn_kernel,
        grid=(n_tok // tm,),
        in_specs=[tile] + _ffn_weight_specs(),
        out_specs=tile,
        out_shape=jax.ShapeDtypeStruct((n_tok, D_MODEL), F32),
        compiler_params=pltpu.CompilerParams(
            dimension_semantics=("arbitrary",), vmem_limit_bytes=VMEM_LIMIT_BYTES),
        name="ffn",
    )(x, g.reshape(1, D_MODEL), wg, wu, wd)


N_QKV_CHUNKS = 3 * N_GROUPS
N_U_CHUNKS = SSM_WIDTH // MXU_WIDTH
N_GATE_CHUNKS = 2 * D_MODEL // MXU_WIDTH
N_STRIDED = 3 * (N_GROUPS - 1)


def _inproj_kernel(x_ref, g_ref, w_ref, gb_ref, *refs):
    qkv_refs = refs[:N_QKV_CHUNKS]
    u_ref, gate_ref, slab_ref = refs[N_QKV_CHUNKS:]
    tm = x_ref.shape[0]
    h = _rms(x_ref[...], g_ref[...]).astype(BF16)
    strided_idx = 0
    for c in range(IN_WIDTH // MXU_WIDTH):
        z = jnp.dot(h, w_ref[:, c * MXU_WIDTH:(c + 1) * MXU_WIDTH], preferred_element_type=F32)
        if c < N_QKV_CHUNKS:
            kind, grp = divmod(c, N_GROUPS)
            if kind == 0:
                z = z * (HEAD_DIM ** -0.5)
            d = DILATIONS[grp]
            o_ref = qkv_refs[c]
            if d == 1:
                o_ref[0, 0] = z.astype(BF16)
            else:
                base = 2 * strided_idx
                strided_idx += 1
                for s in range(2):
                    slab_ref[base + s] = z[:, s * LANES:(s + 1) * LANES]
                for r in range(d):
                    for s in range(2):
                        o_ref[0, r, :, s * LANES:(s + 1) * LANES] = (
                            slab_ref[base + s, pl.ds(r, tm // d, stride=d), :].astype(BF16))
        elif c < N_QKV_CHUNKS + N_U_CHUNKS:
            j = c - N_QKV_CHUNKS
            for s in range(2):
                u_ref[2 * j + s] = z[:, s * LANES:(s + 1) * LANES]
        else:
            j = c - N_QKV_CHUNKS - N_U_CHUNKS
            sl = slice(j * MXU_WIDTH, (j + 1) * MXU_WIDTH)
            gate_ref[:, sl] = jax.nn.sigmoid(z + gb_ref[:, sl]).astype(BF16)


def _in_proj(x, g, w_in, gate_bias, batch, seq):
    n_tok = x.shape[0]
    tm = PROJ_TILE
    tiles_per_seq = seq // tm
    out_shapes, out_specs = [], []
    for _ in range(3):
        for d in DILATIONS:
            out_shapes.append(jax.ShapeDtypeStruct((batch, d, seq // d, GROUP_WIDTH), BF16))
            out_specs.append(pl.BlockSpec(
                (1, d, tm // d, GROUP_WIDTH),
                lambda i: (i // tiles_per_seq, 0, i % tiles_per_seq, 0)))
    out_shapes.append(jax.ShapeDtypeStruct((N_OCTETS, n_tok, LANES), F32))
    out_specs.append(pl.BlockSpec((N_OCTETS, tm, LANES), lambda i: (0, i, 0)))
    out_shapes.append(jax.ShapeDtypeStruct((n_tok, 2 * D_MODEL), BF16))
    out_specs.append(pl.BlockSpec((tm, 2 * D_MODEL), lambda i: (i, 0)))
    outs = pl.pallas_call(
        _inproj_kernel,
        grid=(n_tok // tm,),
        in_specs=[pl.BlockSpec((tm, D_MODEL), lambda i: (i, 0)),
                  _const_spec((1, D_MODEL)),
                  _const_spec((D_MODEL, IN_WIDTH)),
                  _const_spec((1, 2 * D_MODEL))],
        out_specs=out_specs,
        out_shape=out_shapes,
        scratch_shapes=[pltpu.VMEM((2 * N_STRIDED, tm, LANES), F32)],
        compiler_params=pltpu.CompilerParams(
            dimension_semantics=("arbitrary",), vmem_limit_bytes=VMEM_LIMIT_BYTES),
        name="in_proj",
    )(x, g.reshape(1, D_MODEL), w_in, gate_bias.reshape(1, 2 * D_MODEL))
    q, k, v = outs[0:3], outs[3:6], outs[6:9]
    return q, k, v, outs[9], outs[10]


def _t5_bucket_np(dist):
    max_exact = N_BUCKETS // 2
    d = np.maximum(dist, 1).astype(np.float32)
    ratio = np.log(d / np.float32(max_exact)) / np.float32(math.log(MAX_DISTANCE / max_exact))
    large = max_exact + (ratio * np.float32(N_BUCKETS - max_exact)).astype(np.int32)
    large = np.minimum(large, N_BUCKETS - 1)
    return np.where(dist < max_exact, dist, large)


def _bucket_tables():
    qi = np.arange(BLOCK)[:, None]
    kj = np.arange(2 * BLOCK)[None, :]
    steps = qi + BLOCK - kj
    tables = []
    for window, d in zip(WINDOWS, DILATIONS):
        band = (steps >= 0) & (steps <= window // d)
        bucket = _t5_bucket_np(np.maximum(steps, 0) * d)
        tables.append(np.where(band, bucket, -1).astype(np.int32))
    return jnp.asarray(np.stack(tables))


def _attn_block(q_blk, k_blk, v_blk, bias):
    lane_head = lax.broadcasted_iota(jnp.int32, (BLOCK, GROUP_WIDTH), 1) // HEAD_DIM
    zero = jnp.zeros_like(q_blk)
    qs = jnp.concatenate(
        [jnp.where(lane_head == h, q_blk, zero) for h in range(HEADS_PER_GROUP)], axis=0)
    logits = lax.dot_general(qs, k_blk, (((1,), (1,)), ((), ())), preferred_element_type=F32) + bias
    m = jnp.max(logits, axis=-1, keepdims=True)
    p = jnp.exp(logits - m)
    l = jnp.sum(p, axis=-1, keepdims=True)
    pv = jnp.dot(p.astype(BF16), v_blk, preferred_element_type=F32)
    o = jnp.zeros((BLOCK, GROUP_WIDTH), F32)
    den = jnp.ones((BLOCK, GROUP_WIDTH), F32)
    mx = jnp.zeros((BLOCK, GROUP_WIDTH), F32)
    for h in range(HEADS_PER_GROUP):
        rows = slice(h * BLOCK, (h + 1) * BLOCK)
        sel = lane_head == h
        o = jnp.where(sel, pv[rows], o)
        den = jnp.where(sel, l[rows], den)
        mx = jnp.where(sel, m[rows], mx)
    return o / den, mx + jnp.log(den)


def _build_bias(tab_ref, bucket_ref, bias_scr):
    own_block = lax.broadcasted_iota(jnp.int32, (BLOCK, 2 * BLOCK), 1) >= BLOCK
    for grp in range(N_GROUPS):
        bucket = bucket_ref[grp]
        for h in range(HEADS_PER_GROUP):
            col = grp * HEADS_PER_GROUP + h

            def pick(b, acc, bucket=bucket, col=col):
                return jnp.where(bucket == b, tab_ref[b, col], acc)

            bias = lax.fori_loop(0, N_BUCKETS, pick, jnp.full((BLOCK, 2 * BLOCK), NEG_INF, F32))
            rows = slice(h * BLOCK, (h + 1) * BLOCK)
            bias_scr[grp, 0, rows, :] = bias
            bias_scr[grp, 1, rows, :] = jnp.where(own_block, bias, NEG_INF)


def _block_aligned(row):
    return row if isinstance(row, int) else pl.multiple_of(row, BLOCK)


def _attn_kernel(*refs):
    n = N_GROUPS
    tab_ref, bucket_ref = refs[0:2]
    refs = refs[2:]
    q_refs, kc_refs, kp_refs = refs[0:n], refs[n:2 * n], refs[2 * n:3 * n]
    vc_refs, vp_refs = refs[3 * n:4 * n], refs[4 * n:5 * n]
    o_ref, bias_scr, o_scr, l_scr = refs[5 * n:]
    tile = pl.program_id(1)

    @pl.when((pl.program_id(0) == 0) & (tile == 0))
    def _():
        _build_bias(tab_ref, bucket_ref, bias_scr)

    for grp, d in enumerate(DILATIONS):
        q_ref, kc_ref, kp_ref = q_refs[grp], kc_refs[grp], kp_refs[grp]
        vc_ref, vp_ref = vc_refs[grp], vp_refs[grp]
        blocks_per_residue = ATTN_TILE // d // BLOCK

        def run_block(r, blk, grp=grp, d=d, q_ref=q_ref, kc_ref=kc_ref, kp_ref=kp_ref,
                      vc_ref=vc_ref, vp_ref=vp_ref):
            if isinstance(blk, int) and blk == 0:
                k_blk = jnp.concatenate([kp_ref[0, r], kc_ref[0, r, 0:BLOCK, :]], axis=0)
                v_blk = jnp.concatenate([vp_ref[0, r], vc_ref[0, r, 0:BLOCK, :]], axis=0)
                bias = jnp.where(tile == 0, bias_scr[grp, 1], bias_scr[grp, 0])
                q_blk = q_ref[0, r, 0:BLOCK, :]
            else:
                kv_rows = pl.ds(_block_aligned((blk - 1) * BLOCK), 2 * BLOCK)
                k_blk = kc_ref[0, r, kv_rows, :]
                v_blk = vc_ref[0, r, kv_rows, :]
                bias = bias_scr[grp, 0]
                q_blk = q_ref[0, r, pl.ds(_block_aligned(blk * BLOCK), BLOCK), :]
            o, lse = _attn_block(q_blk, k_blk, v_blk, bias)
            start = blk * (BLOCK * d) + r
            for s in range(2):
                rows = pl.ds(start, BLOCK, stride=d) if d > 1 else pl.ds(start, BLOCK)
                o_scr[grp, s, rows, :] = o[:, s * LANES:(s + 1) * LANES]
                l_scr[grp, s, rows, :] = lse[:, s * LANES:(s + 1) * LANES]

        if blocks_per_residue == 1:
            def residue_set(i, c, run_block=run_block):
                for k in range(ATTN_INTERLEAVE):
                    run_block(ATTN_INTERLEAVE * i + k, 0)
                return c
            lax.fori_loop(0, d // ATTN_INTERLEAVE, residue_set, 0)
        elif d > 1:
            def residue(r, c, run_block=run_block, blocks_per_residue=blocks_per_residue):
                run_block(r, 0)
                for blk in range(1, blocks_per_residue):
                    run_block(r, blk)
                return c
            lax.fori_loop(0, d, residue, 0)
        else:
            for blk in range(ATTN_INTERLEAVE):
                run_block(0, blk)

            def block_set(i, c, run_block=run_block):
                for k in range(ATTN_INTERLEAVE):
                    run_block(0, ATTN_INTERLEAVE * i + k)
                return c
            lax.fori_loop(1, blocks_per_residue // ATTN_INTERLEAVE, block_set, 0)

    merge_rows = 256

    def merge(c, carry):
        rows = pl.ds(pl.multiple_of(c * merge_rows, merge_rows), merge_rows)
        for s in range(2):
            lses = [l_scr[grp, s, rows, :] for grp in range(N_GROUPS)]
            mx = functools.reduce(jnp.maximum, lses)
            es = [jnp.exp(x - mx) for x in lses]
            num = sum(e * o_scr[grp, s, rows, :] for grp, e in enumerate(es))
            o_ref[rows, s * LANES:(s + 1) * LANES] = (num / sum(es)).astype(BF16)
        return carry

    lax.fori_loop(0, ATTN_TILE // merge_rows, merge, 0)


def _attention(q, k, v, rel_bias_table, batch, seq):
    tiles = seq // ATTN_TILE

    def cur_spec(d):
        return pl.BlockSpec((1, d, ATTN_TILE // d, GROUP_WIDTH), lambda b, j: (b, 0, j, 0))

    def prev_spec(d):
        per_tile = ATTN_TILE // d // BLOCK
        return pl.BlockSpec((1, d, BLOCK, GROUP_WIDTH),
                            lambda b, j: (b, 0, jnp.maximum(j * per_tile - 1, 0), 0))

    in_specs = ([pl.BlockSpec(memory_space=pltpu.SMEM),
                 _const_spec((N_GROUPS, BLOCK, 2 * BLOCK))]
                + [cur_spec(d) for d in DILATIONS] + [cur_spec(d) for d in DILATIONS]
                + [prev_spec(d) for d in DILATIONS] + [cur_spec(d) for d in DILATIONS]
                + [prev_spec(d) for d in DILATIONS])
    return pl.pallas_call(
        _attn_kernel,
        grid=(batch, tiles),
        in_specs=in_specs,
        out_specs=pl.BlockSpec((ATTN_TILE, GROUP_WIDTH), lambda b, j: (b * tiles + j, 0)),
        out_shape=jax.ShapeDtypeStruct((batch * seq, GROUP_WIDTH), BF16),
        scratch_shapes=[pltpu.VMEM((N_GROUPS, 2, HEADS_PER_GROUP * BLOCK, 2 * BLOCK), F32),
                        pltpu.VMEM((N_GROUPS, 2, ATTN_TILE, LANES), F32),
                        pltpu.VMEM((N_GROUPS, 2, ATTN_TILE, LANES), F32)],
        compiler_params=pltpu.CompilerParams(
            dimension_semantics=("arbitrary", "arbitrary"), vmem_limit_bytes=VMEM_LIMIT_BYTES),
        name="dilated_attn",
    )(rel_bias_table.astype(F32), _bucket_tables(), *q, *k, *k, *v, *v)


def _octet_mask(row_group, col_group, rows, cols):
    r = np.arange(rows)[:, None] // row_group
    c = np.arange(cols)[None, :] // col_group
    return jnp.asarray(r == c)


def _s5_matrices(a_re, a_im, log_dt, b_re, b_im, c_re, c_im, d_skip):
    hi = lax.Precision.HIGHEST
    tc = SSM_CHUNK
    lam_re = a_re.astype(F32)
    lam_im = a_im.astype(F32)
    dt = jnp.exp(log_dt.astype(F32))[:, None]
    mag = jnp.exp(lam_re * dt)
    ab_re = mag * jnp.cos(lam_im * dt)
    ab_im = mag * jnp.sin(lam_im * dt)
    den = lam_re * lam_re + lam_im * lam_im
    xr = ab_re - 1.0
    coef_re = (xr * lam_re + ab_im * lam_im) / den
    coef_im = (ab_im * lam_re - xr * lam_im) / den
    br = b_re.astype(F32)
    bi = b_im.astype(F32)
    bb_re = coef_re[..., None] * br - coef_im[..., None] * bi
    bb_im = coef_re[..., None] * bi + coef_im[..., None] * br
    cr = c_re.astype(F32)
    ci = c_im.astype(F32)

    def a_pow(k):
        kk = k.astype(F32)[:, None, None]
        pm = jnp.exp(lam_re * dt * kk)
        return pm * jnp.cos(lam_im * dt * kk), pm * jnp.sin(lam_im * dt * kk)

    lags = jnp.arange(tc)
    pr, pi = a_pow(lags)
    abr = pr[..., None] * bb_re - pi[..., None] * bb_im
    abi = pr[..., None] * bb_im + pi[..., None] * bb_re
    kern = (jnp.einsum('gon,lgnc->lgoc', cr, abr, precision=hi)
            - jnp.einsum('gon,lgnc->lgoc', ci, abi, precision=hi))
    skip = d_skip.astype(F32).reshape(SSM_GROUPS, SSM_GROUP)
    kern = kern.at[0].add(skip[:, :, None] * jnp.eye(SSM_GROUP, dtype=F32))

    kt = jnp.transpose(kern, (0, 1, 3, 2)).reshape(tc, N_OCTETS, LANES, SSM_GROUP)
    kt = jnp.transpose(kt, (1, 0, 2, 3))
    kblk = jnp.where(_octet_mask(SSM_GROUP, SSM_GROUP, LANES, LANES),
                     jnp.tile(kt, (1, 1, 1, SSM_OCTET)), 0.0)

    def p_half(ab):
        src = ab[::-1].reshape(tc, N_OCTETS, SSM_OCTET, SSM_STATE, SSM_GROUP)
        src = jnp.transpose(src, (1, 0, 2, 4, 3)).reshape(N_OCTETS, tc, LANES, SSM_STATE)
        return jnp.where(_octet_mask(SSM_GROUP, SSM_STATE, LANES, OCTET_STATE),
                         jnp.tile(src, (1, 1, 1, SSM_OCTET)), 0.0)

    pmat = jnp.concatenate([p_half(abr), p_half(abi)], axis=-1).reshape(
        N_OCTETS, tc * LANES, 2 * OCTET_STATE)

    qr_pow, qi_pow = a_pow(lags + 1)
    q_sr = cr[None] * qr_pow[:, :, None, :] - ci[None] * qi_pow[:, :, None, :]
    q_si = -(cr[None] * qi_pow[:, :, None, :] + ci[None] * qr_pow[:, :, None, :])

    def q_half(qq):
        src = qq.reshape(tc, N_OCTETS, SSM_OCTET, SSM_GROUP, SSM_STATE)
        src = jnp.transpose(src, (1, 2, 4, 0, 3)).reshape(N_OCTETS, OCTET_STATE, tc, SSM_GROUP)
        mask = _octet_mask(SSM_STATE, SSM_GROUP, OCTET_STATE, LANES)[None, :, None, :]
        return jnp.where(mask, jnp.tile(src, (1, 1, 1, SSM_OCTET)), 0.0).reshape(
            N_OCTETS, OCTET_STATE, tc * LANES)

    qmat = jnp.concatenate([q_half(q_sr), q_half(q_si)], axis=1)

    ac_re, ac_im = a_pow(jnp.array([tc]))
    a_chunk = jnp.concatenate([ac_re.reshape(N_OCTETS, 1, OCTET_STATE),
                               ac_im.reshape(N_OCTETS, 1, OCTET_STATE)], axis=-1)
    return kblk.astype(BF16), pmat.astype(BF16), qmat.astype(BF16), a_chunk


def _s5_kernel(u_ref, kblk_ref, p_ref, q_ref, a_ref, o_ref, k_scr, x_scr, s_scr, z_scr):
    n_chunks = x_scr.shape[0]
    tc = SSM_CHUNK

    @pl.when(pl.program_id(1) == 0)
    def _():
        k_scr[...] = jnp.zeros(k_scr.shape, BF16)
        for s in range(tc):
            for t in range(s, tc):
                k_scr[s * LANES:(s + 1) * LANES, t * LANES:(t + 1) * LANES] = kblk_ref[0, t - s]

    inc = None
    for pair in range(tc // 2):
        for t in (2 * pair, 2 * pair + 1):
            x_scr[:, t * LANES:(t + 1) * LANES] = (
                u_ref[0, pl.ds(t, n_chunks, stride=tc), :].astype(BF16))
        rows = slice(pair * MXU_WIDTH, (pair + 1) * MXU_WIDTH)
        part = jnp.dot(x_scr[:, rows], p_ref[0, rows, :], preferred_element_type=F32)
        inc = part if inc is None else inc + part
    s_scr[...] = inc

    a_re = a_ref[0, :, 0:OCTET_STATE]
    a_im = a_ref[0, :, OCTET_STATE:2 * OCTET_STATE]

    def step(j, carry):
        c_re, c_im = carry
        row = pl.ds(j, 1)
        z_scr[row, 0:OCTET_STATE] = c_re
        z_scr[row, OCTET_STATE:2 * OCTET_STATE] = c_im
        inc_re = s_scr[row, 0:OCTET_STATE]
        inc_im = s_scr[row, OCTET_STATE:2 * OCTET_STATE]
        return (a_re * c_re - a_im * c_im + inc_re, a_re * c_im + a_im * c_re + inc_im)

    zero = jnp.zeros((1, OCTET_STATE), F32)
    lax.fori_loop(0, n_chunks, step, (zero, zero))

    zb = z_scr[...].astype(BF16)
    for pair in range(tc // 2):
        cols = slice(pair * MXU_WIDTH, (pair + 1) * MXU_WIDTH)
        live = (2 * pair + 2) * LANES
        y = (jnp.dot(x_scr[:, 0:live], k_scr[0:live, cols], preferred_element_type=F32)
             + jnp.dot(zb, q_ref[0, :, cols], preferred_element_type=F32))
        y = jax.nn.gelu(y)
        o_ref[0, pl.ds(2 * pair, n_chunks, stride=tc), :] = y[:, 0:LANES]
        o_ref[0, pl.ds(2 * pair + 1, n_chunks, stride=tc), :] = y[:, LANES:2 * LANES]


def _s5(u, mats, batch, seq):
    kblk, pmat, qmat, a_chunk = mats
    n_chunks = seq // SSM_CHUNK
    width = SSM_CHUNK * LANES

    def per_octet(shape):
        return pl.BlockSpec((1,) + shape, lambda o, b: (o,) + (0,) * len(shape),
                            pipeline_mode=pl.Buffered(1))

    return pl.pallas_call(
        _s5_kernel,
        grid=(N_OCTETS, batch),
        in_specs=[pl.BlockSpec((1, seq, LANES), lambda o, b: (o, b, 0)),
                  per_octet((SSM_CHUNK, LANES, LANES)),
                  per_octet((width, 2 * OCTET_STATE)),
                  per_octet((2 * OCTET_STATE, width)),
                  per_octet((1, 2 * OCTET_STATE))],
        out_specs=pl.BlockSpec((1, seq, LANES), lambda o, b: (o, b, 0)),
        out_shape=jax.ShapeDtypeStruct((N_OCTETS, batch * seq, LANES), F32),
        scratch_shapes=[pltpu.VMEM((width, width), BF16),
                        pltpu.VMEM((n_chunks, width), BF16),
                        pltpu.VMEM((n_chunks, 2 * OCTET_STATE), F32),
                        pltpu.VMEM((n_chunks, 2 * OCTET_STATE), F32)],
        compiler_params=pltpu.CompilerParams(
            dimension_semantics=("arbitrary", "arbitrary"), vmem_limit_bytes=VMEM_LIMIT_BYTES),
        name="s5_mixer",
    )(u, kblk, pmat, qmat, a_chunk)


def _merge_ffn_kernel(y_ref, oa_ref, gate_ref, x_ref, wglu_ref, wssm_ref, wattn_ref, wout_ref,
                      g_ref, wg_ref, wu_ref, wd_ref, fn_ref, o_ref, *, final):
    y_gelu = jnp.concatenate([y_ref[o].astype(BF16) for o in range(N_OCTETS)], axis=1)
    glu = jnp.dot(y_gelu, wglu_ref[...], preferred_element_type=F32)
    ys = (glu[:, 0:SSM_WIDTH] * jax.nn.sigmoid(glu[:, SSM_WIDTH:2 * SSM_WIDTH])).astype(BF16)
    y_ssm = jnp.dot(ys, wssm_ref[...], preferred_element_type=F32)
    y_attn = jnp.dot(oa_ref[...], wattn_ref[...], preferred_element_type=F32)
    g_attn = gate_ref[:, 0:D_MODEL].astype(F32)
    g_ssm = gate_ref[:, D_MODEL:2 * D_MODEL].astype(F32)
    mixed = (g_attn * y_attn + g_ssm * y_ssm).astype(BF16)
    x = x_ref[...] + jnp.dot(mixed, wout_ref[...], preferred_element_type=F32)
    y = _swiglu_residual(x, g_ref[...], wg_ref, wu_ref, wd_ref)
    o_ref[...] = _rms(y, fn_ref[...]) if final else y


def _merge_ffn(y_gelu, o_attn, gates, x, w_glu, w_ssm, w_attn, w_out, g, wg, wu, wd, final_g, final):
    n_tok = x.shape[0]
    tm = MERGE_TILE

    def rows(width):
        return pl.BlockSpec((tm, width), lambda i: (i, 0))

    return pl.pallas_call(
        functools.partial(_merge_ffn_kernel, final=final),
        grid=(n_tok // tm,),
        in_specs=[pl.BlockSpec((N_OCTETS, tm, LANES), lambda i: (0, i, 0)), rows(GROUP_WIDTH), rows(2 * D_MODEL), rows(D_MODEL),
                  _const_spec((SSM_WIDTH, 2 * SSM_WIDTH)), _const_spec((SSM_WIDTH, D_MODEL)),
                  _const_spec((GROUP_WIDTH, D_MODEL)), _const_spec((D_MODEL, D_MODEL))]
                 + _ffn_weight_specs() + [_const_spec((1, D_MODEL))],
        out_specs=rows(D_MODEL),
        out_shape=jax.ShapeDtypeStruct((n_tok, D_MODEL), F32),
        compiler_params=pltpu.CompilerParams(
            dimension_semantics=("arbitrary",), vmem_limit_bytes=VMEM_LIMIT_BYTES),
        name="merge_ffn",
    )(y_gelu, o_attn, gates, x, w_glu, w_ssm, w_attn, w_out,
      g.reshape(1, D_MODEL), wg, wu, wd, final_g.reshape(1, D_MODEL))


def kernel(x, ffn1_norm, ffn1_w_gate, ffn1_w_up, ffn1_w_down, mix_norm, w_in, gate_bias,
           rel_bias_table, ssm_a_re, ssm_a_im, ssm_log_dt, ssm_b_re, ssm_b_im, ssm_c_re,
           ssm_c_im, ssm_d, ssm_w_glu, w_attn_branch, w_ssm_branch, w_out, ffn2_norm,
           ffn2_w_gate, ffn2_w_up, ffn2_w_down, final_norm):
    batch, seq, _ = x.shape
    depth = ffn1_norm.shape[0]
    h = x.reshape(batch * seq, D_MODEL)
    for l in range(depth):
        h = _ffn(h, ffn1_norm[l], ffn1_w_gate[l].astype(BF16), ffn1_w_up[l].astype(BF16),
                 ffn1_w_down[l].astype(BF16))
        q, k, v, u, gates = _in_proj(h, mix_norm[l], w_in[l].astype(BF16), gate_bias[l], batch, seq)
        o_attn = _attention(q, k, v, rel_bias_table, batch, seq)
        mats = _s5_matrices(ssm_a_re[l], ssm_a_im[l], ssm_log_dt[l], ssm_b_re[l], ssm_b_im[l],
                            ssm_c_re[l], ssm_c_im[l], ssm_d[l])
        y_gelu = _s5(u, mats, batch, seq)
        h = _merge_ffn(y_gelu, o_attn, gates, h, ssm_w_glu[l].astype(BF16),
                       w_ssm_branch[l].astype(BF16), w_attn_branch[l].astype(BF16),
                       w_out[l].astype(BF16), ffn2_norm[l], ffn2_w_gate[l].astype(BF16),
                       ffn2_w_up[l].astype(BF16), ffn2_w_down[l].astype(BF16), final_norm,
                       final=(l == depth - 1))
    return h.reshape(batch, seq, D_MODEL)
```

```python
import functools
import math

import jax
import jax.numpy as jnp
import numpy as np
from jax import lax
from jax.experimental import pallas as pl
from jax.experimental.pallas import tpu as pltpu

F32 = jnp.float32
BF16 = jnp.bfloat16

D_MODEL = 1024
D_FF = 2816
EPS = 1e-6
HEAD_DIM = 64
HEADS_PER_GROUP = 4
GROUP_WIDTH = HEADS_PER_GROUP * HEAD_DIM
DILATIONS = (1, 4, 16)
WINDOWS = (128, 512, 2048)
N_GROUPS = len(DILATIONS)
ATTN_WIDTH = N_GROUPS * GROUP_WIDTH
BLOCK = 128
N_BUCKETS = 32
MAX_DISTANCE = 2048
NEG_INF = -1e30
SSM_GROUP = 16
SSM_WIDTH = 512
SSM_GROUPS = SSM_WIDTH // SSM_GROUP
SSM_STATE = 64
IN_WIDTH = 3 * ATTN_WIDTH + SSM_WIDTH + 2 * D_MODEL

LANES = 128
MXU_WIDTH = 256
VMEM_LIMIT_BYTES = 56 * 1024 * 1024

FFN_TILE = 1024
PROJ_TILE = 1024
MERGE_TILE = 512
ATTN_TILE = BLOCK * DILATIONS[-1]
ATTN_INTERLEAVE = 4
SSM_CHUNK = 16
SSM_OCTET = LANES // SSM_GROUP
N_OCTETS = SSM_GROUPS // SSM_OCTET
OCTET_STATE = SSM_OCTET * SSM_STATE


def _rms(x, g):
    return x * lax.rsqrt(jnp.mean(x * x, axis=-1, keepdims=True) + EPS) * g


def _const_spec(shape):
    return pl.BlockSpec(shape, lambda *_: (0,) * len(shape), pipeline_mode=pl.Buffered(1))


def _swiglu_residual(x, g, wg_ref, wu_ref, wd_ref):
    h = _rms(x, g).astype(BF16)
    acc = jnp.zeros(x.shape, F32)
    for c in range(D_FF // MXU_WIDTH):
        sl = slice(c * MXU_WIDTH, (c + 1) * MXU_WIDTH)
        a = jnp.dot(h, wg_ref[:, sl], preferred_element_type=F32)
        b = jnp.dot(h, wu_ref[:, sl], preferred_element_type=F32)
        t = (a * jax.nn.sigmoid(a) * b).astype(BF16)
        acc = acc + jnp.dot(t, wd_ref[sl, :], preferred_element_type=F32)
    return x + 0.5 * acc


def _ffn_kernel(x_ref, g_ref, wg_ref, wu_ref, wd_ref, o_ref):
    o_ref[...] = _swiglu_residual(x_ref[...], g_ref[...], wg_ref, wu_ref, wd_ref)


def _ffn_weight_specs():
    return [_const_spec((1, D_MODEL)), _const_spec((D_MODEL, D_FF)),
            _const_spec((D_MODEL, D_FF)), _const_spec((D_FF, D_MODEL))]


def _ffn(x, g, wg, wu, wd):
    n_tok = x.shape[0]
    tm = FFN_TILE
    tile = pl.BlockSpec((tm, D_MODEL), lambda i: (i, 0))
    return pl.pallas_call(
        _ffn_kernel,
        grid=(n_tok // tm,),
        in_specs=[tile] + _ffn_weight_specs(),
        out_specs=tile,
        out_shape=jax.ShapeDtypeStruct((n_tok, D_MODEL), F32),
        compiler_params=pltpu.CompilerParams(
            dimension_semantics=("arbitrary",), vmem_limit_bytes=VMEM_LIMIT_BYTES),
        name="ffn",
    )(x, g.reshape(1, D_MODEL), wg, wu, wd)


N_QKV_CHUNKS = 3 * N_GROUPS
N_U_CHUNKS = SSM_WIDTH // MXU_WIDTH
N_GATE_CHUNKS = 2 * D_MODEL // MXU_WIDTH
N_STRIDED = 3 * (N_GROUPS - 1)


def _inproj_kernel(x_ref, g_ref, w_ref, gb_ref, *refs):
    qkv_refs = refs[:N_QKV_CHUNKS]
    u_ref, gate_ref, slab_ref = refs[N_QKV_CHUNKS:]
    tm = x_ref.shape[0]
    h = _rms(x_ref[...], g_ref[...]).astype(BF16)
    strided_idx = 0
    for c in range(IN_WIDTH // MXU_WIDTH):
        z = jnp.dot(h, w_ref[:, c * MXU_WIDTH:(c + 1) * MXU_WIDTH], preferred_element_type=F32)
        if c < N_QKV_CHUNKS:
            kind, grp = divmod(c, N_GROUPS)
            if kind == 0:
                z = z * (HEAD_DIM ** -0.5)
            d = DILATIONS[grp]
            o_ref = qkv_refs[c]
            if d == 1:
                o_ref[0, 0] = z.astype(BF16)
            else:
                base = 2 * strided_idx
                strided_idx += 1
                for s in range(2):
                    slab_ref[base + s] = z[:, s * LANES:(s + 1) * LANES]
                for r in range(d):
                    for s in range(2):
                        o_ref[0, r, :, s * LANES:(s + 1) * LANES] = (
                            slab_ref[base + s, pl.ds(r, tm // d, stride=d), :].astype(BF16))
        elif c < N_QKV_CHUNKS + N_U_CHUNKS:
            j = c - N_QKV_CHUNKS
            for s in range(2):
                u_ref[2 * j + s] = z[:, s * LANES:(s + 1) * LANES]
        else:
            j = c - N_QKV_CHUNKS - N_U_CHUNKS
            sl = slice(j * MXU_WIDTH, (j + 1) * MXU_WIDTH)
            gate_ref[:, sl] = jax.nn.sigmoid(z + gb_ref[:, sl]).astype(BF16)


def _in_proj(x, g, w_in, gate_bias, batch, seq):
    n_tok = x.shape[0]
    tm = PROJ_TILE
    tiles_per_seq = seq // tm
    out_shapes, out_specs = [], []
    for _ in range(3):
        for d in DILATIONS:
            out_shapes.append(jax.ShapeDtypeStruct((batch, d, seq // d, GROUP_WIDTH), BF16))
            out_specs.append(pl.BlockSpec(
                (1, d, tm // d, GROUP_WIDTH),
                lambda i: (i // tiles_per_seq, 0, i % tiles_per_seq, 0)))
    out_shapes.append(jax.ShapeDtypeStruct((N_OCTETS, n_tok, LANES), F32))
    out_specs.append(pl.BlockSpec((N_OCTETS, tm, LANES), lambda i: (0, i, 0)))
    out_shapes.append(jax.ShapeDtypeStruct((n_tok, 2 * D_MODEL), BF16))
    out_specs.append(pl.BlockSpec((tm, 2 * D_MODEL), lambda i: (i, 0)))
    outs = pl.pallas_call(
        _inproj_kernel,
        grid=(n_tok // tm,),
        in_specs=[pl.BlockSpec((tm, D_MODEL), lambda i: (i, 0)),
                  _const_spec((1, D_MODEL)),
                  _const_spec((D_MODEL, IN_WIDTH)),
                  _const_spec((1, 2 * D_MODEL))],
        out_specs=out_specs,
        out_shape=out_shapes,
        scratch_shapes=[pltpu.VMEM((2 * N_STRIDED, tm, LANES), F32)],
        compiler_params=pltpu.CompilerParams(
            dimension_semantics=("arbitrary",), vmem_limit_bytes=VMEM_LIMIT_BYTES),
        name="in_proj",
    )(x, g.reshape(1, D_MODEL), w_in, gate_bias.reshape(1, 2 * D_MODEL))
    q, k, v = outs[0:3], outs[3:6], outs[6:9]
    return q, k, v, outs[9], outs[10]


def _t5_bucket_np(dist):
    max_exact = N_BUCKETS // 2
    d = np.maximum(dist, 1).astype(np.float32)
    ratio = np.log(d / np.float32(max_exact)) / np.float32(math.log(MAX_DISTANCE / max_exact))
    large = max_exact + (ratio * np.float32(N_BUCKETS - max_exact)).astype(np.int32)
    large = np.minimum(large, N_BUCKETS - 1)
    return np.where(dist < max_exact, dist, large)


def _bucket_tables():
    qi = np.arange(BLOCK)[:, None]
    kj = np.arange(2 * BLOCK)[None, :]
    steps = qi + BLOCK - kj
    tables = []
    for window, d in zip(WINDOWS, DILATIONS):
        band = (steps >= 0) & (steps <= window // d)
        bucket = _t5_bucket_np(np.maximum(steps, 0) * d)
        tables.append(np.where(band, bucket, -1).astype(np.int32))
    return jnp.asarray(np.stack(tables))


def _attn_block(q_blk, k_blk, v_blk, bias):
    lane_head = lax.broadcasted_iota(jnp.int32, (BLOCK, GROUP_WIDTH), 1) // HEAD_DIM
    zero = jnp.zeros_like(q_blk)
    qs = jnp.concatenate(
        [jnp.where(lane_head == h, q_blk, zero) for h in range(HEADS_PER_GROUP)], axis=0)
    logits = lax.dot_general(qs, k_blk, (((1,), (1,)), ((), ())), preferred_element_type=F32) + bias
    m = jnp.max(logits, axis=-1, keepdims=True)
    p = jnp.exp(logits - m)
    l = jnp.sum(p, axis=-1, keepdims=True)
    pv = jnp.dot(p.astype(BF16), v_blk, preferred_element_type=F32)
    o = jnp.zeros((BLOCK, GROUP_WIDTH), F32)
    den = jnp.ones((BLOCK, GROUP_WIDTH), F32)
    mx = jnp.zeros((BLOCK, GROUP_WIDTH), F32)
    for h in range(HEADS_PER_GROUP):
        rows = slice(h * BLOCK, (h + 1) * BLOCK)
        sel = lane_head == h
        o = jnp.where(sel, pv[rows], o)
        den = jnp.where(sel, l[rows], den)
        mx = jnp.where(sel, m[rows], mx)
    return o / den, mx + jnp.log(den)


def _build_bias(tab_ref, bucket_ref, bias_scr):
    own_block = lax.broadcasted_iota(jnp.int32, (BLOCK, 2 * BLOCK), 1) >= BLOCK
    for grp in range(N_GROUPS):
        bucket = bucket_ref[grp]
        for h in range(HEADS_PER_GROUP):
            col = grp * HEADS_PER_GROUP + h

            def pick(b, acc, bucket=bucket, col=col):
                return jnp.where(bucket == b, tab_ref[b, col], acc)

            bias = lax.fori_loop(0, N_BUCKETS, pick, jnp.full((BLOCK, 2 * BLOCK), NEG_INF, F32))
            rows = slice(h * BLOCK, (h + 1) * BLOCK)
            bias_scr[grp, 0, rows, :] = bias
            bias_scr[grp, 1, rows, :] = jnp.where(own_block, bias, NEG_INF)


def _block_aligned(row):
    return row if isinstance(row, int) else pl.multiple_of(row, BLOCK)


def _attn_kernel(*refs):
    n = N_GROUPS
    tab_ref, bucket_ref = refs[0:2]
    refs = refs[2:]
    q_refs, kc_refs, kp_refs = refs[0:n], refs[n:2 * n], refs[2 * n:3 * n]
    vc_refs, vp_refs = refs[3 * n:4 * n], refs[4 * n:5 * n]
    o_ref, bias_scr, o_scr, l_scr = refs[5 * n:]
    tile = pl.program_id(1)

    @pl.when((pl.program_id(0) == 0) & (tile == 0))
    def _():
        _build_bias(tab_ref, bucket_ref, bias_scr)

    for grp, d in enumerate(DILATIONS):
        q_ref, kc_ref, kp_ref = q_refs[grp], kc_refs[grp], kp_refs[grp]
        vc_ref, vp_ref = vc_refs[grp], vp_refs[grp]
        blocks_per_residue = ATTN_TILE // d // BLOCK

        def run_block(r, blk, grp=grp, d=d, q_ref=q_ref, kc_ref=kc_ref, kp_ref=kp_ref,
                      vc_ref=vc_ref, vp_ref=vp_ref):
            if isinstance(blk, int) and blk == 0:
                k_blk = jnp.concatenate([kp_ref[0, r], kc_ref[0, r, 0:BLOCK, :]], axis=0)
                v_blk = jnp.concatenate([vp_ref[0, r], vc_ref[0, r, 0:BLOCK, :]], axis=0)
                bias = jnp.where(tile == 0, bias_scr[grp, 1], bias_scr[grp, 0])
                q_blk = q_ref[0, r, 0:BLOCK, :]
            else:
                kv_rows = pl.ds(_block_aligned((blk - 1) * BLOCK), 2 * BLOCK)
                k_blk = kc_ref[0, r, kv_rows, :]
                v_blk = vc_ref[0, r, kv_rows, :]
                bias = bias_scr[grp, 0]
                q_blk = q_ref[0, r, pl.ds(_block_aligned(blk * BLOCK), BLOCK), :]
            o, lse = _attn_block(q_blk, k_blk, v_blk, bias)
            start = blk * (BLOCK * d) + r
            for s in range(2):
                rows = pl.ds(start, BLOCK, stride=d) if d > 1 else pl.ds(start, BLOCK)
                o_scr[grp, s, rows, :] = o[:, s * LANES:(s + 1) * LANES]
                l_scr[grp, s, rows, :] = lse[:, s * LANES:(s + 1) * LANES]

        if blocks_per_residue == 1:
            def residue_set(i, c, run_block=run_block):
                for k in range(ATTN_INTERLEAVE):
                    run_block(ATTN_INTERLEAVE * i + k, 0)
                return c
            lax.fori_loop(0, d // ATTN_INTERLEAVE, residue_set, 0)
        elif d > 1:
            def residue(r, c, run_block=run_block, blocks_per_residue=blocks_per_residue):
                run_block(r, 0)
                for blk in range(1, blocks_per_residue):
                    run_block(r, blk)
                return c
            lax.fori_loop(0, d, residue, 0)
        else:
            for blk in range(ATTN_INTERLEAVE):
                run_block(0, blk)

            def block_set(i, c, run_block=run_block):
                for k in range(ATTN_INTERLEAVE):
                    run_block(0, ATTN_INTERLEAVE * i + k)
                return c
            lax.fori_loop(1, blocks_per_residue // ATTN_INTERLEAVE, block_set, 0)

    merge_rows = 256

    def merge(c, carry):
        rows = pl.ds(pl.multiple_of(c * merge_rows, merge_rows), merge_rows)
        for s in range(2):
            lses = [l_scr[grp, s, rows, :] for grp in range(N_GROUPS)]
            mx = functools.reduce(jnp.maximum, lses)
            es = [jnp.exp(x - mx) for x in lses]
            num = sum(e * o_scr[grp, s, rows, :] for grp, e in enumerate(es))
            o_ref[rows, s * LANES:(s + 1) * LANES] = (num / sum(es)).astype(BF16)
        return carry

    lax.fori_loop(0, ATTN_TILE // merge_rows, merge, 0)


def _attention(q, k, v, rel_bias_table, batch, seq):
    tiles = seq // ATTN_TILE

    def cur_spec(d):
        return pl.BlockSpec((1, d, ATTN_TILE // d, GROUP_WIDTH), lambda b, j: (b, 0, j, 0))

    def prev_spec(d):
        per_tile = ATTN_TILE // d // BLOCK
        return pl.BlockSpec((1, d, BLOCK, GROUP_WIDTH),
                            lambda b, j: (b, 0, jnp.maximum(j * per_tile - 1, 0), 0))

    in_specs = ([pl.BlockSpec(memory_space=pltpu.SMEM),
                 _const_spec((N_GROUPS, BLOCK, 2 * BLOCK))]
                + [cur_spec(d) for d in DILATIONS] + [cur_spec(d) for d in DILATIONS]
                + [prev_spec(d) for d in DILATIONS] + [cur_spec(d) for d in DILATIONS]
                + [prev_spec(d) for d in DILATIONS])
    return pl.pallas_call(
        _attn_kernel,
        grid=(batch, tiles),
        in_specs=in_specs,
        out_specs=pl.BlockSpec((ATTN_TILE, GROUP_WIDTH), lambda b, j: (b * tiles + j, 0)),
        out_shape=jax.ShapeDtypeStruct((batch * seq, GROUP_WIDTH), BF16),
        scratch_shapes=[pltpu.VMEM((N_GROUPS, 2, HEADS_PER_GROUP * BLOCK, 2 * BLOCK), F32),
                        pltpu.VMEM((N_GROUPS, 2, ATTN_TILE, LANES), F32),
                        pltpu.VMEM((N_GROUPS, 2, ATTN_TILE, LANES), F32)],
        compiler_params=pltpu.CompilerParams(
            dimension_semantics=("arbitrary", "arbitrary"), vmem_limit_bytes=VMEM_LIMIT_BYTES),
        name="dilated_attn",
    )(rel_bias_table.astype(F32), _bucket_tables(), *q, *k, *k, *v, *v)


def _octet_mask(row_group, col_group, rows, cols):
    r = np.arange(rows)[:, None] // row_group
    c = np.arange(cols)[None, :] // col_group
    return jnp.asarray(r == c)


def _s5_matrices(a_re, a_im, log_dt, b_re, b_im, c_re, c_im, d_skip):
    hi = lax.Precision.HIGHEST
    tc = SSM_CHUNK
    lam_re = a_re.astype(F32)
    lam_im = a_im.astype(F32)
    dt = jnp.exp(log_dt.astype(F32))[:, None]
    mag = jnp.exp(lam_re * dt)
    ab_re = mag * jnp.cos(lam_im * dt)
    ab_im = mag * jnp.sin(lam_im * dt)
    den = lam_re * lam_re + lam_im * lam_im
    xr = ab_re - 1.0
    coef_re = (xr * lam_re + ab_im * lam_im) / den
    coef_im = (ab_im * lam_re - xr * lam_im) / den
    br = b_re.astype(F32)
    bi = b_im.astype(F32)
    bb_re = coef_re[..., None] * br - coef_im[..., None] * bi
    bb_im = coef_re[..., None] * bi + coef_im[..., None] * br
    cr = c_re.astype(F32)
    ci = c_im.astype(F32)

    def a_pow(k):
        kk = k.astype(F32)[:, None, None]
        pm = jnp.exp(lam_re * dt * kk)
        return pm * jnp.cos(lam_im * dt * kk), pm * jnp.sin(lam_im * dt * kk)

    pw_re, pw_im = a_pow(jnp.arange(tc + 1))
    pr, pi = pw_re[:tc], pw_im[:tc]
    abr = pr[..., None] * bb_re - pi[..., None] * bb_im
    abi = pr[..., None] * bb_im + pi[..., None] * bb_re
    kern = (jnp.einsum('gon,lgnc->lgoc', cr, abr, precision=hi)
            - jnp.einsum('gon,lgnc->lgoc', ci, abi, precision=hi))
    skip = d_skip.astype(F32).reshape(SSM_GROUPS, SSM_GROUP)
    kern = kern.at[0].add(skip[:, :, None] * jnp.eye(SSM_GROUP, dtype=F32))

    kt = jnp.transpose(kern, (0, 1, 3, 2)).reshape(tc, N_OCTETS, LANES, SSM_GROUP)
    kt = jnp.transpose(kt, (1, 0, 2, 3))
    kblk = jnp.where(_octet_mask(SSM_GROUP, SSM_GROUP, LANES, LANES),
                     jnp.tile(kt, (1, 1, 1, SSM_OCTET)), 0.0)

    def by_octet(x):
        return x.reshape(x.shape[0], N_OCTETS, 1, SSM_OCTET, SSM_STATE)

    def channel_major(x):
        x = x.reshape(N_OCTETS, SSM_OCTET, SSM_GROUP, SSM_STATE)
        return jnp.transpose(x, (0, 2, 1, 3))[None]

    def compact(re, im):
        both = jnp.stack([re, im], axis=1)
        return both.reshape(tc, 2, N_OCTETS, SSM_GROUP, OCTET_STATE).astype(BF16)

    lr, li = by_octet(pr[::-1]), by_octet(pi[::-1])
    bt_re = channel_major(jnp.transpose(bb_re, (0, 2, 1)))
    bt_im = channel_major(jnp.transpose(bb_im, (0, 2, 1)))
    p_src = compact(lr * bt_re - li * bt_im, lr * bt_im + li * bt_re)

    qr, qi = by_octet(pw_re[1:]), by_octet(pw_im[1:])
    ct_re, ct_im = channel_major(cr), channel_major(ci)
    q_src = compact(ct_re * qr - ct_im * qi, -(ct_re * qi + ct_im * qr))

    a_chunk = jnp.concatenate([pw_re[tc].reshape(N_OCTETS, 1, OCTET_STATE),
                               pw_im[tc].reshape(N_OCTETS, 1, OCTET_STATE)], axis=-1)
    return kblk.astype(BF16), p_src, q_src, a_chunk


def _s5_kernel(u_ref, kblk_ref, p_ref, q_ref, a_ref, o_ref, k_scr, p_scr, qt_scr, x_scr, s_scr, z_scr):
    n_chunks = x_scr.shape[0]
    tc = SSM_CHUNK

    @pl.when(pl.program_id(1) == 0)
    def _():
        k_scr[...] = jnp.zeros(k_scr.shape, BF16)
        for s in range(tc):
            for t in range(s, tc):
                k_scr[s * LANES:(s + 1) * LANES, t * LANES:(t + 1) * LANES] = kblk_ref[0, t - s]
        row_group = lax.broadcasted_iota(jnp.int32, (LANES, OCTET_STATE), 0) // SSM_GROUP
        lane_group = lax.broadcasted_iota(jnp.int32, (LANES, OCTET_STATE), 1) // SSM_STATE
        same_group = row_group == lane_group
        zeros = jnp.zeros((LANES, OCTET_STATE), BF16)
        for src_ref, dst in ((p_ref, p_scr), (q_ref, qt_scr)):
            for t in range(tc):
                for part in range(2):
                    src = src_ref[t, part, 0]
                    tiled = jnp.concatenate([src] * SSM_OCTET, axis=0)
                    dst[t * LANES:(t + 1) * LANES, part * OCTET_STATE:(part + 1) * OCTET_STATE] = (
                        jnp.where(same_group, tiled, zeros))

    inc = None
    for pair in range(tc // 2):
        for t in (2 * pair, 2 * pair + 1):
            x_scr[:, t * LANES:(t + 1) * LANES] = (
                u_ref[0, pl.ds(t, n_chunks, stride=tc), :].astype(BF16))
        rows = slice(pair * MXU_WIDTH, (pair + 1) * MXU_WIDTH)
        part = jnp.dot(x_scr[:, rows], p_scr[rows, :], preferred_element_type=F32)
        inc = part if inc is None else inc + part
    s_scr[...] = inc

    a_re = a_ref[0, :, 0:OCTET_STATE]
    a_im = a_ref[0, :, OCTET_STATE:2 * OCTET_STATE]

    def step(j, carry):
        c_re, c_im = carry
        row = pl.ds(j, 1)
        z_scr[row, 0:OCTET_STATE] = c_re
        z_scr[row, OCTET_STATE:2 * OCTET_STATE] = c_im
        inc_re = s_scr[row, 0:OCTET_STATE]
        inc_im = s_scr[row, OCTET_STATE:2 * OCTET_STATE]
        return (a_re * c_re - a_im * c_im + inc_re, a_re * c_im + a_im * c_re + inc_im)

    zero = jnp.zeros((1, OCTET_STATE), F32)
    lax.fori_loop(0, n_chunks, step, (zero, zero))

    zb = z_scr[...].astype(BF16)
    for pair in range(tc // 2):
        cols = slice(pair * MXU_WIDTH, (pair + 1) * MXU_WIDTH)
        live = (2 * pair + 2) * LANES
        y = (jnp.dot(x_scr[:, 0:live], k_scr[0:live, cols], preferred_element_type=F32)
             + lax.dot_general(zb, qt_scr[cols, :], (((1,), (1,)), ((), ())),
                               preferred_element_type=F32))
        y = jax.nn.gelu(y)
        o_ref[0, pl.ds(2 * pair, n_chunks, stride=tc), :] = y[:, 0:LANES]
        o_ref[0, pl.ds(2 * pair + 1, n_chunks, stride=tc), :] = y[:, LANES:2 * LANES]


def _s5(u, mats, batch, seq):
    kblk, p_src, q_src, a_chunk = mats
    n_chunks = seq // SSM_CHUNK
    width = SSM_CHUNK * LANES

    def per_octet(shape):
        return pl.BlockSpec((1,) + shape, lambda o, b: (o,) + (0,) * len(shape),
                            pipeline_mode=pl.Buffered(1))

    compact_spec = pl.BlockSpec((SSM_CHUNK, 2, 1, SSM_GROUP, OCTET_STATE),
                                lambda o, b: (0, 0, o, 0, 0))
    return pl.pallas_call(
        _s5_kernel,
        grid=(N_OCTETS, batch),
        in_specs=[pl.BlockSpec((1, seq, LANES), lambda o, b: (o, b, 0)),
                  per_octet((SSM_CHUNK, LANES, LANES)),
                  compact_spec,
                  compact_spec,
                  per_octet((1, 2 * OCTET_STATE))],
        out_specs=pl.BlockSpec((1, seq, LANES), lambda o, b: (o, b, 0)),
        out_shape=jax.ShapeDtypeStruct((N_OCTETS, batch * seq, LANES), F32),
        scratch_shapes=[pltpu.VMEM((width, width), BF16),
                        pltpu.VMEM((width, 2 * OCTET_STATE), BF16),
                        pltpu.VMEM((width, 2 * OCTET_STATE), BF16),
                        pltpu.VMEM((n_chunks, width), BF16),
                        pltpu.VMEM((n_chunks, 2 * OCTET_STATE), F32),
                        pltpu.VMEM((n_chunks, 2 * OCTET_STATE), F32)],
        compiler_params=pltpu.CompilerParams(
            dimension_semantics=("arbitrary", "arbitrary"), vmem_limit_bytes=VMEM_LIMIT_BYTES),
        name="s5_mixer",
    )(u, kblk, p_src, q_src, a_chunk)


def _merge_ffn_kernel(y_ref, oa_ref, gate_ref, x_ref, wglu_ref, wssm_ref, wattn_ref, wout_ref,
                      g_ref, wg_ref, wu_ref, wd_ref, fn_ref, o_ref, *, final):
    y_gelu = jnp.concatenate([y_ref[o].astype(BF16) for o in range(N_OCTETS)], axis=1)
    glu = jnp.dot(y_gelu, wglu_ref[...], preferred_element_type=F32)
    ys = (glu[:, 0:SSM_WIDTH] * jax.nn.sigmoid(glu[:, SSM_WIDTH:2 * SSM_WIDTH])).astype(BF16)
    y_ssm = jnp.dot(ys, wssm_ref[...], preferred_element_type=F32)
    y_attn = jnp.dot(oa_ref[...], wattn_ref[...], preferred_element_type=F32)
    g_attn = gate_ref[:, 0:D_MODEL].astype(F32)
    g_ssm = gate_ref[:, D_MODEL:2 * D_MODEL].astype(F32)
    mixed = (g_attn * y_attn + g_ssm * y_ssm).astype(BF16)
    x = x_ref[...] + jnp.dot(mixed, wout_ref[...], preferred_element_type=F32)
    y = _swiglu_residual(x, g_ref[...], wg_ref, wu_ref, wd_ref)
    o_ref[...] = _rms(y, fn_ref[...]) if final else y


def _merge_ffn(y_gelu, o_attn, gates, x, w_glu, w_ssm, w_attn, w_out, g, wg, wu, wd, final_g, final):
    n_tok = x.shape[0]
    tm = MERGE_TILE

    def rows(width):
        return pl.BlockSpec((tm, width), lambda i: (i, 0))

    return pl.pallas_call(
        functools.partial(_merge_ffn_kernel, final=final),
        grid=(n_tok // tm,),
        in_specs=[pl.BlockSpec((N_OCTETS, tm, LANES), lambda i: (0, i, 0)), rows(GROUP_WIDTH), rows(2 * D_MODEL), rows(D_MODEL),
                  _const_spec((SSM_WIDTH, 2 * SSM_WIDTH)), _const_spec((SSM_WIDTH, D_MODEL)),
                  _const_spec((GROUP_WIDTH, D_MODEL)), _const_spec((D_MODEL, D_MODEL))]
                 + _ffn_weight_specs() + [_const_spec((1, D_MODEL))],
        out_specs=rows(D_MODEL),
        out_shape=jax.ShapeDtypeStruct((n_tok, D_MODEL), F32),
        compiler_params=pltpu.CompilerParams(
            dimension_semantics=("arbitrary",), vmem_limit_bytes=VMEM_LIMIT_BYTES),
        name="merge_ffn",
    )(y_gelu, o_attn, gates, x, w_glu, w_ssm, w_attn, w_out,
      g.reshape(1, D_MODEL), wg, wu, wd, final_g.reshape(1, D_MODEL))


def kernel(x, ffn1_norm, ffn1_w_gate, ffn1_w_up, ffn1_w_down, mix_norm, w_in, gate_bias,
           rel_bias_table, ssm_a_re, ssm_a_im, ssm_log_dt, ssm_b_re, ssm_b_im, ssm_c_re,
           ssm_c_im, ssm_d, ssm_w_glu, w_attn_branch, w_ssm_branch, w_out, ffn2_norm,
           ffn2_w_gate, ffn2_w_up, ffn2_w_down, final_norm):
    batch, seq, _ = x.shape
    depth = ffn1_norm.shape[0]
    h = x.reshape(batch * seq, D_MODEL)
    for l in range(depth):
        h = _ffn(h, ffn1_norm[l], ffn1_w_gate[l].astype(BF16), ffn1_w_up[l].astype(BF16),
                 ffn1_w_down[l].astype(BF16))
        q, k, v, u, gates = _in_proj(h, mix_norm[l], w_in[l].astype(BF16), gate_bias[l], batch, seq)
        o_attn = _attention(q, k, v, rel_bias_table, batch, seq)
        mats = _s5_matrices(ssm_a_re[l], ssm_a_im[l], ssm_log_dt[l], ssm_b_re[l], ssm_b_im[l],
                            ssm_c_re[l], ssm_c_im[l], ssm_d[l])
        y_gelu = _s5(u, mats, batch, seq)
        h = _merge_ffn(y_gelu, o_attn, gates, h, ssm_w_glu[l].astype(BF16),
                       w_ssm_branch[l].astype(BF16), w_attn_branch[l].astype(BF16),
                       w_out[l].astype(BF16), ffn2_norm[l], ffn2_w_gate[l].astype(BF16),
                       ffn2_w_up[l].astype(BF16), ffn2_w_down[l].astype(BF16), final_norm,
                       final=(l == depth - 1))
    return h.reshape(batch, seq, D_MODEL)
```

```python
import functools
import math

import jax
import jax.numpy as jnp
import numpy as np
from jax import lax
from jax.experimental import pallas as pl
from jax.experimental.pallas import tpu as pltpu

F32 = jnp.float32
BF16 = jnp.bfloat16

D_MODEL = 1024
D_FF = 2816
EPS = 1e-6
HEAD_DIM = 64
HEADS_PER_GROUP = 4
GROUP_WIDTH = HEADS_PER_GROUP * HEAD_DIM
DILATIONS = (1, 4, 16)
WINDOWS = (128, 512, 2048)
N_GROUPS = len(DILATIONS)
ATTN_WIDTH = N_GROUPS * GROUP_WIDTH
BLOCK = 128
N_BUCKETS = 32
MAX_DISTANCE = 2048
NEG_INF = -1e30
SSM_GROUP = 16
SSM_WIDTH = 512
SSM_GROUPS = SSM_WIDTH // SSM_GROUP
SSM_STATE = 64
IN_WIDTH = 3 * ATTN_WIDTH + SSM_WIDTH + 2 * D_MODEL

LANES = 128
MXU_WIDTH = 256
VMEM_LIMIT_BYTES = 56 * 1024 * 1024

FFN_TILE = 1024
PROJ_TILE = 1024
MERGE_TILE = 512
ATTN_TILE = BLOCK * DILATIONS[-1]
ATTN_INTERLEAVE = 4
SSM_CHUNK = 16
CHUNK_PITCH = SSM_CHUNK + 4
SSM_OCTET = LANES // SSM_GROUP
N_OCTETS = SSM_GROUPS // SSM_OCTET
OCTET_STATE = SSM_OCTET * SSM_STATE


def _rms(x, g):
    return x * lax.rsqrt(jnp.mean(x * x, axis=-1, keepdims=True) + EPS) * g


def _const_spec(shape):
    return pl.BlockSpec(shape, lambda *_: (0,) * len(shape), pipeline_mode=pl.Buffered(1))


def _swiglu_residual(x, g, wg_ref, wu_ref, wd_ref):
    h = _rms(x, g).astype(BF16)
    acc = jnp.zeros(x.shape, F32)
    for c in range(D_FF // MXU_WIDTH):
        sl = slice(c * MXU_WIDTH, (c + 1) * MXU_WIDTH)
        a = jnp.dot(h, wg_ref[:, sl], preferred_element_type=F32)
        b = jnp.dot(h, wu_ref[:, sl], preferred_element_type=F32)
        t = (a * jax.nn.sigmoid(a) * b).astype(BF16)
        acc = acc + jnp.dot(t, wd_ref[sl, :], preferred_element_type=F32)
    return x + 0.5 * acc


def _ffn_kernel(x_ref, g_ref, wg_ref, wu_ref, wd_ref, o_ref):
    o_ref[...] = _swiglu_residual(x_ref[...], g_ref[...], wg_ref, wu_ref, wd_ref)


def _ffn_weight_specs():
    return [_const_spec((1, D_MODEL)), _const_spec((D_MODEL, D_FF)),
            _const_spec((D_MODEL, D_FF)), _const_spec((D_FF, D_MODEL))]


def _ffn(x, g, wg, wu, wd):
    n_tok = x.shape[0]
    tm = FFN_TILE
    tile = pl.BlockSpec((tm, D_MODEL), lambda i: (i, 0))
    return pl.pallas_call(
        _ffn_kernel,
        grid=(n_tok // tm,),
        in_specs=[tile] + _ffn_weight_specs(),
        out_specs=tile,
        out_shape=jax.ShapeDtypeStruct((n_tok, D_MODEL), F32),
        compiler_params=pltpu.CompilerParams(
            dimension_semantics=("arbitrary",), vmem_limit_bytes=VMEM_LIMIT_BYTES),
        name="ffn",
    )(x, g.reshape(1, D_MODEL), wg, wu, wd)


N_QKV_CHUNKS = 3 * N_GROUPS
N_U_CHUNKS = SSM_WIDTH // MXU_WIDTH
N_GATE_CHUNKS = 2 * D_MODEL // MXU_WIDTH
N_STRIDED = 3 * (N_GROUPS - 1)


def _padded_rows(n_tokens):
    return n_tokens // SSM_CHUNK * CHUNK_PITCH


def _store_padded_chunks(ref, slab, rows):
    pad = jnp.zeros((CHUNK_PITCH - SSM_CHUNK, rows.shape[1]), rows.dtype)
    for ch in range(rows.shape[0] // SSM_CHUNK):
        base = ch * CHUNK_PITCH
        ref[slab, base:base + SSM_CHUNK, :] = rows[ch * SSM_CHUNK:(ch + 1) * SSM_CHUNK, :]
        ref[slab, base + SSM_CHUNK:base + CHUNK_PITCH, :] = pad


def _load_padded_chunks(ref, slab, n_tokens):
    return jnp.concatenate(
        [ref[slab, ch * CHUNK_PITCH:ch * CHUNK_PITCH + SSM_CHUNK, :]
         for ch in range(n_tokens // SSM_CHUNK)], axis=0)


def _inproj_kernel(x_ref, g_ref, w_ref, gb_ref, *refs):
    qkv_refs = refs[:N_QKV_CHUNKS]
    u_ref, gate_ref, slab_ref = refs[N_QKV_CHUNKS:]
    tm = x_ref.shape[0]
    h = _rms(x_ref[...], g_ref[...]).astype(BF16)
    strided_idx = 0
    for c in range(IN_WIDTH // MXU_WIDTH):
        z = jnp.dot(h, w_ref[:, c * MXU_WIDTH:(c + 1) * MXU_WIDTH], preferred_element_type=F32)
        if c < N_QKV_CHUNKS:
            kind, grp = divmod(c, N_GROUPS)
            if kind == 0:
                z = z * (HEAD_DIM ** -0.5)
            d = DILATIONS[grp]
            o_ref = qkv_refs[c]
            if d == 1:
                o_ref[0, 0] = z.astype(BF16)
            else:
                base = 2 * strided_idx
                strided_idx += 1
                for s in range(2):
                    slab_ref[base + s] = z[:, s * LANES:(s + 1) * LANES]
                for r in range(d):
                    for s in range(2):
                        o_ref[0, r, :, s * LANES:(s + 1) * LANES] = (
                            slab_ref[base + s, pl.ds(r, tm // d, stride=d), :].astype(BF16))
        elif c < N_QKV_CHUNKS + N_U_CHUNKS:
            j = c - N_QKV_CHUNKS
            for s in range(2):
                _store_padded_chunks(u_ref, 2 * j + s, z[:, s * LANES:(s + 1) * LANES])
        else:
            j = c - N_QKV_CHUNKS - N_U_CHUNKS
            sl = slice(j * MXU_WIDTH, (j + 1) * MXU_WIDTH)
            gate_ref[:, sl] = jax.nn.sigmoid(z + gb_ref[:, sl]).astype(BF16)


def _in_proj(x, g, w_in, gate_bias, batch, seq):
    n_tok = x.shape[0]
    tm = PROJ_TILE
    tiles_per_seq = seq // tm
    out_shapes, out_specs = [], []
    for _ in range(3):
        for d in DILATIONS:
            out_shapes.append(jax.ShapeDtypeStruct((batch, d, seq // d, GROUP_WIDTH), BF16))
            out_specs.append(pl.BlockSpec(
                (1, d, tm // d, GROUP_WIDTH),
                lambda i: (i // tiles_per_seq, 0, i % tiles_per_seq, 0)))
    out_shapes.append(jax.ShapeDtypeStruct((N_OCTETS, _padded_rows(n_tok), LANES), F32))
    out_specs.append(pl.BlockSpec((N_OCTETS, _padded_rows(tm), LANES), lambda i: (0, i, 0)))
    out_shapes.append(jax.ShapeDtypeStruct((n_tok, 2 * D_MODEL), BF16))
    out_specs.append(pl.BlockSpec((tm, 2 * D_MODEL), lambda i: (i, 0)))
    outs = pl.pallas_call(
        _inproj_kernel,
        grid=(n_tok // tm,),
        in_specs=[pl.BlockSpec((tm, D_MODEL), lambda i: (i, 0)),
                  _const_spec((1, D_MODEL)),
                  _const_spec((D_MODEL, IN_WIDTH)),
                  _const_spec((1, 2 * D_MODEL))],
        out_specs=out_specs,
        out_shape=out_shapes,
        scratch_shapes=[pltpu.VMEM((2 * N_STRIDED, tm, LANES), F32)],
        compiler_params=pltpu.CompilerParams(
            dimension_semantics=("arbitrary",), vmem_limit_bytes=VMEM_LIMIT_BYTES),
        name="in_proj",
    )(x, g.reshape(1, D_MODEL), w_in, gate_bias.reshape(1, 2 * D_MODEL))
    q, k, v = outs[0:3], outs[3:6], outs[6:9]
    return q, k, v, outs[9], outs[10]


def _t5_bucket_np(dist):
    max_exact = N_BUCKETS // 2
    d = np.maximum(dist, 1).astype(np.float32)
    ratio = np.log(d / np.float32(max_exact)) / np.float32(math.log(MAX_DISTANCE / max_exact))
    large = max_exact + (ratio * np.float32(N_BUCKETS - max_exact)).astype(np.int32)
    large = np.minimum(large, N_BUCKETS - 1)
    return np.where(dist < max_exact, dist, large)


def _bucket_tables():
    qi = np.arange(BLOCK)[:, None]
    kj = np.arange(2 * BLOCK)[None, :]
    steps = qi + BLOCK - kj
    tables = []
    for window, d in zip(WINDOWS, DILATIONS):
        band = (steps >= 0) & (steps <= window // d)
        bucket = _t5_bucket_np(np.maximum(steps, 0) * d)
        tables.append(np.where(band, bucket, -1).astype(np.int32))
    return jnp.asarray(np.stack(tables))


def _attn_block(q_blk, k_blk, v_blk, bias):
    lane_head = lax.broadcasted_iota(jnp.int32, (BLOCK, GROUP_WIDTH), 1) // HEAD_DIM
    zero = jnp.zeros_like(q_blk)
    qs = jnp.concatenate(
        [jnp.where(lane_head == h, q_blk, zero) for h in range(HEADS_PER_GROUP)], axis=0)
    logits = lax.dot_general(qs, k_blk, (((1,), (1,)), ((), ())), preferred_element_type=F32) + bias
    m = jnp.max(logits, axis=-1, keepdims=True)
    p = jnp.exp(logits - m)
    l = jnp.sum(p, axis=-1, keepdims=True)
    pv = jnp.dot(p.astype(BF16), v_blk, preferred_element_type=F32)
    o = jnp.zeros((BLOCK, GROUP_WIDTH), F32)
    den = jnp.ones((BLOCK, GROUP_WIDTH), F32)
    mx = jnp.zeros((BLOCK, GROUP_WIDTH), F32)
    for h in range(HEADS_PER_GROUP):
        rows = slice(h * BLOCK, (h + 1) * BLOCK)
        sel = lane_head == h
        o = jnp.where(sel, pv[rows], o)
        den = jnp.where(sel, l[rows], den)
        mx = jnp.where(sel, m[rows], mx)
    return o / den, mx + jnp.log(den)


def _build_bias(tab_ref, bucket_ref, bias_scr):
    own_block = lax.broadcasted_iota(jnp.int32, (BLOCK, 2 * BLOCK), 1) >= BLOCK
    for grp in range(N_GROUPS):
        bucket = bucket_ref[grp]
        for h in range(HEADS_PER_GROUP):
            col = grp * HEADS_PER_GROUP + h

            def pick(b, acc, bucket=bucket, col=col):
                return jnp.where(bucket == b, tab_ref[b, col], acc)

            bias = lax.fori_loop(0, N_BUCKETS, pick, jnp.full((BLOCK, 2 * BLOCK), NEG_INF, F32))
            rows = slice(h * BLOCK, (h + 1) * BLOCK)
            bias_scr[grp, 0, rows, :] = bias
            bias_scr[grp, 1, rows, :] = jnp.where(own_block, bias, NEG_INF)


def _block_aligned(row):
    return row if isinstance(row, int) else pl.multiple_of(row, BLOCK)


def _attn_kernel(*refs):
    n = N_GROUPS
    tab_ref, bucket_ref = refs[0:2]
    refs = refs[2:]
    q_refs, kc_refs, kp_refs = refs[0:n], refs[n:2 * n], refs[2 * n:3 * n]
    vc_refs, vp_refs = refs[3 * n:4 * n], refs[4 * n:5 * n]
    o_ref, bias_scr, o_scr, l_scr = refs[5 * n:]
    tile = pl.program_id(1)

    @pl.when((pl.program_id(0) == 0) & (tile == 0))
    def _():
        _build_bias(tab_ref, bucket_ref, bias_scr)

    for grp, d in enumerate(DILATIONS):
        q_ref, kc_ref, kp_ref = q_refs[grp], kc_refs[grp], kp_refs[grp]
        vc_ref, vp_ref = vc_refs[grp], vp_refs[grp]
        blocks_per_residue = ATTN_TILE // d // BLOCK

        def run_block(r, blk, grp=grp, d=d, q_ref=q_ref, kc_ref=kc_ref, kp_ref=kp_ref,
                      vc_ref=vc_ref, vp_ref=vp_ref):
            if isinstance(blk, int) and blk == 0:
                k_blk = jnp.concatenate([kp_ref[0, r], kc_ref[0, r, 0:BLOCK, :]], axis=0)
                v_blk = jnp.concatenate([vp_ref[0, r], vc_ref[0, r, 0:BLOCK, :]], axis=0)
                bias = jnp.where(tile == 0, bias_scr[grp, 1], bias_scr[grp, 0])
                q_blk = q_ref[0, r, 0:BLOCK, :]
            else:
                kv_rows = pl.ds(_block_aligned((blk - 1) * BLOCK), 2 * BLOCK)
                k_blk = kc_ref[0, r, kv_rows, :]
                v_blk = vc_ref[0, r, kv_rows, :]
                bias = bias_scr[grp, 0]
                q_blk = q_ref[0, r, pl.ds(_block_aligned(blk * BLOCK), BLOCK), :]
            o, lse = _attn_block(q_blk, k_blk, v_blk, bias)
            start = blk * (BLOCK * d) + r
            for s in range(2):
                rows = pl.ds(start, BLOCK, stride=d) if d > 1 else pl.ds(start, BLOCK)
                o_scr[grp, s, rows, :] = o[:, s * LANES:(s + 1) * LANES]
                l_scr[grp, s, rows, :] = lse[:, s * LANES:(s + 1) * LANES]

        if blocks_per_residue == 1:
            def residue_set(i, c, run_block=run_block):
                for k in range(ATTN_INTERLEAVE):
                    run_block(ATTN_INTERLEAVE * i + k, 0)
                return c
            lax.fori_loop(0, d // ATTN_INTERLEAVE, residue_set, 0)
        elif d > 1:
            def residue(r, c, run_block=run_block, blocks_per_residue=blocks_per_residue):
                run_block(r, 0)
                for blk in range(1, blocks_per_residue):
                    run_block(r, blk)
                return c
            lax.fori_loop(0, d, residue, 0)
        else:
            for blk in range(ATTN_INTERLEAVE):
                run_block(0, blk)

            def block_set(i, c, run_block=run_block):
                for k in range(ATTN_INTERLEAVE):
                    run_block(0, ATTN_INTERLEAVE * i + k)
                return c
            lax.fori_loop(1, blocks_per_residue // ATTN_INTERLEAVE, block_set, 0)

    merge_rows = 256

    def merge(c, carry):
        rows = pl.ds(pl.multiple_of(c * merge_rows, merge_rows), merge_rows)
        for s in range(2):
            lses = [l_scr[grp, s, rows, :] for grp in range(N_GROUPS)]
            mx = functools.reduce(jnp.maximum, lses)
            es = [jnp.exp(x - mx) for x in lses]
            num = sum(e * o_scr[grp, s, rows, :] for grp, e in enumerate(es))
            o_ref[rows, s * LANES:(s + 1) * LANES] = (num / sum(es)).astype(BF16)
        return carry

    lax.fori_loop(0, ATTN_TILE // merge_rows, merge, 0)


def _attention(q, k, v, rel_bias_table, batch, seq):
    tiles = seq // ATTN_TILE

    def cur_spec(d):
        return pl.BlockSpec((1, d, ATTN_TILE // d, GROUP_WIDTH), lambda b, j: (b, 0, j, 0))

    def prev_spec(d):
        per_tile = ATTN_TILE // d // BLOCK
        return pl.BlockSpec((1, d, BLOCK, GROUP_WIDTH),
                            lambda b, j: (b, 0, jnp.maximum(j * per_tile - 1, 0), 0))

    in_specs = ([pl.BlockSpec(memory_space=pltpu.SMEM),
                 _const_spec((N_GROUPS, BLOCK, 2 * BLOCK))]
                + [cur_spec(d) for d in DILATIONS] + [cur_spec(d) for d in DILATIONS]
                + [prev_spec(d) for d in DILATIONS] + [cur_spec(d) for d in DILATIONS]
                + [prev_spec(d) for d in DILATIONS])
    return pl.pallas_call(
        _attn_kernel,
        grid=(batch, tiles),
        in_specs=in_specs,
        out_specs=pl.BlockSpec((ATTN_TILE, GROUP_WIDTH), lambda b, j: (b * tiles + j, 0)),
        out_shape=jax.ShapeDtypeStruct((batch * seq, GROUP_WIDTH), BF16),
        scratch_shapes=[pltpu.VMEM((N_GROUPS, 2, HEADS_PER_GROUP * BLOCK, 2 * BLOCK), F32),
                        pltpu.VMEM((N_GROUPS, 2, ATTN_TILE, LANES), F32),
                        pltpu.VMEM((N_GROUPS, 2, ATTN_TILE, LANES), F32)],
        compiler_params=pltpu.CompilerParams(
            dimension_semantics=("arbitrary", "arbitrary"), vmem_limit_bytes=VMEM_LIMIT_BYTES),
        name="dilated_attn",
    )(rel_bias_table.astype(F32), _bucket_tables(), *q, *k, *k, *v, *v)


def _octet_mask(row_group, col_group, rows, cols):
    r = np.arange(rows)[:, None] // row_group
    c = np.arange(cols)[None, :] // col_group
    return jnp.asarray(r == c)


def _s5_matrices(a_re, a_im, log_dt, b_re, b_im, c_re, c_im, d_skip):
    hi = lax.Precision.HIGHEST
    tc = SSM_CHUNK
    lam_re = a_re.astype(F32)
    lam_im = a_im.astype(F32)
    dt = jnp.exp(log_dt.astype(F32))[:, None]
    mag = jnp.exp(lam_re * dt)
    ab_re = mag * jnp.cos(lam_im * dt)
    ab_im = mag * jnp.sin(lam_im * dt)
    den = lam_re * lam_re + lam_im * lam_im
    xr = ab_re - 1.0
    coef_re = (xr * lam_re + ab_im * lam_im) / den
    coef_im = (ab_im * lam_re - xr * lam_im) / den
    br = b_re.astype(F32)
    bi = b_im.astype(F32)
    bb_re = coef_re[..., None] * br - coef_im[..., None] * bi
    bb_im = coef_re[..., None] * bi + coef_im[..., None] * br
    cr = c_re.astype(F32)
    ci = c_im.astype(F32)

    def a_pow(k):
        kk = k.astype(F32)[:, None, None]
        pm = jnp.exp(lam_re * dt * kk)
        return pm * jnp.cos(lam_im * dt * kk), pm * jnp.sin(lam_im * dt * kk)

    pw_re, pw_im = a_pow(jnp.arange(tc + 1))
    pr, pi = pw_re[:tc], pw_im[:tc]
    abr = pr[..., None] * bb_re - pi[..., None] * bb_im
    abi = pr[..., None] * bb_im + pi[..., None] * bb_re
    kern = (jnp.einsum('gon,lgnc->lgoc', cr, abr, precision=hi)
            - jnp.einsum('gon,lgnc->lgoc', ci, abi, precision=hi))
    skip = d_skip.astype(F32).reshape(SSM_GROUPS, SSM_GROUP)
    kern = kern.at[0].add(skip[:, :, None] * jnp.eye(SSM_GROUP, dtype=F32))

    kt = jnp.transpose(kern, (0, 1, 3, 2)).reshape(tc, N_OCTETS, LANES, SSM_GROUP)
    kt = jnp.transpose(kt, (1, 0, 2, 3))
    kblk = jnp.where(_octet_mask(SSM_GROUP, SSM_GROUP, LANES, LANES),
                     jnp.tile(kt, (1, 1, 1, SSM_OCTET)), 0.0)

    def by_octet(x):
        return x.reshape(x.shape[0], N_OCTETS, 1, SSM_OCTET, SSM_STATE)

    def channel_major(x):
        x = x.reshape(N_OCTETS, SSM_OCTET, SSM_GROUP, SSM_STATE)
        return jnp.transpose(x, (0, 2, 1, 3))[None]

    def compact(re, im):
        both = jnp.stack([re, im], axis=1)
        return both.reshape(tc, 2, N_OCTETS, SSM_GROUP, OCTET_STATE).astype(BF16)

    lr, li = by_octet(pr[::-1]), by_octet(pi[::-1])
    bt_re = channel_major(jnp.transpose(bb_re, (0, 2, 1)))
    bt_im = channel_major(jnp.transpose(bb_im, (0, 2, 1)))
    p_src = compact(lr * bt_re - li * bt_im, lr * bt_im + li * bt_re)

    qr, qi = by_octet(pw_re[1:]), by_octet(pw_im[1:])
    ct_re, ct_im = channel_major(cr), channel_major(ci)
    q_src = compact(ct_re * qr - ct_im * qi, -(ct_re * qi + ct_im * qr))

    a_chunk = jnp.concatenate([pw_re[tc].reshape(N_OCTETS, 1, OCTET_STATE),
                               pw_im[tc].reshape(N_OCTETS, 1, OCTET_STATE)], axis=-1)
    return kblk.astype(BF16), p_src, q_src, a_chunk


def _s5_kernel(u_ref, kblk_ref, p_ref, q_ref, a_ref, o_ref, k_scr, p_scr, qt_scr, x_scr, s_scr, z_scr):
    n_chunks = x_scr.shape[0]
    tc = SSM_CHUNK

    @pl.when(pl.program_id(1) == 0)
    def _():
        k_scr[...] = jnp.zeros(k_scr.shape, BF16)
        for s in range(tc):
            for t in range(s, tc):
                k_scr[s * LANES:(s + 1) * LANES, t * LANES:(t + 1) * LANES] = kblk_ref[0, t - s]
        row_group = lax.broadcasted_iota(jnp.int32, (LANES, OCTET_STATE), 0) // SSM_GROUP
        lane_group = lax.broadcasted_iota(jnp.int32, (LANES, OCTET_STATE), 1) // SSM_STATE
        same_group = row_group == lane_group
        zeros = jnp.zeros((LANES, OCTET_STATE), BF16)
        for src_ref, dst in ((p_ref, p_scr), (q_ref, qt_scr)):
            for t in range(tc):
                for part in range(2):
                    src = src_ref[t, part, 0]
                    tiled = jnp.concatenate([src] * SSM_OCTET, axis=0)
                    dst[t * LANES:(t + 1) * LANES, part * OCTET_STATE:(part + 1) * OCTET_STATE] = (
                        jnp.where(same_group, tiled, zeros))

    inc = None
    for pair in range(tc // 2):
        for t in (2 * pair, 2 * pair + 1):
            x_scr[:, t * LANES:(t + 1) * LANES] = (
                u_ref[0, pl.ds(t, n_chunks, stride=CHUNK_PITCH), :].astype(BF16))
        rows = slice(pair * MXU_WIDTH, (pair + 1) * MXU_WIDTH)
        part = jnp.dot(x_scr[:, rows], p_scr[rows, :], preferred_element_type=F32)
        inc = part if inc is None else inc + part
    s_scr[...] = inc

    a_re = a_ref[0, :, 0:OCTET_STATE]
    a_im = a_ref[0, :, OCTET_STATE:2 * OCTET_STATE]

    def step(j, carry):
        c_re, c_im = carry
        row = pl.ds(j, 1)
        z_scr[row, 0:OCTET_STATE] = c_re
        z_scr[row, OCTET_STATE:2 * OCTET_STATE] = c_im
        inc_re = s_scr[row, 0:OCTET_STATE]
        inc_im = s_scr[row, OCTET_STATE:2 * OCTET_STATE]
        return (a_re * c_re - a_im * c_im + inc_re, a_re * c_im + a_im * c_re + inc_im)

    zero = jnp.zeros((1, OCTET_STATE), F32)
    lax.fori_loop(0, n_chunks, step, (zero, zero))

    zb = z_scr[...].astype(BF16)
    for pair in range(tc // 2):
        cols = slice(pair * MXU_WIDTH, (pair + 1) * MXU_WIDTH)
        live = (2 * pair + 2) * LANES
        y = (jnp.dot(x_scr[:, 0:live], k_scr[0:live, cols], preferred_element_type=F32)
             + lax.dot_general(zb, qt_scr[cols, :], (((1,), (1,)), ((), ())),
                               preferred_element_type=F32))
        y = jax.nn.gelu(y)
        o_ref[0, pl.ds(2 * pair, n_chunks, stride=CHUNK_PITCH), :] = y[:, 0:LANES]
        o_ref[0, pl.ds(2 * pair + 1, n_chunks, stride=CHUNK_PITCH), :] = y[:, LANES:2 * LANES]
    for t in range(tc, CHUNK_PITCH):
        o_ref[0, pl.ds(t, n_chunks, stride=CHUNK_PITCH), :] = jnp.zeros((n_chunks, LANES), F32)


def _s5(u, mats, batch, seq):
    kblk, p_src, q_src, a_chunk = mats
    n_chunks = seq // SSM_CHUNK
    width = SSM_CHUNK * LANES

    def per_octet(shape):
        return pl.BlockSpec((1,) + shape, lambda o, b: (o,) + (0,) * len(shape),
                            pipeline_mode=pl.Buffered(1))

    compact_spec = pl.BlockSpec((SSM_CHUNK, 2, 1, SSM_GROUP, OCTET_STATE),
                                lambda o, b: (0, 0, o, 0, 0))
    return pl.pallas_call(
        _s5_kernel,
        grid=(N_OCTETS, batch),
        in_specs=[pl.BlockSpec((1, _padded_rows(seq), LANES), lambda o, b: (o, b, 0)),
                  per_octet((SSM_CHUNK, LANES, LANES)),
                  compact_spec,
                  compact_spec,
                  per_octet((1, 2 * OCTET_STATE))],
        out_specs=pl.BlockSpec((1, _padded_rows(seq), LANES), lambda o, b: (o, b, 0)),
        out_shape=jax.ShapeDtypeStruct((N_OCTETS, _padded_rows(batch * seq), LANES), F32),
        scratch_shapes=[pltpu.VMEM((width, width), BF16),
                        pltpu.VMEM((width, 2 * OCTET_STATE), BF16),
                        pltpu.VMEM((width, 2 * OCTET_STATE), BF16),
                        pltpu.VMEM((n_chunks, width), BF16),
                        pltpu.VMEM((n_chunks, 2 * OCTET_STATE), F32),
                        pltpu.VMEM((n_chunks, 2 * OCTET_STATE), F32)],
        compiler_params=pltpu.CompilerParams(
            dimension_semantics=("arbitrary", "arbitrary"), vmem_limit_bytes=VMEM_LIMIT_BYTES),
        name="s5_mixer",
    )(u, kblk, p_src, q_src, a_chunk)


def _merge_ffn_kernel(y_ref, oa_ref, gate_ref, x_ref, wglu_ref, wssm_ref, wattn_ref, wout_ref,
                      g_ref, wg_ref, wu_ref, wd_ref, fn_ref, o_ref, *, final):
    y_gelu = jnp.concatenate(
        [_load_padded_chunks(y_ref, o, x_ref.shape[0]).astype(BF16) for o in range(N_OCTETS)], axis=1)
    glu = jnp.dot(y_gelu, wglu_ref[...], preferred_element_type=F32)
    ys = (glu[:, 0:SSM_WIDTH] * jax.nn.sigmoid(glu[:, SSM_WIDTH:2 * SSM_WIDTH])).astype(BF16)
    y_ssm = jnp.dot(ys, wssm_ref[...], preferred_element_type=F32)
    y_attn = jnp.dot(oa_ref[...], wattn_ref[...], preferred_element_type=F32)
    g_attn = gate_ref[:, 0:D_MODEL].astype(F32)
    g_ssm = gate_ref[:, D_MODEL:2 * D_MODEL].astype(F32)
    mixed = (g_attn * y_attn + g_ssm * y_ssm).astype(BF16)
    x = x_ref[...] + jnp.dot(mixed, wout_ref[...], preferred_element_type=F32)
    y = _swiglu_residual(x, g_ref[...], wg_ref, wu_ref, wd_ref)
    o_ref[...] = _rms(y, fn_ref[...]) if final else y


def _merge_ffn(y_gelu, o_attn, gates, x, w_glu, w_ssm, w_attn, w_out, g, wg, wu, wd, final_g, final):
    n_tok = x.shape[0]
    tm = MERGE_TILE

    def rows(width):
        return pl.BlockSpec((tm, width), lambda i: (i, 0))

    return pl.pallas_call(
        functools.partial(_merge_ffn_kernel, final=final),
        grid=(n_tok // tm,),
        in_specs=[pl.BlockSpec((N_OCTETS, _padded_rows(tm), LANES), lambda i: (0, i, 0)),
                  rows(GROUP_WIDTH), rows(2 * D_MODEL), rows(D_MODEL),
                  _const_spec((SSM_WIDTH, 2 * SSM_WIDTH)), _const_spec((SSM_WIDTH, D_MODEL)),
                  _const_spec((GROUP_WIDTH, D_MODEL)), _const_spec((D_MODEL, D_MODEL))]
                 + _ffn_weight_specs() + [_const_spec((1, D_MODEL))],
        out_specs=rows(D_MODEL),
        out_shape=jax.ShapeDtypeStruct((n_tok, D_MODEL), F32),
        compiler_params=pltpu.CompilerParams(
            dimension_semantics=("arbitrary",), vmem_limit_bytes=VMEM_LIMIT_BYTES),
        name="merge_ffn",
    )(y_gelu, o_attn, gates, x, w_glu, w_ssm, w_attn, w_out,
      g.reshape(1, D_MODEL), wg, wu, wd, final_g.reshape(1, D_MODEL))


def kernel(x, ffn1_norm, ffn1_w_gate, ffn1_w_up, ffn1_w_down, mix_norm, w_in, gate_bias,
           rel_bias_table, ssm_a_re, ssm_a_im, ssm_log_dt, ssm_b_re, ssm_b_im, ssm_c_re,
           ssm_c_im, ssm_d, ssm_w_glu, w_attn_branch, w_ssm_branch, w_out, ffn2_norm,
           ffn2_w_gate, ffn2_w_up, ffn2_w_down, final_norm):
    batch, seq, _ = x.shape
    depth = ffn1_norm.shape[0]
    h = x.reshape(batch * seq, D_MODEL)
    for l in range(depth):
        h = _ffn(h, ffn1_norm[l], ffn1_w_gate[l].astype(BF16), ffn1_w_up[l].astype(BF16),
                 ffn1_w_down[l].astype(BF16))
        q, k, v, u, gates = _in_proj(h, mix_norm[l], w_in[l].astype(BF16), gate_bias[l], batch, seq)
        o_attn = _attention(q, k, v, rel_bias_table, batch, seq)
        mats = _s5_matrices(ssm_a_re[l], ssm_a_im[l], ssm_log_dt[l], ssm_b_re[l], ssm_b_im[l],
                            ssm_c_re[l], ssm_c_im[l], ssm_d[l])
        y_gelu = _s5(u, mats, batch, seq)
        h = _merge_ffn(y_gelu, o_attn, gates, h, ssm_w_glu[l].astype(BF16),
                       w_ssm_branch[l].astype(BF16), w_attn_branch[l].astype(BF16),
                       w_out[l].astype(BF16), ffn2_norm[l], ffn2_w_gate[l].astype(BF16),
                       ffn2_w_up[l].astype(BF16), ffn2_w_down[l].astype(BF16), final_norm,
                       final=(l == depth - 1))
    return h.reshape(batch, seq, D_MODEL)
```

```python
import functools
import math

import jax
import jax.numpy as jnp
import numpy as np
from jax import lax
from jax.experimental import pallas as pl
from jax.experimental.pallas import tpu as pltpu

F32 = jnp.float32
BF16 = jnp.bfloat16

D_MODEL = 1024
D_FF = 2816
EPS = 1e-6
HEAD_DIM = 64
HEADS_PER_GROUP = 4
GROUP_WIDTH = HEADS_PER_GROUP * HEAD_DIM
DILATIONS = (1, 4, 16)
WINDOWS = (128, 512, 2048)
N_GROUPS = len(DILATIONS)
ATTN_WIDTH = N_GROUPS * GROUP_WIDTH
BLOCK = 128
N_BUCKETS = 32
MAX_DISTANCE = 2048
NEG_INF = -1e30
SSM_GROUP = 16
SSM_WIDTH = 512
SSM_GROUPS = SSM_WIDTH // SSM_GROUP
SSM_STATE = 64
IN_WIDTH = 3 * ATTN_WIDTH + SSM_WIDTH + 2 * D_MODEL

LANES = 128
MXU_WIDTH = 256
VMEM_LIMIT_BYTES = 56 * 1024 * 1024

FFN_TILE = 1024
PROJ_TILE = 1024
MERGE_TILE = 512
ATTN_TILE = BLOCK * DILATIONS[-1]
ATTN_INTERLEAVE = 4
SCATTER_STRIDE = 4
SSM_CHUNK = 16
CHUNK_PITCH = SSM_CHUNK + 4
SSM_OCTET = LANES // SSM_GROUP
N_OCTETS = SSM_GROUPS // SSM_OCTET
OCTET_STATE = SSM_OCTET * SSM_STATE


def _rms(x, g):
    return x * lax.rsqrt(jnp.mean(x * x, axis=-1, keepdims=True) + EPS) * g


def _const_spec(shape):
    return pl.BlockSpec(shape, lambda *_: (0,) * len(shape), pipeline_mode=pl.Buffered(1))


def _swiglu_residual(x, g, wg_ref, wu_ref, wd_ref):
    h = _rms(x, g).astype(BF16)
    acc = jnp.zeros(x.shape, F32)
    for c in range(D_FF // MXU_WIDTH):
        sl = slice(c * MXU_WIDTH, (c + 1) * MXU_WIDTH)
        a = jnp.dot(h, wg_ref[:, sl], preferred_element_type=F32)
        b = jnp.dot(h, wu_ref[:, sl], preferred_element_type=F32)
        t = (a * jax.nn.sigmoid(a) * b).astype(BF16)
        acc = acc + jnp.dot(t, wd_ref[sl, :], preferred_element_type=F32)
    return x + 0.5 * acc


def _ffn_kernel(x_ref, g_ref, wg_ref, wu_ref, wd_ref, o_ref):
    o_ref[...] = _swiglu_residual(x_ref[...], g_ref[...], wg_ref, wu_ref, wd_ref)


def _ffn_weight_specs():
    return [_const_spec((1, D_MODEL)), _const_spec((D_MODEL, D_FF)),
            _const_spec((D_MODEL, D_FF)), _const_spec((D_FF, D_MODEL))]


def _ffn(x, g, wg, wu, wd):
    n_tok = x.shape[0]
    tm = FFN_TILE
    tile = pl.BlockSpec((tm, D_MODEL), lambda i: (i, 0))
    return pl.pallas_call(
        _ffn_kernel,
        grid=(n_tok // tm,),
        in_specs=[tile] + _ffn_weight_specs(),
        out_specs=tile,
        out_shape=jax.ShapeDtypeStruct((n_tok, D_MODEL), F32),
        compiler_params=pltpu.CompilerParams(
            dimension_semantics=("arbitrary",), vmem_limit_bytes=VMEM_LIMIT_BYTES),
        name="ffn",
    )(x, g.reshape(1, D_MODEL), wg, wu, wd)


N_QKV_CHUNKS = 3 * N_GROUPS
N_U_CHUNKS = SSM_WIDTH // MXU_WIDTH
N_GATE_CHUNKS = 2 * D_MODEL // MXU_WIDTH
N_STRIDED = 3 * (N_GROUPS - 1)
N_TWO_HOP = 3 * sum(d > SCATTER_STRIDE for d in DILATIONS)


def _padded_rows(n_tokens):
    return n_tokens // SSM_CHUNK * CHUNK_PITCH


def _store_padded_chunks(ref, slab, rows):
    pad = jnp.zeros((CHUNK_PITCH - SSM_CHUNK, rows.shape[1]), rows.dtype)
    for ch in range(rows.shape[0] // SSM_CHUNK):
        base = ch * CHUNK_PITCH
        ref[slab, base:base + SSM_CHUNK, :] = rows[ch * SSM_CHUNK:(ch + 1) * SSM_CHUNK, :]
        ref[slab, base + SSM_CHUNK:base + CHUNK_PITCH, :] = pad


def _load_padded_chunks(ref, slab, n_tokens):
    return jnp.concatenate(
        [ref[slab, ch * CHUNK_PITCH:ch * CHUNK_PITCH + SSM_CHUNK, :]
         for ch in range(n_tokens // SSM_CHUNK)], axis=0)


def _inproj_kernel(x_ref, g_ref, w_ref, gb_ref, *refs):
    qkv_refs = refs[:N_QKV_CHUNKS]
    u_ref, gate_ref, slab_ref, hop_ref = refs[N_QKV_CHUNKS:]
    tm = x_ref.shape[0]
    h = _rms(x_ref[...], g_ref[...]).astype(BF16)
    strided_idx = 0
    hop_idx = 0
    heavy = list(range(N_QKV_CHUNKS + N_U_CHUNKS))
    light = list(range(N_QKV_CHUNKS + N_U_CHUNKS, IN_WIDTH // MXU_WIDTH))
    order = []
    while heavy or light:
        if heavy:
            order.append(heavy.pop(0))
        if light:
            order.append(light.pop(0))
    for c in order:
        z = jnp.dot(h, w_ref[:, c * MXU_WIDTH:(c + 1) * MXU_WIDTH], preferred_element_type=F32)
        if c < N_QKV_CHUNKS:
            kind, grp = divmod(c, N_GROUPS)
            if kind == 0:
                z = z * (HEAD_DIM ** -0.5)
            d = DILATIONS[grp]
            o_ref = qkv_refs[c]
            if d == 1:
                o_ref[0, 0] = z.astype(BF16)
            else:
                base = 2 * strided_idx
                strided_idx += 1
                for s in range(2):
                    slab_ref[base + s] = z[:, s * LANES:(s + 1) * LANES]
                lanes = [slice(s * LANES, (s + 1) * LANES) for s in range(2)]
                if d <= SCATTER_STRIDE:
                    for r in range(d):
                        for s in range(2):
                            o_ref[0, r, :, lanes[s]] = (
                                slab_ref[base + s, pl.ds(r, tm // d, stride=d), :].astype(BF16))
                else:
                    hop = tm // SCATTER_STRIDE
                    hop_base = 2 * hop_idx
                    hop_idx += 1
                    for s in range(2):
                        for low in range(SCATTER_STRIDE):
                            hop_ref[hop_base + s, pl.ds(low * hop, hop), :] = (
                                slab_ref[base + s, pl.ds(low, hop, stride=SCATTER_STRIDE), :])
                    inner = d // SCATTER_STRIDE
                    for r in range(d):
                        low, high = r % SCATTER_STRIDE, r // SCATTER_STRIDE
                        for s in range(2):
                            o_ref[0, r, :, lanes[s]] = hop_ref[
                                hop_base + s, pl.ds(low * hop + high, tm // d, stride=inner), :
                            ].astype(BF16)
        elif c < N_QKV_CHUNKS + N_U_CHUNKS:
            j = c - N_QKV_CHUNKS
            for s in range(2):
                _store_padded_chunks(u_ref, 2 * j + s, z[:, s * LANES:(s + 1) * LANES])
        else:
            j = c - N_QKV_CHUNKS - N_U_CHUNKS
            sl = slice(j * MXU_WIDTH, (j + 1) * MXU_WIDTH)
            gate_ref[:, sl] = jax.nn.sigmoid(z + gb_ref[:, sl]).astype(BF16)


def _in_proj(x, g, w_in, gate_bias, batch, seq):
    n_tok = x.shape[0]
    tm = PROJ_TILE
    tiles_per_seq = seq // tm
    out_shapes, out_specs = [], []
    for _ in range(3):
        for d in DILATIONS:
            out_shapes.append(jax.ShapeDtypeStruct((batch, d, seq // d, GROUP_WIDTH), BF16))
            out_specs.append(pl.BlockSpec(
                (1, d, tm // d, GROUP_WIDTH),
                lambda i: (i // tiles_per_seq, 0, i % tiles_per_seq, 0)))
    out_shapes.append(jax.ShapeDtypeStruct((N_OCTETS, _padded_rows(n_tok), LANES), F32))
    out_specs.append(pl.BlockSpec((N_OCTETS, _padded_rows(tm), LANES), lambda i: (0, i, 0)))
    out_shapes.append(jax.ShapeDtypeStruct((n_tok, 2 * D_MODEL), BF16))
    out_specs.append(pl.BlockSpec((tm, 2 * D_MODEL), lambda i: (i, 0)))
    outs = pl.pallas_call(
        _inproj_kernel,
        grid=(n_tok // tm,),
        in_specs=[pl.BlockSpec((tm, D_MODEL), lambda i: (i, 0)),
                  _const_spec((1, D_MODEL)),
                  _const_spec((D_MODEL, IN_WIDTH)),
                  _const_spec((1, 2 * D_MODEL))],
        out_specs=out_specs,
        out_shape=out_shapes,
        scratch_shapes=[pltpu.VMEM((2 * N_STRIDED, tm, LANES), F32),
                        pltpu.VMEM((2 * N_TWO_HOP, tm, LANES), F32)],
        compiler_params=pltpu.CompilerParams(
            dimension_semantics=("arbitrary",), vmem_limit_bytes=VMEM_LIMIT_BYTES),
        name="in_proj",
    )(x, g.reshape(1, D_MODEL), w_in, gate_bias.reshape(1, 2 * D_MODEL))
    q, k, v = outs[0:3], outs[3:6], outs[6:9]
    return q, k, v, outs[9], outs[10]


def _t5_bucket_np(dist):
    max_exact = N_BUCKETS // 2
    d = np.maximum(dist, 1).astype(np.float32)
    ratio = np.log(d / np.float32(max_exact)) / np.float32(math.log(MAX_DISTANCE / max_exact))
    large = max_exact + (ratio * np.float32(N_BUCKETS - max_exact)).astype(np.int32)
    large = np.minimum(large, N_BUCKETS - 1)
    return np.where(dist < max_exact, dist, large)


def _bucket_tables():
    qi = np.arange(BLOCK)[:, None]
    kj = np.arange(2 * BLOCK)[None, :]
    steps = qi + BLOCK - kj
    tables = []
    for window, d in zip(WINDOWS, DILATIONS):
        band = (steps >= 0) & (steps <= window // d)
        bucket = _t5_bucket_np(np.maximum(steps, 0) * d)
        tables.append(np.where(band, bucket, -1).astype(np.int32))
    return jnp.asarray(np.stack(tables))


def _attn_block(q_blk, k_blk, v_blk, bias):
    lane_head = lax.broadcasted_iota(jnp.int32, (BLOCK, GROUP_WIDTH), 1) // HEAD_DIM
    zero = jnp.zeros_like(q_blk)
    qs = jnp.concatenate(
        [jnp.where(lane_head == h, q_blk, zero) for h in range(HEADS_PER_GROUP)], axis=0)
    logits = lax.dot_general(qs, k_blk, (((1,), (1,)), ((), ())), preferred_element_type=F32) + bias
    m = jnp.max(logits, axis=-1, keepdims=True)
    p = jnp.exp(logits - m)
    l = jnp.sum(p, axis=-1, keepdims=True)
    pv = jnp.dot(p.astype(BF16), v_blk, preferred_element_type=F32)
    o = jnp.zeros((BLOCK, GROUP_WIDTH), F32)
    den = jnp.ones((BLOCK, GROUP_WIDTH), F32)
    mx = jnp.zeros((BLOCK, GROUP_WIDTH), F32)
    for h in range(HEADS_PER_GROUP):
        rows = slice(h * BLOCK, (h + 1) * BLOCK)
        sel = lane_head == h
        o = jnp.where(sel, pv[rows], o)
        den = jnp.where(sel, l[rows], den)
        mx = jnp.where(sel, m[rows], mx)
    return o / den, mx + jnp.log(den)


def _build_bias(tab_ref, bucket_ref, bias_scr):
    own_block = lax.broadcasted_iota(jnp.int32, (BLOCK, 2 * BLOCK), 1) >= BLOCK
    for grp in range(N_GROUPS):
        bucket = bucket_ref[grp]
        for h in range(HEADS_PER_GROUP):
            col = grp * HEADS_PER_GROUP + h

            def pick(b, acc, bucket=bucket, col=col):
                return jnp.where(bucket == b, tab_ref[b, col], acc)

            bias = lax.fori_loop(0, N_BUCKETS, pick, jnp.full((BLOCK, 2 * BLOCK), NEG_INF, F32))
            rows = slice(h * BLOCK, (h + 1) * BLOCK)
            bias_scr[grp, 0, rows, :] = bias
            bias_scr[grp, 1, rows, :] = jnp.where(own_block, bias, NEG_INF)


def _block_aligned(row):
    return row if isinstance(row, int) else pl.multiple_of(row, BLOCK)


def _attn_kernel(*refs):
    n = N_GROUPS
    tab_ref, bucket_ref = refs[0:2]
    refs = refs[2:]
    q_refs, kc_refs, kp_refs = refs[0:n], refs[n:2 * n], refs[2 * n:3 * n]
    vc_refs, vp_refs = refs[3 * n:4 * n], refs[4 * n:5 * n]
    o_ref, bias_scr, o_scr, l_scr, t_scr = refs[5 * n:]
    tile = pl.program_id(1)

    @pl.when((pl.program_id(0) == 0) & (tile == 0))
    def _():
        _build_bias(tab_ref, bucket_ref, bias_scr)

    for grp, d in enumerate(DILATIONS):
        q_ref, kc_ref, kp_ref = q_refs[grp], kc_refs[grp], kp_refs[grp]
        vc_ref, vp_ref = vc_refs[grp], vp_refs[grp]
        blocks_per_residue = ATTN_TILE // d // BLOCK

        def run_block(r, blk, grp=grp, d=d, q_ref=q_ref, kc_ref=kc_ref, kp_ref=kp_ref,
                      vc_ref=vc_ref, vp_ref=vp_ref):
            if isinstance(blk, int) and blk == 0:
                k_blk = jnp.concatenate([kp_ref[0, r], kc_ref[0, r, 0:BLOCK, :]], axis=0)
                v_blk = jnp.concatenate([vp_ref[0, r], vc_ref[0, r, 0:BLOCK, :]], axis=0)
                bias = jnp.where(tile == 0, bias_scr[grp, 1], bias_scr[grp, 0])
                q_blk = q_ref[0, r, 0:BLOCK, :]
            else:
                kv_rows = pl.ds(_block_aligned((blk - 1) * BLOCK), 2 * BLOCK)
                k_blk = kc_ref[0, r, kv_rows, :]
                v_blk = vc_ref[0, r, kv_rows, :]
                bias = bias_scr[grp, 0]
                q_blk = q_ref[0, r, pl.ds(_block_aligned(blk * BLOCK), BLOCK), :]
            o, lse = _attn_block(q_blk, k_blk, v_blk, bias)
            if d <= SCATTER_STRIDE:
                start = blk * (BLOCK * d) + r
                rows = pl.ds(start, BLOCK, stride=d) if d > 1 else pl.ds(start, BLOCK)
                for s in range(2):
                    o_scr[grp, s, rows, :] = o[:, s * LANES:(s + 1) * LANES]
                    l_scr[grp, s, rows, :] = lse[:, s * LANES:(s + 1) * LANES]
            else:
                inner = d // SCATTER_STRIDE
                low, high = lax.rem(r, SCATTER_STRIDE), r // SCATTER_STRIDE
                rows = pl.ds(blk * (BLOCK * inner) + high, BLOCK, stride=inner)
                for s in range(2):
                    t_scr[0, s, low, rows, :] = o[:, s * LANES:(s + 1) * LANES]
                    t_scr[1, s, low, rows, :] = lse[:, s * LANES:(s + 1) * LANES]

        if blocks_per_residue == 1:
            def residue_set(i, c, run_block=run_block):
                for k in range(ATTN_INTERLEAVE):
                    run_block(ATTN_INTERLEAVE * i + k, 0)
                return c
            lax.fori_loop(0, d // ATTN_INTERLEAVE, residue_set, 0)
        elif d > 1:
            def residue(r, c, run_block=run_block, blocks_per_residue=blocks_per_residue):
                run_block(r, 0)
                for blk in range(1, blocks_per_residue):
                    run_block(r, blk)
                return c
            lax.fori_loop(0, d, residue, 0)
        else:
            for blk in range(ATTN_INTERLEAVE):
                run_block(0, blk)

            def block_set(i, c, run_block=run_block):
                for k in range(ATTN_INTERLEAVE):
                    run_block(0, ATTN_INTERLEAVE * i + k)
                return c
            lax.fori_loop(1, blocks_per_residue // ATTN_INTERLEAVE, block_set, 0)

        if d > SCATTER_STRIDE:
            def unstage(low, c, grp=grp):
                rows = pl.ds(low, ATTN_TILE // SCATTER_STRIDE, stride=SCATTER_STRIDE)
                for s in range(2):
                    o_scr[grp, s, rows, :] = t_scr[0, s, low]
                    l_scr[grp, s, rows, :] = t_scr[1, s, low]
                return c
            lax.fori_loop(0, SCATTER_STRIDE, unstage, 0)

    merge_rows = 256

    def merge(c, carry):
        rows = pl.ds(pl.multiple_of(c * merge_rows, merge_rows), merge_rows)
        for s in range(2):
            lses = [l_scr[grp, s, rows, :] for grp in range(N_GROUPS)]
            mx = functools.reduce(jnp.maximum, lses)
            es = [jnp.exp(x - mx) for x in lses]
            num = sum(e * o_scr[grp, s, rows, :] for grp, e in enumerate(es))
            o_ref[rows, s * LANES:(s + 1) * LANES] = (num / sum(es)).astype(BF16)
        return carry

    lax.fori_loop(0, ATTN_TILE // merge_rows, merge, 0)


def _attention(q, k, v, rel_bias_table, batch, seq):
    tiles = seq // ATTN_TILE

    def cur_spec(d):
        return pl.BlockSpec((1, d, ATTN_TILE // d, GROUP_WIDTH), lambda b, j: (b, 0, j, 0))

    def prev_spec(d):
        per_tile = ATTN_TILE // d // BLOCK
        return pl.BlockSpec((1, d, BLOCK, GROUP_WIDTH),
                            lambda b, j: (b, 0, jnp.maximum(j * per_tile - 1, 0), 0))

    in_specs = ([pl.BlockSpec(memory_space=pltpu.SMEM),
                 _const_spec((N_GROUPS, BLOCK, 2 * BLOCK))]
                + [cur_spec(d) for d in DILATIONS] + [cur_spec(d) for d in DILATIONS]
                + [prev_spec(d) for d in DILATIONS] + [cur_spec(d) for d in DILATIONS]
                + [prev_spec(d) for d in DILATIONS])
    return pl.pallas_call(
        _attn_kernel,
        grid=(batch, tiles),
        in_specs=in_specs,
        out_specs=pl.BlockSpec((ATTN_TILE, GROUP_WIDTH), lambda b, j: (b * tiles + j, 0)),
        out_shape=jax.ShapeDtypeStruct((batch * seq, GROUP_WIDTH), BF16),
        scratch_shapes=[pltpu.VMEM((N_GROUPS, 2, HEADS_PER_GROUP * BLOCK, 2 * BLOCK), F32),
                        pltpu.VMEM((N_GROUPS, 2, ATTN_TILE, LANES), F32),
                        pltpu.VMEM((N_GROUPS, 2, ATTN_TILE, LANES), F32),
                        pltpu.VMEM((2, 2, SCATTER_STRIDE, ATTN_TILE // SCATTER_STRIDE, LANES), F32)],
        compiler_params=pltpu.CompilerParams(
            dimension_semantics=("arbitrary", "arbitrary"), vmem_limit_bytes=VMEM_LIMIT_BYTES),
        name="dilated_attn",
    )(rel_bias_table.astype(F32), _bucket_tables(), *q, *k, *k, *v, *v)


def _octet_mask(row_group, col_group, rows, cols):
    r = np.arange(rows)[:, None] // row_group
    c = np.arange(cols)[None, :] // col_group
    return jnp.asarray(r == c)


def _s5_matrices(a_re, a_im, log_dt, b_re, b_im, c_re, c_im, d_skip):
    hi = lax.Precision.HIGHEST
    tc = SSM_CHUNK
    lam_re = a_re.astype(F32)
    lam_im = a_im.astype(F32)
    dt = jnp.exp(log_dt.astype(F32))[:, None]
    mag = jnp.exp(lam_re * dt)
    ab_re = mag * jnp.cos(lam_im * dt)
    ab_im = mag * jnp.sin(lam_im * dt)
    den = lam_re * lam_re + lam_im * lam_im
    xr = ab_re - 1.0
    coef_re = (xr * lam_re + ab_im * lam_im) / den
    coef_im = (ab_im * lam_re - xr * lam_im) / den
    br = b_re.astype(F32)
    bi = b_im.astype(F32)
    bb_re = coef_re[..., None] * br - coef_im[..., None] * bi
    bb_im = coef_re[..., None] * bi + coef_im[..., None] * br
    cr = c_re.astype(F32)
    ci = c_im.astype(F32)

    def a_pow(k):
        kk = k.astype(F32)[:, None, None]
        pm = jnp.exp(lam_re * dt * kk)
        return pm * jnp.cos(lam_im * dt * kk), pm * jnp.sin(lam_im * dt * kk)

    pw_re, pw_im = a_pow(jnp.arange(tc + 1))
    pr, pi = pw_re[:tc], pw_im[:tc]
    abr = pr[..., None] * bb_re - pi[..., None] * bb_im
    abi = pr[..., None] * bb_im + pi[..., None] * bb_re
    kern = (jnp.einsum('gon,lgnc->lgoc', cr, abr, precision=hi)
            - jnp.einsum('gon,lgnc->lgoc', ci, abi, precision=hi))
    skip = d_skip.astype(F32).reshape(SSM_GROUPS, SSM_GROUP)
    kern = kern.at[0].add(skip[:, :, None] * jnp.eye(SSM_GROUP, dtype=F32))

    kt = jnp.transpose(kern, (0, 1, 3, 2)).reshape(tc, N_OCTETS, LANES, SSM_GROUP)
    kt = jnp.transpose(kt, (1, 0, 2, 3))
    kblk = jnp.where(_octet_mask(SSM_GROUP, SSM_GROUP, LANES, LANES),
                     jnp.tile(kt, (1, 1, 1, SSM_OCTET)), 0.0)

    def by_octet(x):
        return x.reshape(x.shape[0], N_OCTETS, 1, SSM_OCTET, SSM_STATE)

    def channel_major(x):
        x = x.reshape(N_OCTETS, SSM_OCTET, SSM_GROUP, SSM_STATE)
        return jnp.transpose(x, (0, 2, 1, 3))[None]

    def compact(re, im):
        both = jnp.stack([re, im], axis=1)
        return both.reshape(tc, 2, N_OCTETS, SSM_GROUP, OCTET_STATE).astype(BF16)

    lr, li = by_octet(pr[::-1]), by_octet(pi[::-1])
    bt_re = channel_major(jnp.transpose(bb_re, (0, 2, 1)))
    bt_im = channel_major(jnp.transpose(bb_im, (0, 2, 1)))
    p_src = compact(lr * bt_re - li * bt_im, lr * bt_im + li * bt_re)

    qr, qi = by_octet(pw_re[1:]), by_octet(pw_im[1:])
    ct_re, ct_im = channel_major(cr), channel_major(ci)
    q_src = compact(ct_re * qr - ct_im * qi, -(ct_re * qi + ct_im * qr))

    a_chunk = jnp.concatenate([pw_re[tc].reshape(N_OCTETS, 1, OCTET_STATE),
                               pw_im[tc].reshape(N_OCTETS, 1, OCTET_STATE)], axis=-1)
    return kblk.astype(BF16), p_src, q_src, a_chunk


def _s5_kernel(u_ref, kblk_ref, p_ref, q_ref, a_ref, o_ref, k_scr, p_scr, qt_scr, x_scr, s_scr, z_scr):
    n_chunks = x_scr.shape[0]
    tc = SSM_CHUNK

    @pl.when(pl.program_id(1) == 0)
    def _():
        k_scr[...] = jnp.zeros(k_scr.shape, BF16)
        for s in range(tc):
            for t in range(s, tc):
                k_scr[s * LANES:(s + 1) * LANES, t * LANES:(t + 1) * LANES] = kblk_ref[0, t - s]
        row_group = lax.broadcasted_iota(jnp.int32, (LANES, OCTET_STATE), 0) // SSM_GROUP
        lane_group = lax.broadcasted_iota(jnp.int32, (LANES, OCTET_STATE), 1) // SSM_STATE
        same_group = row_group == lane_group
        zeros = jnp.zeros((LANES, OCTET_STATE), BF16)
        for src_ref, dst in ((p_ref, p_scr), (q_ref, qt_scr)):
            for t in range(tc):
                for part in range(2):
                    src = src_ref[t, part, 0]
                    tiled = jnp.concatenate([src] * SSM_OCTET, axis=0)
                    dst[t * LANES:(t + 1) * LANES, part * OCTET_STATE:(part + 1) * OCTET_STATE] = (
                        jnp.where(same_group, tiled, zeros))

    inc = None
    for pair in range(tc // 2):
        for t in (2 * pair, 2 * pair + 1):
            x_scr[:, t * LANES:(t + 1) * LANES] = (
                u_ref[0, pl.ds(t, n_chunks, stride=CHUNK_PITCH), :].astype(BF16))
        rows = slice(pair * MXU_WIDTH, (pair + 1) * MXU_WIDTH)
        part = jnp.dot(x_scr[:, rows], p_scr[rows, :], preferred_element_type=F32)
        inc = part if inc is None else inc + part
    s_scr[...] = inc

    a_re = a_ref[0, :, 0:OCTET_STATE]
    a_im = a_ref[0, :, OCTET_STATE:2 * OCTET_STATE]

    def step(j, carry):
        c_re, c_im = carry
        row = pl.ds(j, 1)
        z_scr[row, 0:OCTET_STATE] = c_re
        z_scr[row, OCTET_STATE:2 * OCTET_STATE] = c_im
        inc_re = s_scr[row, 0:OCTET_STATE]
        inc_im = s_scr[row, OCTET_STATE:2 * OCTET_STATE]
        return (a_re * c_re - a_im * c_im + inc_re, a_re * c_im + a_im * c_re + inc_im)

    zero = jnp.zeros((1, OCTET_STATE), F32)
    lax.fori_loop(0, n_chunks, step, (zero, zero), unroll=8)

    zb = z_scr[...].astype(BF16)
    for pair in range(tc // 2):
        cols = slice(pair * MXU_WIDTH, (pair + 1) * MXU_WIDTH)
        live = (2 * pair + 2) * LANES
        y = (jnp.dot(x_scr[:, 0:live], k_scr[0:live, cols], preferred_element_type=F32)
             + lax.dot_general(zb, qt_scr[cols, :], (((1,), (1,)), ((), ())),
                               preferred_element_type=F32))
        y = jax.nn.gelu(y)
        o_ref[0, pl.ds(2 * pair, n_chunks, stride=CHUNK_PITCH), :] = y[:, 0:LANES]
        o_ref[0, pl.ds(2 * pair + 1, n_chunks, stride=CHUNK_PITCH), :] = y[:, LANES:2 * LANES]
    for t in range(tc, CHUNK_PITCH):
        o_ref[0, pl.ds(t, n_chunks, stride=CHUNK_PITCH), :] = jnp.zeros((n_chunks, LANES), F32)


def _s5(u, mats, batch, seq):
    kblk, p_src, q_src, a_chunk = mats
    n_chunks = seq // SSM_CHUNK
    width = SSM_CHUNK * LANES

    def per_octet(shape):
        return pl.BlockSpec((1,) + shape, lambda o, b: (o,) + (0,) * len(shape),
                            pipeline_mode=pl.Buffered(1))

    compact_spec = pl.BlockSpec((SSM_CHUNK, 2, 1, SSM_GROUP, OCTET_STATE),
                                lambda o, b: (0, 0, o, 0, 0))
    return pl.pallas_call(
        _s5_kernel,
        grid=(N_OCTETS, batch),
        in_specs=[pl.BlockSpec((1, _padded_rows(seq), LANES), lambda o, b: (o, b, 0)),
                  per_octet((SSM_CHUNK, LANES, LANES)),
                  compact_spec,
                  compact_spec,
                  per_octet((1, 2 * OCTET_STATE))],
        out_specs=pl.BlockSpec((1, _padded_rows(seq), LANES), lambda o, b: (o, b, 0)),
        out_shape=jax.ShapeDtypeStruct((N_OCTETS, _padded_rows(batch * seq), LANES), F32),
        scratch_shapes=[pltpu.VMEM((width, width), BF16),
                        pltpu.VMEM((width, 2 * OCTET_STATE), BF16),
                        pltpu.VMEM((width, 2 * OCTET_STATE), BF16),
                        pltpu.VMEM((n_chunks, width), BF16),
                        pltpu.VMEM((n_chunks, 2 * OCTET_STATE), F32),
                        pltpu.VMEM((n_chunks, 2 * OCTET_STATE), F32)],
        compiler_params=pltpu.CompilerParams(
            dimension_semantics=("arbitrary", "arbitrary"), vmem_limit_bytes=VMEM_LIMIT_BYTES),
        name="s5_mixer",
    )(u, kblk, p_src, q_src, a_chunk)


def _merge_ffn_kernel(y_ref, oa_ref, gate_ref, x_ref, wglu_ref, wssm_ref, wattn_ref, wout_ref,
                      g_ref, wg_ref, wu_ref, wd_ref, fn_ref, o_ref, *, final):
    y_gelu = jnp.concatenate(
        [_load_padded_chunks(y_ref, o, x_ref.shape[0]).astype(BF16) for o in range(N_OCTETS)], axis=1)
    glu = jnp.dot(y_gelu, wglu_ref[...], preferred_element_type=F32)
    ys = (glu[:, 0:SSM_WIDTH] * jax.nn.sigmoid(glu[:, SSM_WIDTH:2 * SSM_WIDTH])).astype(BF16)
    y_ssm = jnp.dot(ys, wssm_ref[...], preferred_element_type=F32)
    y_attn = jnp.dot(oa_ref[...], wattn_ref[...], preferred_element_type=F32)
    g_attn = gate_ref[:, 0:D_MODEL].astype(F32)
    g_ssm = gate_ref[:, D_MODEL:2 * D_MODEL].astype(F32)
    mixed = (g_attn * y_attn + g_ssm * y_ssm).astype(BF16)
    x = x_ref[...] + jnp.dot(mixed, wout_ref[...], preferred_element_type=F32)
    y = _swiglu_residual(x, g_ref[...], wg_ref, wu_ref, wd_ref)
    o_ref[...] = _rms(y, fn_ref[...]) if final else y


def _merge_ffn(y_gelu, o_attn, gates, x, w_glu, w_ssm, w_attn, w_out, g, wg, wu, wd, final_g, final):
    n_tok = x.shape[0]
    tm = MERGE_TILE

    def rows(width):
        return pl.BlockSpec((tm, width), lambda i: (i, 0))

    return pl.pallas_call(
        functools.partial(_merge_ffn_kernel, final=final),
        grid=(n_tok // tm,),
        in_specs=[pl.BlockSpec((N_OCTETS, _padded_rows(tm), LANES), lambda i: (0, i, 0)),
                  rows(GROUP_WIDTH), rows(2 * D_MODEL), rows(D_MODEL),
                  _const_spec((SSM_WIDTH, 2 * SSM_WIDTH)), _const_spec((SSM_WIDTH, D_MODEL)),
                  _const_spec((GROUP_WIDTH, D_MODEL)), _const_spec((D_MODEL, D_MODEL))]
                 + _ffn_weight_specs() + [_const_spec((1, D_MODEL))],
        out_specs=rows(D_MODEL),
        out_shape=jax.ShapeDtypeStruct((n_tok, D_MODEL), F32),
        compiler_params=pltpu.CompilerParams(
            dimension_semantics=("arbitrary",), vmem_limit_bytes=VMEM_LIMIT_BYTES),
        name="merge_ffn",
    )(y_gelu, o_attn, gates, x, w_glu, w_ssm, w_attn, w_out,
      g.reshape(1, D_MODEL), wg, wu, wd, final_g.reshape(1, D_MODEL))


def kernel(x, ffn1_norm, ffn1_w_gate, ffn1_w_up, ffn1_w_down, mix_norm, w_in, gate_bias,
           rel_bias_table, ssm_a_re, ssm_a_im, ssm_log_dt, ssm_b_re, ssm_b_im, ssm_c_re,
           ssm_c_im, ssm_d, ssm_w_glu, w_attn_branch, w_ssm_branch, w_out, ffn2_norm,
           ffn2_w_gate, ffn2_w_up, ffn2_w_down, final_norm):
    batch, seq, _ = x.shape
    depth = ffn1_norm.shape[0]
    h = x.reshape(batch * seq, D_MODEL)
    for l in range(depth):
        h = _ffn(h, ffn1_norm[l], ffn1_w_gate[l].astype(BF16), ffn1_w_up[l].astype(BF16),
                 ffn1_w_down[l].astype(BF16))
        q, k, v, u, gates = _in_proj(h, mix_norm[l], w_in[l].astype(BF16), gate_bias[l], batch, seq)
        o_attn = _attention(q, k, v, rel_bias_table, batch, seq)
        mats = _s5_matrices(ssm_a_re[l], ssm_a_im[l], ssm_log_dt[l], ssm_b_re[l], ssm_b_im[l],
                            ssm_c_re[l], ssm_c_im[l], ssm_d[l])
        y_gelu = _s5(u, mats, batch, seq)
        h = _merge_ffn(y_gelu, o_attn, gates, h, ssm_w_glu[l].astype(BF16),
                       w_ssm_branch[l].astype(BF16), w_attn_branch[l].astype(BF16),
                       w_out[l].astype(BF16), ffn2_norm[l], ffn2_w_gate[l].astype(BF16),
                       ffn2_w_up[l].astype(BF16), ffn2_w_down[l].astype(BF16), final_norm,
                       final=(l == depth - 1))
    return h.reshape(batch, seq, D_MODEL)
```

```python
import functools
import math

import jax
import jax.numpy as jnp
import numpy as np
from jax import lax
from jax.experimental import pallas as pl
from jax.experimental.pallas import tpu as pltpu

F32 = jnp.float32
BF16 = jnp.bfloat16

D_MODEL = 1024
D_FF = 2816
EPS = 1e-6
HEAD_DIM = 64
HEADS_PER_GROUP = 4
GROUP_WIDTH = HEADS_PER_GROUP * HEAD_DIM
DILATIONS = (1, 4, 16)
WINDOWS = (128, 512, 2048)
N_GROUPS = len(DILATIONS)
ATTN_WIDTH = N_GROUPS * GROUP_WIDTH
BLOCK = 128
N_BUCKETS = 32
MAX_DISTANCE = 2048
NEG_INF = -1e30
SSM_GROUP = 16
SSM_WIDTH = 512
SSM_GROUPS = SSM_WIDTH // SSM_GROUP
SSM_STATE = 64
IN_WIDTH = 3 * ATTN_WIDTH + SSM_WIDTH + 2 * D_MODEL

LANES = 128
MXU_WIDTH = 256
VMEM_LIMIT_BYTES = 56 * 1024 * 1024

FFN_TILE = 1024
PROJ_TILE = 1024
MERGE_TILE = 512
WEIGHT_STAGE_STEPS = 8
ATTN_TILE = BLOCK * DILATIONS[-1]
ATTN_INTERLEAVE = 4
SCATTER_STRIDE = 4
SSM_CHUNK = 16
CHUNK_PITCH = SSM_CHUNK + 4
SSM_OCTET = LANES // SSM_GROUP
N_OCTETS = SSM_GROUPS // SSM_OCTET
OCTET_STATE = SSM_OCTET * SSM_STATE


def _rms(x, g):
    return x * lax.rsqrt(jnp.mean(x * x, axis=-1, keepdims=True) + EPS) * g


def _const_spec(shape):
    return pl.BlockSpec(shape, lambda *_: (0,) * len(shape), pipeline_mode=pl.Buffered(1))


def _swiglu_residual(x, g, wg_ref, wu_ref, wd_ref):
    h = _rms(x, g).astype(BF16)
    acc = jnp.zeros(x.shape, F32)
    for c in range(D_FF // MXU_WIDTH):
        sl = slice(c * MXU_WIDTH, (c + 1) * MXU_WIDTH)
        a = jnp.dot(h, wg_ref[:, sl], preferred_element_type=F32)
        b = jnp.dot(h, wu_ref[:, sl], preferred_element_type=F32)
        t = (a * jax.nn.sigmoid(a) * b).astype(BF16)
        acc = acc + jnp.dot(t, wd_ref[sl, :], preferred_element_type=F32)
    return x + 0.5 * acc


def _weight_copy(w_hbm, stage, sem, i):
    rows = stage.shape[1]
    return pltpu.make_async_copy(w_hbm.at[pl.ds(i * rows, rows), :], stage.at[i % 2], sem.at[i % 2])


def _fetch_cast(w_hbm, w_scr, stage, sem):
    rows = stage.shape[1]
    n = w_hbm.shape[0] // rows
    _weight_copy(w_hbm, stage, sem, 0).start()
    for i in range(n):
        if i + 1 < n:
            _weight_copy(w_hbm, stage, sem, i + 1).start()
        _weight_copy(w_hbm, stage, sem, i).wait()
        w_scr[i * rows:(i + 1) * rows, :] = stage[i % 2].astype(BF16)


def _fetch_ffn_weights(wg_hbm, wu_hbm, wd_hbm, wg_scr, wu_scr, wd_scr, up_stage, down_stage, sem):
    _fetch_cast(wg_hbm, wg_scr, up_stage, sem.at[0])
    _fetch_cast(wu_hbm, wu_scr, up_stage, sem.at[0])
    _fetch_cast(wd_hbm, wd_scr, down_stage, sem.at[1])


def _ffn_kernel(x_ref, g_ref, wg_hbm, wu_hbm, wd_hbm, o_ref, *weight_scratch):
    @pl.when(pl.program_id(0) == 0)
    def _():
        _fetch_ffn_weights(wg_hbm, wu_hbm, wd_hbm, *weight_scratch)

    wg_scr, wu_scr, wd_scr = weight_scratch[0:3]
    o_ref[...] = _swiglu_residual(x_ref[...], g_ref[...], wg_scr, wu_scr, wd_scr)


def _ffn_weight_specs():
    hbm = pl.BlockSpec(memory_space=pl.ANY)
    return [_const_spec((1, D_MODEL)), hbm, hbm, hbm]


def _ffn_weight_scratch():
    return [pltpu.VMEM((D_MODEL, D_FF), BF16), pltpu.VMEM((D_MODEL, D_FF), BF16),
            pltpu.VMEM((D_FF, D_MODEL), BF16),
            pltpu.VMEM((2, D_MODEL // WEIGHT_STAGE_STEPS, D_FF), F32),
            pltpu.VMEM((2, D_FF // WEIGHT_STAGE_STEPS, D_MODEL), F32),
            pltpu.SemaphoreType.DMA((2, 2))]


def _ffn(x, g, wg, wu, wd):
    n_tok = x.shape[0]
    tm = FFN_TILE
    tile = pl.BlockSpec((tm, D_MODEL), lambda i: (i, 0))
    return pl.pallas_call(
        _ffn_kernel,
        grid=(n_tok // tm,),
        in_specs=[tile] + _ffn_weight_specs(),
        out_specs=tile,
        out_shape=jax.ShapeDtypeStruct((n_tok, D_MODEL), F32),
        scratch_shapes=_ffn_weight_scratch(),
        compiler_params=pltpu.CompilerParams(
            dimension_semantics=("arbitrary",), vmem_limit_bytes=VMEM_LIMIT_BYTES),
        name="ffn",
    )(x, g.reshape(1, D_MODEL), wg, wu, wd)


N_QKV_CHUNKS = 3 * N_GROUPS
N_U_CHUNKS = SSM_WIDTH // MXU_WIDTH
N_GATE_CHUNKS = 2 * D_MODEL // MXU_WIDTH
N_STRIDED = 3 * (N_GROUPS - 1)
N_TWO_HOP = 3 * sum(d > SCATTER_STRIDE for d in DILATIONS)


def _padded_rows(n_tokens):
    return n_tokens // SSM_CHUNK * CHUNK_PITCH


def _store_padded_chunks(ref, slab, rows):
    pad = jnp.zeros((CHUNK_PITCH - SSM_CHUNK, rows.shape[1]), rows.dtype)
    for ch in range(rows.shape[0] // SSM_CHUNK):
        base = ch * CHUNK_PITCH
        ref[slab, base:base + SSM_CHUNK, :] = rows[ch * SSM_CHUNK:(ch + 1) * SSM_CHUNK, :]
        ref[slab, base + SSM_CHUNK:base + CHUNK_PITCH, :] = pad


def _load_padded_chunks(ref, slab, n_tokens):
    return jnp.concatenate(
        [ref[slab, ch * CHUNK_PITCH:ch * CHUNK_PITCH + SSM_CHUNK, :]
         for ch in range(n_tokens // SSM_CHUNK)], axis=0)


def _inproj_kernel(x_ref, g_ref, w_hbm, gb_ref, *refs):
    qkv_refs = refs[:N_QKV_CHUNKS]
    u_ref, gate_ref, slab_ref, hop_ref, w_ref, w_stage, w_sem = refs[N_QKV_CHUNKS:]
    tm = x_ref.shape[0]

    @pl.when(pl.program_id(0) == 0)
    def _():
        _fetch_cast(w_hbm, w_ref, w_stage, w_sem)

    h = _rms(x_ref[...], g_ref[...]).astype(BF16)
    strided_idx = 0
    hop_idx = 0
    heavy = list(range(N_QKV_CHUNKS + N_U_CHUNKS))
    light = list(range(N_QKV_CHUNKS + N_U_CHUNKS, IN_WIDTH // MXU_WIDTH))
    order = []
    while heavy or light:
        if heavy:
            order.append(heavy.pop(0))
        if light:
            order.append(light.pop(0))
    for c in order:
        z = jnp.dot(h, w_ref[:, c * MXU_WIDTH:(c + 1) * MXU_WIDTH], preferred_element_type=F32)
        if c < N_QKV_CHUNKS:
            kind, grp = divmod(c, N_GROUPS)
            if kind == 0:
                z = z * (HEAD_DIM ** -0.5)
            d = DILATIONS[grp]
            o_ref = qkv_refs[c]
            if d == 1:
                o_ref[0, 0] = z.astype(BF16)
            else:
                base = 2 * strided_idx
                strided_idx += 1
                for s in range(2):
                    slab_ref[base + s] = z[:, s * LANES:(s + 1) * LANES]
                lanes = [slice(s * LANES, (s + 1) * LANES) for s in range(2)]
                if d <= SCATTER_STRIDE:
                    for r in range(d):
                        for s in range(2):
                            o_ref[0, r, :, lanes[s]] = (
                                slab_ref[base + s, pl.ds(r, tm // d, stride=d), :].astype(BF16))
                else:
                    hop = tm // SCATTER_STRIDE
                    hop_base = 2 * hop_idx
                    hop_idx += 1
                    for s in range(2):
                        for low in range(SCATTER_STRIDE):
                            hop_ref[hop_base + s, pl.ds(low * hop, hop), :] = (
                                slab_ref[base + s, pl.ds(low, hop, stride=SCATTER_STRIDE), :])
                    inner = d // SCATTER_STRIDE
                    for r in range(d):
                        low, high = r % SCATTER_STRIDE, r // SCATTER_STRIDE
                        for s in range(2):
                            o_ref[0, r, :, lanes[s]] = hop_ref[
                                hop_base + s, pl.ds(low * hop + high, tm // d, stride=inner), :
                            ].astype(BF16)
        elif c < N_QKV_CHUNKS + N_U_CHUNKS:
            j = c - N_QKV_CHUNKS
            for s in range(2):
                _store_padded_chunks(u_ref, 2 * j + s, z[:, s * LANES:(s + 1) * LANES])
        else:
            j = c - N_QKV_CHUNKS - N_U_CHUNKS
            sl = slice(j * MXU_WIDTH, (j + 1) * MXU_WIDTH)
            gate_ref[:, sl] = jax.nn.sigmoid(z + gb_ref[:, sl]).astype(BF16)


def _in_proj(x, g, w_in, gate_bias, batch, seq):
    n_tok = x.shape[0]
    tm = PROJ_TILE
    tiles_per_seq = seq // tm
    out_shapes, out_specs = [], []
    for _ in range(3):
        for d in DILATIONS:
            out_shapes.append(jax.ShapeDtypeStruct((batch, d, seq // d, GROUP_WIDTH), BF16))
            out_specs.append(pl.BlockSpec(
                (1, d, tm // d, GROUP_WIDTH),
                lambda i: (i // tiles_per_seq, 0, i % tiles_per_seq, 0)))
    out_shapes.append(jax.ShapeDtypeStruct((N_OCTETS, _padded_rows(n_tok), LANES), F32))
    out_specs.append(pl.BlockSpec((N_OCTETS, _padded_rows(tm), LANES), lambda i: (0, i, 0)))
    out_shapes.append(jax.ShapeDtypeStruct((n_tok, 2 * D_MODEL), BF16))
    out_specs.append(pl.BlockSpec((tm, 2 * D_MODEL), lambda i: (i, 0)))
    outs = pl.pallas_call(
        _inproj_kernel,
        grid=(n_tok // tm,),
        in_specs=[pl.BlockSpec((tm, D_MODEL), lambda i: (i, 0)),
                  _const_spec((1, D_MODEL)),
                  pl.BlockSpec(memory_space=pl.ANY),
                  _const_spec((1, 2 * D_MODEL))],
        out_specs=out_specs,
        out_shape=out_shapes,
        scratch_shapes=[pltpu.VMEM((2 * N_STRIDED, tm, LANES), F32),
                        pltpu.VMEM((2 * N_TWO_HOP, tm, LANES), F32),
                        pltpu.VMEM((D_MODEL, IN_WIDTH), BF16),
                        pltpu.VMEM((2, D_MODEL // (2 * WEIGHT_STAGE_STEPS), IN_WIDTH), F32),
                        pltpu.SemaphoreType.DMA((2,))],
        compiler_params=pltpu.CompilerParams(
            dimension_semantics=("arbitrary",), vmem_limit_bytes=VMEM_LIMIT_BYTES),
        name="in_proj",
    )(x, g.reshape(1, D_MODEL), w_in, gate_bias.reshape(1, 2 * D_MODEL))
    q, k, v = outs[0:3], outs[3:6], outs[6:9]
    return q, k, v, outs[9], outs[10]


def _t5_bucket_np(dist):
    max_exact = N_BUCKETS // 2
    d = np.maximum(dist, 1).astype(np.float32)
    ratio = np.log(d / np.float32(max_exact)) / np.float32(math.log(MAX_DISTANCE / max_exact))
    large = max_exact + (ratio * np.float32(N_BUCKETS - max_exact)).astype(np.int32)
    large = np.minimum(large, N_BUCKETS - 1)
    return np.where(dist < max_exact, dist, large)


def _bucket_tables():
    qi = np.arange(BLOCK)[:, None]
    kj = np.arange(2 * BLOCK)[None, :]
    steps = qi + BLOCK - kj
    tables = []
    for window, d in zip(WINDOWS, DILATIONS):
        band = (steps >= 0) & (steps <= window // d)
        bucket = _t5_bucket_np(np.maximum(steps, 0) * d)
        tables.append(np.where(band, bucket, -1).astype(np.int32))
    return jnp.asarray(np.stack(tables))


def _attn_block(q_blk, k_blk, v_blk, bias):
    lane_head = lax.broadcasted_iota(jnp.int32, (BLOCK, GROUP_WIDTH), 1) // HEAD_DIM
    zero = jnp.zeros_like(q_blk)
    qs = jnp.concatenate(
        [jnp.where(lane_head == h, q_blk, zero) for h in range(HEADS_PER_GROUP)], axis=0)
    logits = lax.dot_general(qs, k_blk, (((1,), (1,)), ((), ())), preferred_element_type=F32) + bias
    m = jnp.max(logits, axis=-1, keepdims=True)
    p = jnp.exp(logits - m)
    l = jnp.sum(p, axis=-1, keepdims=True)
    pv = jnp.dot(p.astype(BF16), v_blk, preferred_element_type=F32)
    o = jnp.zeros((BLOCK, GROUP_WIDTH), F32)
    den = jnp.ones((BLOCK, GROUP_WIDTH), F32)
    mx = jnp.zeros((BLOCK, GROUP_WIDTH), F32)
    for h in range(HEADS_PER_GROUP):
        rows = slice(h * BLOCK, (h + 1) * BLOCK)
        sel = lane_head == h
        o = jnp.where(sel, pv[rows], o)
        den = jnp.where(sel, l[rows], den)
        mx = jnp.where(sel, m[rows], mx)
    return o / den, mx + jnp.log(den)


def _build_bias(tab_ref, bucket_ref, bias_scr):
    own_block = lax.broadcasted_iota(jnp.int32, (BLOCK, 2 * BLOCK), 1) >= BLOCK
    for grp in range(N_GROUPS):
        bucket = bucket_ref[grp]
        for h in range(HEADS_PER_GROUP):
            col = grp * HEADS_PER_GROUP + h

            def pick(b, acc, bucket=bucket, col=col):
                return jnp.where(bucket == b, tab_ref[b, col], acc)

            bias = lax.fori_loop(0, N_BUCKETS, pick, jnp.full((BLOCK, 2 * BLOCK), NEG_INF, F32))
            rows = slice(h * BLOCK, (h + 1) * BLOCK)
            bias_scr[grp, 0, rows, :] = bias
            bias_scr[grp, 1, rows, :] = jnp.where(own_block, bias, NEG_INF)


def _block_aligned(row):
    return row if isinstance(row, int) else pl.multiple_of(row, BLOCK)


def _attn_kernel(*refs):
    n = N_GROUPS
    tab_ref, bucket_ref = refs[0:2]
    refs = refs[2:]
    q_refs, kc_refs, kp_refs = refs[0:n], refs[n:2 * n], refs[2 * n:3 * n]
    vc_refs, vp_refs = refs[3 * n:4 * n], refs[4 * n:5 * n]
    o_ref, bias_scr, o_scr, l_scr, t_scr = refs[5 * n:]
    tile = pl.program_id(1)

    @pl.when((pl.program_id(0) == 0) & (tile == 0))
    def _():
        _build_bias(tab_ref, bucket_ref, bias_scr)

    for grp, d in enumerate(DILATIONS):
        q_ref, kc_ref, kp_ref = q_refs[grp], kc_refs[grp], kp_refs[grp]
        vc_ref, vp_ref = vc_refs[grp], vp_refs[grp]
        blocks_per_residue = ATTN_TILE // d // BLOCK

        def run_block(r, blk, grp=grp, d=d, q_ref=q_ref, kc_ref=kc_ref, kp_ref=kp_ref,
                      vc_ref=vc_ref, vp_ref=vp_ref):
            if isinstance(blk, int) and blk == 0:
                k_blk = jnp.concatenate([kp_ref[0, r], kc_ref[0, r, 0:BLOCK, :]], axis=0)
                v_blk = jnp.concatenate([vp_ref[0, r], vc_ref[0, r, 0:BLOCK, :]], axis=0)
                bias = jnp.where(tile == 0, bias_scr[grp, 1], bias_scr[grp, 0])
                q_blk = q_ref[0, r, 0:BLOCK, :]
            else:
                kv_rows = pl.ds(_block_aligned((blk - 1) * BLOCK), 2 * BLOCK)
                k_blk = kc_ref[0, r, kv_rows, :]
                v_blk = vc_ref[0, r, kv_rows, :]
                bias = bias_scr[grp, 0]
                q_blk = q_ref[0, r, pl.ds(_block_aligned(blk * BLOCK), BLOCK), :]
            o, lse = _attn_block(q_blk, k_blk, v_blk, bias)
            if d <= SCATTER_STRIDE:
                start = blk * (BLOCK * d) + r
                rows = pl.ds(start, BLOCK, stride=d) if d > 1 else pl.ds(start, BLOCK)
                for s in range(2):
                    o_scr[grp, s, rows, :] = o[:, s * LANES:(s + 1) * LANES]
                    l_scr[grp, s, rows, :] = lse[:, s * LANES:(s + 1) * LANES]
            else:
                inner = d // SCATTER_STRIDE
                low, high = lax.rem(r, SCATTER_STRIDE), r // SCATTER_STRIDE
                rows = pl.ds(blk * (BLOCK * inner) + high, BLOCK, stride=inner)
                for s in range(2):
                    t_scr[0, s, low, rows, :] = o[:, s * LANES:(s + 1) * LANES]
                    t_scr[1, s, low, rows, :] = lse[:, s * LANES:(s + 1) * LANES]

        if blocks_per_residue == 1:
            def residue_set(i, c, run_block=run_block):
                for k in range(ATTN_INTERLEAVE):
                    run_block(ATTN_INTERLEAVE * i + k, 0)
                return c
            lax.fori_loop(0, d // ATTN_INTERLEAVE, residue_set, 0)
        elif d > 1:
            def residue(r, c, run_block=run_block, blocks_per_residue=blocks_per_residue):
                run_block(r, 0)
                for blk in range(1, blocks_per_residue):
                    run_block(r, blk)
                return c
            lax.fori_loop(0, d, residue, 0)
        else:
            for blk in range(ATTN_INTERLEAVE):
                run_block(0, blk)

            def block_set(i, c, run_block=run_block):
                for k in range(ATTN_INTERLEAVE):
                    run_block(0, ATTN_INTERLEAVE * i + k)
                return c
            lax.fori_loop(1, blocks_per_residue // ATTN_INTERLEAVE, block_set, 0)

        if d > SCATTER_STRIDE:
            def unstage(low, c, grp=grp):
                rows = pl.ds(low, ATTN_TILE // SCATTER_STRIDE, stride=SCATTER_STRIDE)
                for s in range(2):
                    o_scr[grp, s, rows, :] = t_scr[0, s, low]
                    l_scr[grp, s, rows, :] = t_scr[1, s, low]
                return c
            lax.fori_loop(0, SCATTER_STRIDE, unstage, 0)

    merge_rows = 256

    def merge(c, carry):
        rows = pl.ds(pl.multiple_of(c * merge_rows, merge_rows), merge_rows)
        for s in range(2):
            lses = [l_scr[grp, s, rows, :] for grp in range(N_GROUPS)]
            mx = functools.reduce(jnp.maximum, lses)
            es = [jnp.exp(x - mx) for x in lses]
            num = sum(e * o_scr[grp, s, rows, :] for grp, e in enumerate(es))
            o_ref[rows, s * LANES:(s + 1) * LANES] = (num / sum(es)).astype(BF16)
        return carry

    lax.fori_loop(0, ATTN_TILE // merge_rows, merge, 0)


def _attention(q, k, v, rel_bias_table, batch, seq):
    tiles = seq // ATTN_TILE

    def cur_spec(d):
        return pl.BlockSpec((1, d, ATTN_TILE // d, GROUP_WIDTH), lambda b, j: (b, 0, j, 0))

    def prev_spec(d):
        per_tile = ATTN_TILE // d // BLOCK
        return pl.BlockSpec((1, d, BLOCK, GROUP_WIDTH),
                            lambda b, j: (b, 0, jnp.maximum(j * per_tile - 1, 0), 0))

    in_specs = ([pl.BlockSpec(memory_space=pltpu.SMEM),
                 _const_spec((N_GROUPS, BLOCK, 2 * BLOCK))]
                + [cur_spec(d) for d in DILATIONS] + [cur_spec(d) for d in DILATIONS]
                + [prev_spec(d) for d in DILATIONS] + [cur_spec(d) for d in DILATIONS]
                + [prev_spec(d) for d in DILATIONS])
    return pl.pallas_call(
        _attn_kernel,
        grid=(batch, tiles),
        in_specs=in_specs,
        out_specs=pl.BlockSpec((ATTN_TILE, GROUP_WIDTH), lambda b, j: (b * tiles + j, 0)),
        out_shape=jax.ShapeDtypeStruct((batch * seq, GROUP_WIDTH), BF16),
        scratch_shapes=[pltpu.VMEM((N_GROUPS, 2, HEADS_PER_GROUP * BLOCK, 2 * BLOCK), F32),
                        pltpu.VMEM((N_GROUPS, 2, ATTN_TILE, LANES), F32),
                        pltpu.VMEM((N_GROUPS, 2, ATTN_TILE, LANES), F32),
                        pltpu.VMEM((2, 2, SCATTER_STRIDE, ATTN_TILE // SCATTER_STRIDE, LANES), F32)],
        compiler_params=pltpu.CompilerParams(
            dimension_semantics=("arbitrary", "arbitrary"), vmem_limit_bytes=VMEM_LIMIT_BYTES),
        name="dilated_attn",
    )(rel_bias_table.astype(F32), _bucket_tables(), *q, *k, *k, *v, *v)


def _octet_mask(row_group, col_group, rows, cols):
    r = np.arange(rows)[:, None] // row_group
    c = np.arange(cols)[None, :] // col_group
    return jnp.asarray(r == c)


def _s5_matrices(a_re, a_im, log_dt, b_re, b_im, c_re, c_im, d_skip):
    hi = lax.Precision.HIGHEST
    tc = SSM_CHUNK
    lam_re = a_re.astype(F32)
    lam_im = a_im.astype(F32)
    dt = jnp.exp(log_dt.astype(F32))[:, None]
    mag = jnp.exp(lam_re * dt)
    ab_re = mag * jnp.cos(lam_im * dt)
    ab_im = mag * jnp.sin(lam_im * dt)
    den = lam_re * lam_re + lam_im * lam_im
    xr = ab_re - 1.0
    coef_re = (xr * lam_re + ab_im * lam_im) / den
    coef_im = (ab_im * lam_re - xr * lam_im) / den
    br = b_re.astype(F32)
    bi = b_im.astype(F32)
    bb_re = coef_re[..., None] * br - coef_im[..., None] * bi
    bb_im = coef_re[..., None] * bi + coef_im[..., None] * br
    cr = c_re.astype(F32)
    ci = c_im.astype(F32)

    def a_pow(k):
        kk = k.astype(F32)[:, None, None]
        pm = jnp.exp(lam_re * dt * kk)
        return pm * jnp.cos(lam_im * dt * kk), pm * jnp.sin(lam_im * dt * kk)

    pw_re, pw_im = a_pow(jnp.arange(tc + 1))
    pr, pi = pw_re[:tc], pw_im[:tc]
    abr = pr[..., None] * bb_re - pi[..., None] * bb_im
    abi = pr[..., None] * bb_im + pi[..., None] * bb_re
    kern = (jnp.einsum('gon,lgnc->lgoc', cr, abr, precision=hi)
            - jnp.einsum('gon,lgnc->lgoc', ci, abi, precision=hi))
    skip = d_skip.astype(F32).reshape(SSM_GROUPS, SSM_GROUP)
    kern = kern.at[0].add(skip[:, :, None] * jnp.eye(SSM_GROUP, dtype=F32))

    kt = jnp.transpose(kern, (0, 1, 3, 2)).reshape(tc, N_OCTETS, LANES, SSM_GROUP)
    kt = jnp.transpose(kt, (1, 0, 2, 3))
    kblk = jnp.where(_octet_mask(SSM_GROUP, SSM_GROUP, LANES, LANES),
                     jnp.tile(kt, (1, 1, 1, SSM_OCTET)), 0.0)

    def by_octet(x):
        return x.reshape(x.shape[0], N_OCTETS, 1, SSM_OCTET, SSM_STATE)

    def channel_major(x):
        x = x.reshape(N_OCTETS, SSM_OCTET, SSM_GROUP, SSM_STATE)
        return jnp.transpose(x, (0, 2, 1, 3))[None]

    def compact(re, im):
        both = jnp.stack([re, im], axis=1)
        return both.reshape(tc, 2, N_OCTETS, SSM_GROUP, OCTET_STATE).astype(BF16)

    lr, li = by_octet(pr[::-1]), by_octet(pi[::-1])
    bt_re = channel_major(jnp.transpose(bb_re, (0, 2, 1)))
    bt_im = channel_major(jnp.transpose(bb_im, (0, 2, 1)))
    p_src = compact(lr * bt_re - li * bt_im, lr * bt_im + li * bt_re)

    qr, qi = by_octet(pw_re[1:]), by_octet(pw_im[1:])
    ct_re, ct_im = channel_major(cr), channel_major(ci)
    q_src = compact(ct_re * qr - ct_im * qi, -(ct_re * qi + ct_im * qr))

    a_chunk = jnp.concatenate([pw_re[tc].reshape(N_OCTETS, 1, OCTET_STATE),
                               pw_im[tc].reshape(N_OCTETS, 1, OCTET_STATE)], axis=-1)
    return kblk.astype(BF16), p_src, q_src, a_chunk


def _s5_kernel(u_ref, kblk_ref, p_ref, q_ref, a_ref, o_ref, k_scr, p_scr, qt_scr, x_scr, s_scr, z_scr):
    n_chunks = x_scr.shape[0]
    tc = SSM_CHUNK

    @pl.when(pl.program_id(1) == 0)
    def _():
        k_scr[...] = jnp.zeros(k_scr.shape, BF16)
        for s in range(tc):
            for t in range(s, tc):
                k_scr[s * LANES:(s + 1) * LANES, t * LANES:(t + 1) * LANES] = kblk_ref[0, t - s]
        row_group = lax.broadcasted_iota(jnp.int32, (LANES, OCTET_STATE), 0) // SSM_GROUP
        lane_group = lax.broadcasted_iota(jnp.int32, (LANES, OCTET_STATE), 1) // SSM_STATE
        same_group = row_group == lane_group
        zeros = jnp.zeros((LANES, OCTET_STATE), BF16)
        for src_ref, dst in ((p_ref, p_scr), (q_ref, qt_scr)):
            for t in range(tc):
                for part in range(2):
                    src = src_ref[t, part, 0]
                    tiled = jnp.concatenate([src] * SSM_OCTET, axis=0)
                    dst[t * LANES:(t + 1) * LANES, part * OCTET_STATE:(part + 1) * OCTET_STATE] = (
                        jnp.where(same_group, tiled, zeros))

    inc = None
    for pair in range(tc // 2):
        for t in (2 * pair, 2 * pair + 1):
            x_scr[:, t * LANES:(t + 1) * LANES] = (
                u_ref[0, pl.ds(t, n_chunks, stride=CHUNK_PITCH), :].astype(BF16))
        rows = slice(pair * MXU_WIDTH, (pair + 1) * MXU_WIDTH)
        part = jnp.dot(x_scr[:, rows], p_scr[rows, :], preferred_element_type=F32)
        inc = part if inc is None else inc + part
    s_scr[...] = inc

    a_re = a_ref[0, :, 0:OCTET_STATE]
    a_im = a_ref[0, :, OCTET_STATE:2 * OCTET_STATE]

    def step(j, carry):
        c_re, c_im = carry
        row = pl.ds(j, 1)
        z_scr[row, 0:OCTET_STATE] = c_re
        z_scr[row, OCTET_STATE:2 * OCTET_STATE] = c_im
        inc_re = s_scr[row, 0:OCTET_STATE]
        inc_im = s_scr[row, OCTET_STATE:2 * OCTET_STATE]
        return (a_re * c_re - a_im * c_im + inc_re, a_re * c_im + a_im * c_re + inc_im)

    zero = jnp.zeros((1, OCTET_STATE), F32)
    lax.fori_loop(0, n_chunks, step, (zero, zero), unroll=8)

    zb = z_scr[...].astype(BF16)
    for pair in range(tc // 2):
        cols = slice(pair * MXU_WIDTH, (pair + 1) * MXU_WIDTH)
        live = (2 * pair + 2) * LANES
        y = (jnp.dot(x_scr[:, 0:live], k_scr[0:live, cols], preferred_element_type=F32)
             + lax.dot_general(zb, qt_scr[cols, :], (((1,), (1,)), ((), ())),
                               preferred_element_type=F32))
        y = jax.nn.gelu(y)
        o_ref[0, pl.ds(2 * pair, n_chunks, stride=CHUNK_PITCH), :] = y[:, 0:LANES]
        o_ref[0, pl.ds(2 * pair + 1, n_chunks, stride=CHUNK_PITCH), :] = y[:, LANES:2 * LANES]
    for t in range(tc, CHUNK_PITCH):
        o_ref[0, pl.ds(t, n_chunks, stride=CHUNK_PITCH), :] = jnp.zeros((n_chunks, LANES), F32)


def _s5(u, mats, batch, seq):
    kblk, p_src, q_src, a_chunk = mats
    n_chunks = seq // SSM_CHUNK
    width = SSM_CHUNK * LANES

    def per_octet(shape):
        return pl.BlockSpec((1,) + shape, lambda o, b: (o,) + (0,) * len(shape),
                            pipeline_mode=pl.Buffered(1))

    compact_spec = pl.BlockSpec((SSM_CHUNK, 2, 1, SSM_GROUP, OCTET_STATE),
                                lambda o, b: (0, 0, o, 0, 0))
    return pl.pallas_call(
        _s5_kernel,
        grid=(N_OCTETS, batch),
        in_specs=[pl.BlockSpec((1, _padded_rows(seq), LANES), lambda o, b: (o, b, 0)),
                  per_octet((SSM_CHUNK, LANES, LANES)),
                  compact_spec,
                  compact_spec,
                  per_octet((1, 2 * OCTET_STATE))],
        out_specs=pl.BlockSpec((1, _padded_rows(seq), LANES), lambda o, b: (o, b, 0)),
        out_shape=jax.ShapeDtypeStruct((N_OCTETS, _padded_rows(batch * seq), LANES), F32),
        scratch_shapes=[pltpu.VMEM((width, width), BF16),
                        pltpu.VMEM((width, 2 * OCTET_STATE), BF16),
                        pltpu.VMEM((width, 2 * OCTET_STATE), BF16),
                        pltpu.VMEM((n_chunks, width), BF16),
                        pltpu.VMEM((n_chunks, 2 * OCTET_STATE), F32),
                        pltpu.VMEM((n_chunks, 2 * OCTET_STATE), F32)],
        compiler_params=pltpu.CompilerParams(
            dimension_semantics=("arbitrary", "arbitrary"), vmem_limit_bytes=VMEM_LIMIT_BYTES),
        name="s5_mixer",
    )(u, kblk, p_src, q_src, a_chunk)


def _merge_ffn_kernel(y_ref, oa_ref, gate_ref, x_ref, wglu_ref, wssm_ref, wattn_ref, wout_ref,
                      g_ref, wg_hbm, wu_hbm, wd_hbm, fn_ref, o_ref, *weight_scratch, final):
    @pl.when(pl.program_id(0) == 0)
    def _():
        _fetch_ffn_weights(wg_hbm, wu_hbm, wd_hbm, *weight_scratch)

    wg_scr, wu_scr, wd_scr = weight_scratch[0:3]
    y_gelu = jnp.concatenate(
        [_load_padded_chunks(y_ref, o, x_ref.shape[0]).astype(BF16) for o in range(N_OCTETS)], axis=1)
    glu = jnp.dot(y_gelu, wglu_ref[...], preferred_element_type=F32)
    ys = (glu[:, 0:SSM_WIDTH] * jax.nn.sigmoid(glu[:, SSM_WIDTH:2 * SSM_WIDTH])).astype(BF16)
    y_ssm = jnp.dot(ys, wssm_ref[...], preferred_element_type=F32)
    y_attn = jnp.dot(oa_ref[...], wattn_ref[...], preferred_element_type=F32)
    g_attn = gate_ref[:, 0:D_MODEL].astype(F32)
    g_ssm = gate_ref[:, D_MODEL:2 * D_MODEL].astype(F32)
    mixed = (g_attn * y_attn + g_ssm * y_ssm).astype(BF16)
    x = x_ref[...] + jnp.dot(mixed, wout_ref[...], preferred_element_type=F32)
    y = _swiglu_residual(x, g_ref[...], wg_scr, wu_scr, wd_scr)
    o_ref[...] = _rms(y, fn_ref[...]) if final else y


def _merge_ffn(y_gelu, o_attn, gates, x, w_glu, w_ssm, w_attn, w_out, g, wg, wu, wd, final_g, final):
    n_tok = x.shape[0]
    tm = MERGE_TILE

    def rows(width):
        return pl.BlockSpec((tm, width), lambda i: (i, 0))

    return pl.pallas_call(
        functools.partial(_merge_ffn_kernel, final=final),
        grid=(n_tok // tm,),
        in_specs=[pl.BlockSpec((N_OCTETS, _padded_rows(tm), LANES), lambda i: (0, i, 0)),
                  rows(GROUP_WIDTH), rows(2 * D_MODEL), rows(D_MODEL),
                  _const_spec((SSM_WIDTH, 2 * SSM_WIDTH)), _const_spec((SSM_WIDTH, D_MODEL)),
                  _const_spec((GROUP_WIDTH, D_MODEL)), _const_spec((D_MODEL, D_MODEL))]
                 + _ffn_weight_specs() + [_const_spec((1, D_MODEL))],
        out_specs=rows(D_MODEL),
        out_shape=jax.ShapeDtypeStruct((n_tok, D_MODEL), F32),
        scratch_shapes=_ffn_weight_scratch(),
        compiler_params=pltpu.CompilerParams(
            dimension_semantics=("arbitrary",), vmem_limit_bytes=VMEM_LIMIT_BYTES),
        name="merge_ffn",
    )(y_gelu, o_attn, gates, x, w_glu, w_ssm, w_attn, w_out,
      g.reshape(1, D_MODEL), wg, wu, wd, final_g.reshape(1, D_MODEL))


def kernel(x, ffn1_norm, ffn1_w_gate, ffn1_w_up, ffn1_w_down, mix_norm, w_in, gate_bias,
           rel_bias_table, ssm_a_re, ssm_a_im, ssm_log_dt, ssm_b_re, ssm_b_im, ssm_c_re,
           ssm_c_im, ssm_d, ssm_w_glu, w_attn_branch, w_ssm_branch, w_out, ffn2_norm,
           ffn2_w_gate, ffn2_w_up, ffn2_w_down, final_norm):
    batch, seq, _ = x.shape
    depth = ffn1_norm.shape[0]
    h = x.reshape(batch * seq, D_MODEL)
    for l in range(depth):
        h = _ffn(h, ffn1_norm[l], ffn1_w_gate[l].astype(F32), ffn1_w_up[l].astype(F32),
                 ffn1_w_down[l].astype(F32))
        q, k, v, u, gates = _in_proj(h, mix_norm[l], w_in[l].astype(F32), gate_bias[l], batch, seq)
        o_attn = _attention(q, k, v, rel_bias_table, batch, seq)
        mats = _s5_matrices(ssm_a_re[l], ssm_a_im[l], ssm_log_dt[l], ssm_b_re[l], ssm_b_im[l],
                            ssm_c_re[l], ssm_c_im[l], ssm_d[l])
        y_gelu = _s5(u, mats, batch, seq)
        h = _merge_ffn(y_gelu, o_attn, gates, h, ssm_w_glu[l].astype(BF16),
                       w_ssm_branch[l].astype(BF16), w_attn_branch[l].astype(BF16),
                       w_out[l].astype(BF16), ffn2_norm[l], ffn2_w_gate[l].astype(F32),
                       ffn2_w_up[l].astype(F32), ffn2_w_down[l].astype(F32), final_norm,
                       final=(l == depth - 1))
    return h.reshape(batch, seq, D_MODEL)
```

```python
import functools
import math

import jax
import jax.numpy as jnp
import numpy as np
from jax import lax
from jax.experimental import pallas as pl
from jax.experimental.pallas import tpu as pltpu

F32 = jnp.float32
BF16 = jnp.bfloat16

D_MODEL = 1024
D_FF = 2816
EPS = 1e-6
HEAD_DIM = 64
HEADS_PER_GROUP = 4
GROUP_WIDTH = HEADS_PER_GROUP * HEAD_DIM
DILATIONS = (1, 4, 16)
WINDOWS = (128, 512, 2048)
N_GROUPS = len(DILATIONS)
ATTN_WIDTH = N_GROUPS * GROUP_WIDTH
BLOCK = 128
N_BUCKETS = 32
MAX_DISTANCE = 2048
NEG_INF = -1e30
SSM_GROUP = 16
SSM_WIDTH = 512
SSM_GROUPS = SSM_WIDTH // SSM_GROUP
SSM_STATE = 64
IN_WIDTH = 3 * ATTN_WIDTH + SSM_WIDTH + 2 * D_MODEL

LANES = 128
MXU_WIDTH = 256
VMEM_LIMIT_BYTES = 56 * 1024 * 1024

FFN_TILE = 1024
PROJ_TILE = 1024
MERGE_TILE = 512
WEIGHT_STAGE_STEPS = 8
ATTN_TILE = BLOCK * DILATIONS[-1]
ATTN_INTERLEAVE = 4
SCATTER_STRIDE = 4
SSM_CHUNK = 16
CHUNK_PITCH = SSM_CHUNK + 4
SSM_OCTET = LANES // SSM_GROUP
N_OCTETS = SSM_GROUPS // SSM_OCTET
OCTET_STATE = SSM_OCTET * SSM_STATE


def _rms(x, g):
    return x * lax.rsqrt(jnp.mean(x * x, axis=-1, keepdims=True) + EPS) * g


def _const_spec(shape):
    return pl.BlockSpec(shape, lambda *_: (0,) * len(shape), pipeline_mode=pl.Buffered(1))


def _swiglu_residual(x, g, wg_ref, wu_ref, wd_ref):
    h = _rms(x, g).astype(BF16)
    acc = jnp.zeros(x.shape, F32)
    for c in range(D_FF // MXU_WIDTH):
        sl = slice(c * MXU_WIDTH, (c + 1) * MXU_WIDTH)
        a = jnp.dot(h, wg_ref[:, sl], preferred_element_type=F32)
        b = jnp.dot(h, wu_ref[:, sl], preferred_element_type=F32)
        t = (a * jax.nn.sigmoid(a) * b).astype(BF16)
        acc = acc + jnp.dot(t, wd_ref[sl, :], preferred_element_type=F32)
    return x + 0.5 * acc


def _weight_copy(w_hbm, stage, sem, i):
    rows = stage.shape[1]
    return pltpu.make_async_copy(w_hbm.at[pl.ds(i * rows, rows), :], stage.at[i % 2], sem.at[i % 2])


def _fetch_cast(streams):
    steps = {w_hbm.shape[0] // stage.shape[1] for w_hbm, _, stage, _ in streams}
    (n,) = steps
    for w_hbm, _, stage, sem in streams:
        _weight_copy(w_hbm, stage, sem, 0).start()
    for i in range(n):
        for w_hbm, w_scr, stage, sem in streams:
            if i + 1 < n:
                _weight_copy(w_hbm, stage, sem, i + 1).start()
            _weight_copy(w_hbm, stage, sem, i).wait()
            rows = stage.shape[1]
            w_scr[i * rows:(i + 1) * rows, :] = stage[i % 2].astype(BF16)


def _fetch_ffn_weights(wg_hbm, wu_hbm, wd_hbm, wg_scr, wu_scr, wd_scr, gate_stage, up_stage, down_stage, sem):
    _fetch_cast([(wg_hbm, wg_scr, gate_stage, sem.at[0]),
                 (wu_hbm, wu_scr, up_stage, sem.at[1]),
                 (wd_hbm, wd_scr, down_stage, sem.at[2])])


def _ffn_kernel(x_ref, g_ref, wg_hbm, wu_hbm, wd_hbm, o_ref, *weight_scratch):
    @pl.when(pl.program_id(0) == 0)
    def _():
        _fetch_ffn_weights(wg_hbm, wu_hbm, wd_hbm, *weight_scratch)

    wg_scr, wu_scr, wd_scr = weight_scratch[0:3]
    o_ref[...] = _swiglu_residual(x_ref[...], g_ref[...], wg_scr, wu_scr, wd_scr)


def _ffn_weight_specs():
    hbm = pl.BlockSpec(memory_space=pl.ANY)
    return [_const_spec((1, D_MODEL)), hbm, hbm, hbm]


def _ffn_weight_scratch():
    return [pltpu.VMEM((D_MODEL, D_FF), BF16), pltpu.VMEM((D_MODEL, D_FF), BF16),
            pltpu.VMEM((D_FF, D_MODEL), BF16),
            pltpu.VMEM((2, D_MODEL // WEIGHT_STAGE_STEPS, D_FF), F32),
            pltpu.VMEM((2, D_MODEL // WEIGHT_STAGE_STEPS, D_FF), F32),
            pltpu.VMEM((2, D_FF // WEIGHT_STAGE_STEPS, D_MODEL), F32),
            pltpu.SemaphoreType.DMA((3, 2))]


def _ffn(x, g, wg, wu, wd):
    n_tok = x.shape[0]
    tm = FFN_TILE
    tile = pl.BlockSpec((tm, D_MODEL), lambda i: (i, 0))
    return pl.pallas_call(
        _ffn_kernel,
        grid=(n_tok // tm,),
        in_specs=[tile] + _ffn_weight_specs(),
        out_specs=tile,
        out_shape=jax.ShapeDtypeStruct((n_tok, D_MODEL), F32),
        scratch_shapes=_ffn_weight_scratch(),
        compiler_params=pltpu.CompilerParams(
            dimension_semantics=("arbitrary",), vmem_limit_bytes=VMEM_LIMIT_BYTES),
        name="ffn",
    )(x, g.reshape(1, D_MODEL), wg, wu, wd)


N_QKV_CHUNKS = 3 * N_GROUPS
N_U_CHUNKS = SSM_WIDTH // MXU_WIDTH
N_GATE_CHUNKS = 2 * D_MODEL // MXU_WIDTH
N_STRIDED = 3 * (N_GROUPS - 1)
N_TWO_HOP = 3 * sum(d > SCATTER_STRIDE for d in DILATIONS)


def _padded_rows(n_tokens):
    return n_tokens // SSM_CHUNK * CHUNK_PITCH


def _store_padded_chunks(ref, slab, rows):
    pad = jnp.zeros((CHUNK_PITCH - SSM_CHUNK, rows.shape[1]), rows.dtype)
    for ch in range(rows.shape[0] // SSM_CHUNK):
        base = ch * CHUNK_PITCH
        ref[slab, base:base + SSM_CHUNK, :] = rows[ch * SSM_CHUNK:(ch + 1) * SSM_CHUNK, :]
        ref[slab, base + SSM_CHUNK:base + CHUNK_PITCH, :] = pad


def _load_padded_chunks(ref, slab, n_tokens):
    return jnp.concatenate(
        [ref[slab, ch * CHUNK_PITCH:ch * CHUNK_PITCH + SSM_CHUNK, :]
         for ch in range(n_tokens // SSM_CHUNK)], axis=0)


def _inproj_kernel(x_ref, g_ref, w_hbm, gb_ref, *refs):
    qkv_refs = refs[:N_QKV_CHUNKS]
    u_ref, gate_ref, slab_ref, hop_ref, w_ref, w_stage, w_sem = refs[N_QKV_CHUNKS:]
    tm = x_ref.shape[0]

    @pl.when(pl.program_id(0) == 0)
    def _():
        half = D_MODEL // 2
        _fetch_cast([(w_hbm.at[pl.ds(s * half, half), :], w_ref.at[pl.ds(s * half, half), :],
                      w_stage.at[s], w_sem.at[s]) for s in range(2)])

    h = _rms(x_ref[...], g_ref[...]).astype(BF16)
    strided_idx = 0
    hop_idx = 0
    heavy = list(range(N_QKV_CHUNKS + N_U_CHUNKS))
    light = list(range(N_QKV_CHUNKS + N_U_CHUNKS, IN_WIDTH // MXU_WIDTH))
    order = []
    while heavy or light:
        if heavy:
            order.append(heavy.pop(0))
        if light:
            order.append(light.pop(0))
    for c in order:
        z = jnp.dot(h, w_ref[:, c * MXU_WIDTH:(c + 1) * MXU_WIDTH], preferred_element_type=F32)
        if c < N_QKV_CHUNKS:
            kind, grp = divmod(c, N_GROUPS)
            if kind == 0:
                z = z * (HEAD_DIM ** -0.5)
            d = DILATIONS[grp]
            o_ref = qkv_refs[c]
            if d == 1:
                o_ref[0, 0] = z.astype(BF16)
            else:
                base = 2 * strided_idx
                strided_idx += 1
                for s in range(2):
                    slab_ref[base + s] = z[:, s * LANES:(s + 1) * LANES]
                lanes = [slice(s * LANES, (s + 1) * LANES) for s in range(2)]
                if d <= SCATTER_STRIDE:
                    for r in range(d):
                        for s in range(2):
                            o_ref[0, r, :, lanes[s]] = (
                                slab_ref[base + s, pl.ds(r, tm // d, stride=d), :].astype(BF16))
                else:
                    hop = tm // SCATTER_STRIDE
                    hop_base = 2 * hop_idx
                    hop_idx += 1
                    for s in range(2):
                        for low in range(SCATTER_STRIDE):
                            hop_ref[hop_base + s, pl.ds(low * hop, hop), :] = (
                                slab_ref[base + s, pl.ds(low, hop, stride=SCATTER_STRIDE), :])
                    inner = d // SCATTER_STRIDE
                    for r in range(d):
                        low, high = r % SCATTER_STRIDE, r // SCATTER_STRIDE
                        for s in range(2):
                            o_ref[0, r, :, lanes[s]] = hop_ref[
                                hop_base + s, pl.ds(low * hop + high, tm // d, stride=inner), :
                            ].astype(BF16)
        elif c < N_QKV_CHUNKS + N_U_CHUNKS:
            j = c - N_QKV_CHUNKS
            for s in range(2):
                _store_padded_chunks(u_ref, 2 * j + s, z[:, s * LANES:(s + 1) * LANES])
        else:
            j = c - N_QKV_CHUNKS - N_U_CHUNKS
            sl = slice(j * MXU_WIDTH, (j + 1) * MXU_WIDTH)
            gate_ref[:, sl] = jax.nn.sigmoid(z + gb_ref[:, sl]).astype(BF16)


def _in_proj(x, g, w_in, gate_bias, batch, seq):
    n_tok = x.shape[0]
    tm = PROJ_TILE
    tiles_per_seq = seq // tm
    out_shapes, out_specs = [], []
    for _ in range(3):
        for d in DILATIONS:
            out_shapes.append(jax.ShapeDtypeStruct((batch, d, seq // d, GROUP_WIDTH), BF16))
            out_specs.append(pl.BlockSpec(
                (1, d, tm // d, GROUP_WIDTH),
                lambda i: (i // tiles_per_seq, 0, i % tiles_per_seq, 0)))
    out_shapes.append(jax.ShapeDtypeStruct((N_OCTETS, _padded_rows(n_tok), LANES), F32))
    out_specs.append(pl.BlockSpec((N_OCTETS, _padded_rows(tm), LANES), lambda i: (0, i, 0)))
    out_shapes.append(jax.ShapeDtypeStruct((n_tok, 2 * D_MODEL), BF16))
    out_specs.append(pl.BlockSpec((tm, 2 * D_MODEL), lambda i: (i, 0)))
    outs = pl.pallas_call(
        _inproj_kernel,
        grid=(n_tok // tm,),
        in_specs=[pl.BlockSpec((tm, D_MODEL), lambda i: (i, 0)),
                  _const_spec((1, D_MODEL)),
                  pl.BlockSpec(memory_space=pl.ANY),
                  _const_spec((1, 2 * D_MODEL))],
        out_specs=out_specs,
        out_shape=out_shapes,
        scratch_shapes=[pltpu.VMEM((2 * N_STRIDED, tm, LANES), F32),
                        pltpu.VMEM((2 * N_TWO_HOP, tm, LANES), F32),
                        pltpu.VMEM((D_MODEL, IN_WIDTH), BF16),
                        pltpu.VMEM((2, 2, D_MODEL // (2 * WEIGHT_STAGE_STEPS), IN_WIDTH), F32),
                        pltpu.SemaphoreType.DMA((2, 2))],
        compiler_params=pltpu.CompilerParams(
            dimension_semantics=("arbitrary",), vmem_limit_bytes=VMEM_LIMIT_BYTES),
        name="in_proj",
    )(x, g.reshape(1, D_MODEL), w_in, gate_bias.reshape(1, 2 * D_MODEL))
    q, k, v = outs[0:3], outs[3:6], outs[6:9]
    return q, k, v, outs[9], outs[10]


def _t5_bucket_np(dist):
    max_exact = N_BUCKETS // 2
    d = np.maximum(dist, 1).astype(np.float32)
    ratio = np.log(d / np.float32(max_exact)) / np.float32(math.log(MAX_DISTANCE / max_exact))
    large = max_exact + (ratio * np.float32(N_BUCKETS - max_exact)).astype(np.int32)
    large = np.minimum(large, N_BUCKETS - 1)
    return np.where(dist < max_exact, dist, large)


def _bucket_tables():
    qi = np.arange(BLOCK)[:, None]
    kj = np.arange(2 * BLOCK)[None, :]
    steps = qi + BLOCK - kj
    tables = []
    for window, d in zip(WINDOWS, DILATIONS):
        band = (steps >= 0) & (steps <= window // d)
        bucket = _t5_bucket_np(np.maximum(steps, 0) * d)
        tables.append(np.where(band, bucket, -1).astype(np.int32))
    return jnp.asarray(np.stack(tables))


def _attn_block(q_blk, k_blk, v_blk, bias):
    lane_head = lax.broadcasted_iota(jnp.int32, (BLOCK, GROUP_WIDTH), 1) // HEAD_DIM
    zero = jnp.zeros_like(q_blk)
    qs = jnp.concatenate(
        [jnp.where(lane_head == h, q_blk, zero) for h in range(HEADS_PER_GROUP)], axis=0)
    logits = lax.dot_general(qs, k_blk, (((1,), (1,)), ((), ())), preferred_element_type=F32) + bias
    m = jnp.max(logits, axis=-1, keepdims=True)
    p = jnp.exp(logits - m)
    l = jnp.sum(p, axis=-1, keepdims=True)
    pv = jnp.dot(p.astype(BF16), v_blk, preferred_element_type=F32)
    o = jnp.zeros((BLOCK, GROUP_WIDTH), F32)
    den = jnp.ones((BLOCK, GROUP_WIDTH), F32)
    mx = jnp.zeros((BLOCK, GROUP_WIDTH), F32)
    for h in range(HEADS_PER_GROUP):
        rows = slice(h * BLOCK, (h + 1) * BLOCK)
        sel = lane_head == h
        o = jnp.where(sel, pv[rows], o)
        den = jnp.where(sel, l[rows], den)
        mx = jnp.where(sel, m[rows], mx)
    return o / den, mx + jnp.log(den)


def _build_bias(tab_ref, bucket_ref, bias_scr):
    own_block = lax.broadcasted_iota(jnp.int32, (BLOCK, 2 * BLOCK), 1) >= BLOCK
    for grp in range(N_GROUPS):
        bucket = bucket_ref[grp]
        for h in range(HEADS_PER_GROUP):
            col = grp * HEADS_PER_GROUP + h

            def pick(b, acc, bucket=bucket, col=col):
                return jnp.where(bucket == b, tab_ref[b, col], acc)

            bias = lax.fori_loop(0, N_BUCKETS, pick, jnp.full((BLOCK, 2 * BLOCK), NEG_INF, F32))
            rows = slice(h * BLOCK, (h + 1) * BLOCK)
            bias_scr[grp, 0, rows, :] = bias
            bias_scr[grp, 1, rows, :] = jnp.where(own_block, bias, NEG_INF)


def _block_aligned(row):
    return row if isinstance(row, int) else pl.multiple_of(row, BLOCK)


def _attn_kernel(*refs):
    n = N_GROUPS
    tab_ref, bucket_ref = refs[0:2]
    refs = refs[2:]
    q_refs, kc_refs, kp_refs = refs[0:n], refs[n:2 * n], refs[2 * n:3 * n]
    vc_refs, vp_refs = refs[3 * n:4 * n], refs[4 * n:5 * n]
    o_ref, bias_scr, o_scr, l_scr, t_scr = refs[5 * n:]
    tile = pl.program_id(1)

    @pl.when((pl.program_id(0) == 0) & (tile == 0))
    def _():
        _build_bias(tab_ref, bucket_ref, bias_scr)

    for grp, d in enumerate(DILATIONS):
        q_ref, kc_ref, kp_ref = q_refs[grp], kc_refs[grp], kp_refs[grp]
        vc_ref, vp_ref = vc_refs[grp], vp_refs[grp]
        blocks_per_residue = ATTN_TILE // d // BLOCK

        def run_block(r, blk, grp=grp, d=d, q_ref=q_ref, kc_ref=kc_ref, kp_ref=kp_ref,
                      vc_ref=vc_ref, vp_ref=vp_ref):
            if isinstance(blk, int) and blk == 0:
                k_blk = jnp.concatenate([kp_ref[0, r], kc_ref[0, r, 0:BLOCK, :]], axis=0)
                v_blk = jnp.concatenate([vp_ref[0, r], vc_ref[0, r, 0:BLOCK, :]], axis=0)
                bias = jnp.where(tile == 0, bias_scr[grp, 1], bias_scr[grp, 0])
                q_blk = q_ref[0, r, 0:BLOCK, :]
            else:
                kv_rows = pl.ds(_block_aligned((blk - 1) * BLOCK), 2 * BLOCK)
                k_blk = kc_ref[0, r, kv_rows, :]
                v_blk = vc_ref[0, r, kv_rows, :]
                bias = bias_scr[grp, 0]
                q_blk = q_ref[0, r, pl.ds(_block_aligned(blk * BLOCK), BLOCK), :]
            o, lse = _attn_block(q_blk, k_blk, v_blk, bias)
            if d <= SCATTER_STRIDE:
                start = blk * (BLOCK * d) + r
                rows = pl.ds(start, BLOCK, stride=d) if d > 1 else pl.ds(start, BLOCK)
                for s in range(2):
                    o_scr[grp, s, rows, :] = o[:, s * LANES:(s + 1) * LANES]
                    l_scr[grp, s, rows, :] = lse[:, s * LANES:(s + 1) * LANES]
            else:
                inner = d // SCATTER_STRIDE
                low, high = lax.rem(r, SCATTER_STRIDE), r // SCATTER_STRIDE
                rows = pl.ds(blk * (BLOCK * inner) + high, BLOCK, stride=inner)
                for s in range(2):
                    t_scr[0, s, low, rows, :] = o[:, s * LANES:(s + 1) * LANES]
                    t_scr[1, s, low, rows, :] = lse[:, s * LANES:(s + 1) * LANES]

        if blocks_per_residue == 1:
            def residue_set(i, c, run_block=run_block):
                for k in range(ATTN_INTERLEAVE):
                    run_block(ATTN_INTERLEAVE * i + k, 0)
                return c
            lax.fori_loop(0, d // ATTN_INTERLEAVE, residue_set, 0)
        elif d > 1:
            def residue(r, c, run_block=run_block, blocks_per_residue=blocks_per_residue):
                run_block(r, 0)
                for blk in range(1, blocks_per_residue):
                    run_block(r, blk)
                return c
            lax.fori_loop(0, d, residue, 0)
        else:
            for blk in range(ATTN_INTERLEAVE):
                run_block(0, blk)

            def block_set(i, c, run_block=run_block):
                for k in range(ATTN_INTERLEAVE):
                    run_block(0, ATTN_INTERLEAVE * i + k)
                return c
            lax.fori_loop(1, blocks_per_residue // ATTN_INTERLEAVE, block_set, 0)

        if d > SCATTER_STRIDE:
            def unstage(low, c, grp=grp):
                rows = pl.ds(low, ATTN_TILE // SCATTER_STRIDE, stride=SCATTER_STRIDE)
                for s in range(2):
                    o_scr[grp, s, rows, :] = t_scr[0, s, low]
                    l_scr[grp, s, rows, :] = t_scr[1, s, low]
                return c
            lax.fori_loop(0, SCATTER_STRIDE, unstage, 0)

    merge_rows = 256

    def merge(c, carry):
        rows = pl.ds(pl.multiple_of(c * merge_rows, merge_rows), merge_rows)
        for s in range(2):
            lses = [l_scr[grp, s, rows, :] for grp in range(N_GROUPS)]
            mx = functools.reduce(jnp.maximum, lses)
            es = [jnp.exp(x - mx) for x in lses]
            num = sum(e * o_scr[grp, s, rows, :] for grp, e in enumerate(es))
            o_ref[rows, s * LANES:(s + 1) * LANES] = (num / sum(es)).astype(BF16)
        return carry

    lax.fori_loop(0, ATTN_TILE // merge_rows, merge, 0)


def _attention(q, k, v, rel_bias_table, batch, seq):
    tiles = seq // ATTN_TILE

    def cur_spec(d):
        return pl.BlockSpec((1, d, ATTN_TILE // d, GROUP_WIDTH), lambda b, j: (b, 0, j, 0))

    def prev_spec(d):
        per_tile = ATTN_TILE // d // BLOCK
        return pl.BlockSpec((1, d, BLOCK, GROUP_WIDTH),
                            lambda b, j: (b, 0, jnp.maximum(j * per_tile - 1, 0), 0))

    in_specs = ([pl.BlockSpec(memory_space=pltpu.SMEM),
                 _const_spec((N_GROUPS, BLOCK, 2 * BLOCK))]
                + [cur_spec(d) for d in DILATIONS] + [cur_spec(d) for d in DILATIONS]
                + [prev_spec(d) for d in DILATIONS] + [cur_spec(d) for d in DILATIONS]
                + [prev_spec(d) for d in DILATIONS])
    return pl.pallas_call(
        _attn_kernel,
        grid=(batch, tiles),
        in_specs=in_specs,
        out_specs=pl.BlockSpec((ATTN_TILE, GROUP_WIDTH), lambda b, j: (b * tiles + j, 0)),
        out_shape=jax.ShapeDtypeStruct((batch * seq, GROUP_WIDTH), BF16),
        scratch_shapes=[pltpu.VMEM((N_GROUPS, 2, HEADS_PER_GROUP * BLOCK, 2 * BLOCK), F32),
                        pltpu.VMEM((N_GROUPS, 2, ATTN_TILE, LANES), F32),
                        pltpu.VMEM((N_GROUPS, 2, ATTN_TILE, LANES), F32),
                        pltpu.VMEM((2, 2, SCATTER_STRIDE, ATTN_TILE // SCATTER_STRIDE, LANES), F32)],
        compiler_params=pltpu.CompilerParams(
            dimension_semantics=("arbitrary", "arbitrary"), vmem_limit_bytes=VMEM_LIMIT_BYTES),
        name="dilated_attn",
    )(rel_bias_table.astype(F32), _bucket_tables(), *q, *k, *k, *v, *v)


def _octet_mask(row_group, col_group, rows, cols):
    r = np.arange(rows)[:, None] // row_group
    c = np.arange(cols)[None, :] // col_group
    return jnp.asarray(r == c)


def _s5_matrices(a_re, a_im, log_dt, b_re, b_im, c_re, c_im, d_skip):
    hi = lax.Precision.HIGHEST
    tc = SSM_CHUNK
    lam_re = a_re.astype(F32)
    lam_im = a_im.astype(F32)
    dt = jnp.exp(log_dt.astype(F32))[:, None]
    mag = jnp.exp(lam_re * dt)
    ab_re = mag * jnp.cos(lam_im * dt)
    ab_im = mag * jnp.sin(lam_im * dt)
    den = lam_re * lam_re + lam_im * lam_im
    xr = ab_re - 1.0
    coef_re = (xr * lam_re + ab_im * lam_im) / den
    coef_im = (ab_im * lam_re - xr * lam_im) / den
    br = b_re.astype(F32)
    bi = b_im.astype(F32)
    bb_re = coef_re[..., None] * br - coef_im[..., None] * bi
    bb_im = coef_re[..., None] * bi + coef_im[..., None] * br
    cr = c_re.astype(F32)
    ci = c_im.astype(F32)

    def a_pow(k):
        kk = k.astype(F32)[:, None, None]
        pm = jnp.exp(lam_re * dt * kk)
        return pm * jnp.cos(lam_im * dt * kk), pm * jnp.sin(lam_im * dt * kk)

    pw_re, pw_im = a_pow(jnp.arange(tc + 1))
    pr, pi = pw_re[:tc], pw_im[:tc]
    abr = pr[..., None] * bb_re - pi[..., None] * bb_im
    abi = pr[..., None] * bb_im + pi[..., None] * bb_re
    kern = (jnp.einsum('gon,lgnc->lgoc', cr, abr, precision=hi)
            - jnp.einsum('gon,lgnc->lgoc', ci, abi, precision=hi))
    skip = d_skip.astype(F32).reshape(SSM_GROUPS, SSM_GROUP)
    kern = kern.at[0].add(skip[:, :, None] * jnp.eye(SSM_GROUP, dtype=F32))

    kt = jnp.transpose(kern, (0, 1, 3, 2)).reshape(tc, N_OCTETS, LANES, SSM_GROUP)
    kt = jnp.transpose(kt, (1, 0, 2, 3))
    kblk = jnp.where(_octet_mask(SSM_GROUP, SSM_GROUP, LANES, LANES),
                     jnp.tile(kt, (1, 1, 1, SSM_OCTET)), 0.0)

    def by_octet(x):
        return x.reshape(x.shape[0], N_OCTETS, 1, SSM_OCTET, SSM_STATE)

    def channel_major(x):
        x = x.reshape(N_OCTETS, SSM_OCTET, SSM_GROUP, SSM_STATE)
        return jnp.transpose(x, (0, 2, 1, 3))[None]

    def compact(re, im):
        both = jnp.stack([re, im], axis=1)
        return both.reshape(tc, 2, N_OCTETS, SSM_GROUP, OCTET_STATE).astype(BF16)

    lr, li = by_octet(pr[::-1]), by_octet(pi[::-1])
    bt_re = channel_major(jnp.transpose(bb_re, (0, 2, 1)))
    bt_im = channel_major(jnp.transpose(bb_im, (0, 2, 1)))
    p_src = compact(lr * bt_re - li * bt_im, lr * bt_im + li * bt_re)

    qr, qi = by_octet(pw_re[1:]), by_octet(pw_im[1:])
    ct_re, ct_im = channel_major(cr), channel_major(ci)
    q_src = compact(ct_re * qr - ct_im * qi, -(ct_re * qi + ct_im * qr))

    a_chunk = jnp.concatenate([pw_re[tc].reshape(N_OCTETS, 1, OCTET_STATE),
                               pw_im[tc].reshape(N_OCTETS, 1, OCTET_STATE)], axis=-1)
    return kblk.astype(BF16), p_src, q_src, a_chunk


def _s5_kernel(u_ref, kblk_ref, p_ref, q_ref, a_ref, o_ref, k_scr, p_scr, qt_scr, x_scr, s_scr, z_scr):
    n_chunks = x_scr.shape[0]
    tc = SSM_CHUNK

    @pl.when(pl.program_id(1) == 0)
    def _():
        k_scr[...] = jnp.zeros(k_scr.shape, BF16)
        for s in range(tc):
            for t in range(s, tc):
                k_scr[s * LANES:(s + 1) * LANES, t * LANES:(t + 1) * LANES] = kblk_ref[0, t - s]
        row_group = lax.broadcasted_iota(jnp.int32, (LANES, OCTET_STATE), 0) // SSM_GROUP
        lane_group = lax.broadcasted_iota(jnp.int32, (LANES, OCTET_STATE), 1) // SSM_STATE
        same_group = row_group == lane_group
        zeros = jnp.zeros((LANES, OCTET_STATE), BF16)
        for src_ref, dst in ((p_ref, p_scr), (q_ref, qt_scr)):
            for t in range(tc):
                for part in range(2):
                    src = src_ref[t, part, 0]
                    tiled = jnp.concatenate([src] * SSM_OCTET, axis=0)
                    dst[t * LANES:(t + 1) * LANES, part * OCTET_STATE:(part + 1) * OCTET_STATE] = (
                        jnp.where(same_group, tiled, zeros))

    inc = None
    for pair in range(tc // 2):
        for t in (2 * pair, 2 * pair + 1):
            x_scr[:, t * LANES:(t + 1) * LANES] = (
                u_ref[0, pl.ds(t, n_chunks, stride=CHUNK_PITCH), :].astype(BF16))
        rows = slice(pair * MXU_WIDTH, (pair + 1) * MXU_WIDTH)
        part = jnp.dot(x_scr[:, rows], p_scr[rows, :], preferred_element_type=F32)
        inc = part if inc is None else inc + part
    s_scr[...] = inc

    a_re = a_ref[0, :, 0:OCTET_STATE]
    a_im = a_ref[0, :, OCTET_STATE:2 * OCTET_STATE]

    def step(j, carry):
        c_re, c_im = carry
        row = pl.ds(j, 1)
        z_scr[row, 0:OCTET_STATE] = c_re
        z_scr[row, OCTET_STATE:2 * OCTET_STATE] = c_im
        inc_re = s_scr[row, 0:OCTET_STATE]
        inc_im = s_scr[row, OCTET_STATE:2 * OCTET_STATE]
        return (a_re * c_re - a_im * c_im + inc_re, a_re * c_im + a_im * c_re + inc_im)

    zero = jnp.zeros((1, OCTET_STATE), F32)
    lax.fori_loop(0, n_chunks, step, (zero, zero), unroll=8)

    zb = z_scr[...].astype(BF16)
    for pair in range(tc // 2):
        cols = slice(pair * MXU_WIDTH, (pair + 1) * MXU_WIDTH)
        live = (2 * pair + 2) * LANES
        y = (jnp.dot(x_scr[:, 0:live], k_scr[0:live, cols], preferred_element_type=F32)
             + lax.dot_general(zb, qt_scr[cols, :], (((1,), (1,)), ((), ())),
                               preferred_element_type=F32))
        y = jax.nn.gelu(y)
        o_ref[0, pl.ds(2 * pair, n_chunks, stride=CHUNK_PITCH), :] = y[:, 0:LANES]
        o_ref[0, pl.ds(2 * pair + 1, n_chunks, stride=CHUNK_PITCH), :] = y[:, LANES:2 * LANES]
    for t in range(tc, CHUNK_PITCH):
        o_ref[0, pl.ds(t, n_chunks, stride=CHUNK_PITCH), :] = jnp.zeros((n_chunks, LANES), F32)


def _s5(u, mats, batch, seq):
    kblk, p_src, q_src, a_chunk = mats
    n_chunks = seq // SSM_CHUNK
    width = SSM_CHUNK * LANES

    def per_octet(shape):
        return pl.BlockSpec((1,) + shape, lambda o, b: (o,) + (0,) * len(shape),
                            pipeline_mode=pl.Buffered(1))

    compact_spec = pl.BlockSpec((SSM_CHUNK, 2, 1, SSM_GROUP, OCTET_STATE),
                                lambda o, b: (0, 0, o, 0, 0))
    return pl.pallas_call(
        _s5_kernel,
        grid=(N_OCTETS, batch),
        in_specs=[pl.BlockSpec((1, _padded_rows(seq), LANES), lambda o, b: (o, b, 0)),
                  per_octet((SSM_CHUNK, LANES, LANES)),
                  compact_spec,
                  compact_spec,
                  per_octet((1, 2 * OCTET_STATE))],
        out_specs=pl.BlockSpec((1, _padded_rows(seq), LANES), lambda o, b: (o, b, 0)),
        out_shape=jax.ShapeDtypeStruct((N_OCTETS, _padded_rows(batch * seq), LANES), F32),
        scratch_shapes=[pltpu.VMEM((width, width), BF16),
                        pltpu.VMEM((width, 2 * OCTET_STATE), BF16),
                        pltpu.VMEM((width, 2 * OCTET_STATE), BF16),
                        pltpu.VMEM((n_chunks, width), BF16),
                        pltpu.VMEM((n_chunks, 2 * OCTET_STATE), F32),
                        pltpu.VMEM((n_chunks, 2 * OCTET_STATE), F32)],
        compiler_params=pltpu.CompilerParams(
            dimension_semantics=("arbitrary", "arbitrary"), vmem_limit_bytes=VMEM_LIMIT_BYTES),
        name="s5_mixer",
    )(u, kblk, p_src, q_src, a_chunk)


def _merge_ffn_kernel(y_ref, oa_ref, gate_ref, x_ref, wglu_ref, wssm_ref, wattn_ref, wout_ref,
                      g_ref, wg_hbm, wu_hbm, wd_hbm, fn_ref, o_ref, *weight_scratch, final):
    @pl.when(pl.program_id(0) == 0)
    def _():
        _fetch_ffn_weights(wg_hbm, wu_hbm, wd_hbm, *weight_scratch)

    wg_scr, wu_scr, wd_scr = weight_scratch[0:3]
    y_gelu = jnp.concatenate(
        [_load_padded_chunks(y_ref, o, x_ref.shape[0]).astype(BF16) for o in range(N_OCTETS)], axis=1)
    glu = jnp.dot(y_gelu, wglu_ref[...], preferred_element_type=F32)
    ys = (glu[:, 0:SSM_WIDTH] * jax.nn.sigmoid(glu[:, SSM_WIDTH:2 * SSM_WIDTH])).astype(BF16)
    y_ssm = jnp.dot(ys, wssm_ref[...], preferred_element_type=F32)
    y_attn = jnp.dot(oa_ref[...], wattn_ref[...], preferred_element_type=F32)
    g_attn = gate_ref[:, 0:D_MODEL].astype(F32)
    g_ssm = gate_ref[:, D_MODEL:2 * D_MODEL].astype(F32)
    mixed = (g_attn * y_attn + g_ssm * y_ssm).astype(BF16)
    x = x_ref[...] + jnp.dot(mixed, wout_ref[...], preferred_element_type=F32)
    y = _swiglu_residual(x, g_ref[...], wg_scr, wu_scr, wd_scr)
    o_ref[...] = _rms(y, fn_ref[...]) if final else y


def _merge_ffn(y_gelu, o_attn, gates, x, w_glu, w_ssm, w_attn, w_out, g, wg, wu, wd, final_g, final):
    n_tok = x.shape[0]
    tm = MERGE_TILE

    def rows(width):
        return pl.BlockSpec((tm, width), lambda i: (i, 0))

    return pl.pallas_call(
        functools.partial(_merge_ffn_kernel, final=final),
        grid=(n_tok // tm,),
        in_specs=[pl.BlockSpec((N_OCTETS, _padded_rows(tm), LANES), lambda i: (0, i, 0)),
                  rows(GROUP_WIDTH), rows(2 * D_MODEL), rows(D_MODEL),
                  _const_spec((SSM_WIDTH, 2 * SSM_WIDTH)), _const_spec((SSM_WIDTH, D_MODEL)),
                  _const_spec((GROUP_WIDTH, D_MODEL)), _const_spec((D_MODEL, D_MODEL))]
                 + _ffn_weight_specs() + [_const_spec((1, D_MODEL))],
        out_specs=rows(D_MODEL),
        out_shape=jax.ShapeDtypeStruct((n_tok, D_MODEL), F32),
        scratch_shapes=_ffn_weight_scratch(),
        compiler_params=pltpu.CompilerParams(
            dimension_semantics=("arbitrary",), vmem_limit_bytes=VMEM_LIMIT_BYTES),
        name="merge_ffn",
    )(y_gelu, o_attn, gates, x, w_glu, w_ssm, w_attn, w_out,
      g.reshape(1, D_MODEL), wg, wu, wd, final_g.reshape(1, D_MODEL))


def kernel(x, ffn1_norm, ffn1_w_gate, ffn1_w_up, ffn1_w_down, mix_norm, w_in, gate_bias,
           rel_bias_table, ssm_a_re, ssm_a_im, ssm_log_dt, ssm_b_re, ssm_b_im, ssm_c_re,
           ssm_c_im, ssm_d, ssm_w_glu, w_attn_branch, w_ssm_branch, w_out, ffn2_norm,
           ffn2_w_gate, ffn2_w_up, ffn2_w_down, final_norm):
    batch, seq, _ = x.shape
    depth = ffn1_norm.shape[0]
    h = x.reshape(batch * seq, D_MODEL)
    for l in range(depth):
        h = _ffn(h, ffn1_norm[l], ffn1_w_gate[l].astype(F32), ffn1_w_up[l].astype(F32),
                 ffn1_w_down[l].astype(F32))
        q, k, v, u, gates = _in_proj(h, mix_norm[l], w_in[l].astype(F32), gate_bias[l], batch, seq)
        o_attn = _attention(q, k, v, rel_bias_table, batch, seq)
        mats = _s5_matrices(ssm_a_re[l], ssm_a_im[l], ssm_log_dt[l], ssm_b_re[l], ssm_b_im[l],
                            ssm_c_re[l], ssm_c_im[l], ssm_d[l])
        y_gelu = _s5(u, mats, batch, seq)
        h = _merge_ffn(y_gelu, o_attn, gates, h, ssm_w_glu[l].astype(BF16),
                       w_ssm_branch[l].astype(BF16), w_attn_branch[l].astype(BF16),
                       w_out[l].astype(BF16), ffn2_norm[l], ffn2_w_gate[l].astype(F32),
                       ffn2_w_up[l].astype(F32), ffn2_w_down[l].astype(F32), final_norm,
                       final=(l == depth - 1))
    return h.reshape(batch, seq, D_MODEL)
```

```python
import functools
import math

import jax
import jax.numpy as jnp
import numpy as np
from jax import lax
from jax.experimental import pallas as pl
from jax.experimental.pallas import tpu as pltpu

F32 = jnp.float32
BF16 = jnp.bfloat16

D_MODEL = 1024
D_FF = 2816
EPS = 1e-6
HEAD_DIM = 64
HEADS_PER_GROUP = 4
GROUP_WIDTH = HEADS_PER_GROUP * HEAD_DIM
DILATIONS = (1, 4, 16)
WINDOWS = (128, 512, 2048)
N_GROUPS = len(DILATIONS)
ATTN_WIDTH = N_GROUPS * GROUP_WIDTH
BLOCK = 128
N_BUCKETS = 32
MAX_DISTANCE = 2048
NEG_INF = -1e30
SSM_GROUP = 16
SSM_WIDTH = 512
SSM_GROUPS = SSM_WIDTH // SSM_GROUP
SSM_STATE = 64
IN_WIDTH = 3 * ATTN_WIDTH + SSM_WIDTH + 2 * D_MODEL

LANES = 128
MXU_WIDTH = 256
VMEM_LIMIT_BYTES = 56 * 1024 * 1024

FFN_TILE = 1024
PROJ_TILE = 1024
MERGE_TILE = 512
WEIGHT_STAGE_STEPS = 8
ATTN_TILE = BLOCK * DILATIONS[-1]
ATTN_INTERLEAVE = 8
SCATTER_STRIDE = 4
SSM_CHUNK = 16
CHUNK_PITCH = SSM_CHUNK + 4
SSM_OCTET = LANES // SSM_GROUP
N_OCTETS = SSM_GROUPS // SSM_OCTET
OCTET_STATE = SSM_OCTET * SSM_STATE


def _rms(x, g):
    return x * lax.rsqrt(jnp.mean(x * x, axis=-1, keepdims=True) + EPS) * g


def _const_spec(shape):
    return pl.BlockSpec(shape, lambda *_: (0,) * len(shape), pipeline_mode=pl.Buffered(1))


def _swiglu_residual(x, g, wg_ref, wu_ref, wd_ref):
    h = _rms(x, g).astype(BF16)
    acc = jnp.zeros(x.shape, F32)
    for c in range(D_FF // MXU_WIDTH):
        sl = slice(c * MXU_WIDTH, (c + 1) * MXU_WIDTH)
        a = jnp.dot(h, wg_ref[:, sl], preferred_element_type=F32)
        b = jnp.dot(h, wu_ref[:, sl], preferred_element_type=F32)
        t = (a * jax.nn.sigmoid(a) * b).astype(BF16)
        acc = acc + jnp.dot(t, wd_ref[sl, :], preferred_element_type=F32)
    return x + 0.5 * acc


def _weight_copy(w_hbm, stage, sem, i):
    rows = stage.shape[1]
    return pltpu.make_async_copy(w_hbm.at[pl.ds(i * rows, rows), :], stage.at[i % 2], sem.at[i % 2])


def _fetch_cast(streams):
    steps = {w_hbm.shape[0] // stage.shape[1] for w_hbm, _, stage, _ in streams}
    (n,) = steps
    for w_hbm, _, stage, sem in streams:
        _weight_copy(w_hbm, stage, sem, 0).start()
    for i in range(n):
        for w_hbm, w_scr, stage, sem in streams:
            if i + 1 < n:
                _weight_copy(w_hbm, stage, sem, i + 1).start()
            _weight_copy(w_hbm, stage, sem, i).wait()
            rows = stage.shape[1]
            w_scr[i * rows:(i + 1) * rows, :] = stage[i % 2].astype(BF16)


def _fetch_ffn_weights(wg_hbm, wu_hbm, wd_hbm, wg_scr, wu_scr, wd_scr, gate_stage, up_stage, down_stage, sem):
    _fetch_cast([(wg_hbm, wg_scr, gate_stage, sem.at[0]),
                 (wu_hbm, wu_scr, up_stage, sem.at[1]),
                 (wd_hbm, wd_scr, down_stage, sem.at[2])])


def _ffn_kernel(x_ref, g_ref, wg_hbm, wu_hbm, wd_hbm, o_ref, *weight_scratch):
    @pl.when(pl.program_id(0) == 0)
    def _():
        _fetch_ffn_weights(wg_hbm, wu_hbm, wd_hbm, *weight_scratch)

    wg_scr, wu_scr, wd_scr = weight_scratch[0:3]
    o_ref[...] = _swiglu_residual(x_ref[...], g_ref[...], wg_scr, wu_scr, wd_scr)


def _ffn_weight_specs():
    hbm = pl.BlockSpec(memory_space=pl.ANY)
    return [_const_spec((1, D_MODEL)), hbm, hbm, hbm]


def _ffn_weight_scratch():
    return [pltpu.VMEM((D_MODEL, D_FF), BF16), pltpu.VMEM((D_MODEL, D_FF), BF16),
            pltpu.VMEM((D_FF, D_MODEL), BF16),
            pltpu.VMEM((2, D_MODEL // WEIGHT_STAGE_STEPS, D_FF), F32),
            pltpu.VMEM((2, D_MODEL // WEIGHT_STAGE_STEPS, D_FF), F32),
            pltpu.VMEM((2, D_FF // WEIGHT_STAGE_STEPS, D_MODEL), F32),
            pltpu.SemaphoreType.DMA((3, 2))]


def _ffn(x, g, wg, wu, wd):
    n_tok = x.shape[0]
    tm = FFN_TILE
    tile = pl.BlockSpec((tm, D_MODEL), lambda i: (i, 0))
    return pl.pallas_call(
        _ffn_kernel,
        grid=(n_tok // tm,),
        in_specs=[tile] + _ffn_weight_specs(),
        out_specs=tile,
        out_shape=jax.ShapeDtypeStruct((n_tok, D_MODEL), F32),
        scratch_shapes=_ffn_weight_scratch(),
        compiler_params=pltpu.CompilerParams(
            dimension_semantics=("arbitrary",), vmem_limit_bytes=VMEM_LIMIT_BYTES),
        name="ffn",
    )(x, g.reshape(1, D_MODEL), wg, wu, wd)


N_QKV_CHUNKS = 3 * N_GROUPS
N_U_CHUNKS = SSM_WIDTH // MXU_WIDTH
N_GATE_CHUNKS = 2 * D_MODEL // MXU_WIDTH
N_STRIDED = 3 * (N_GROUPS - 1)
N_TWO_HOP = 3 * sum(d > SCATTER_STRIDE for d in DILATIONS)


def _padded_rows(n_tokens):
    return n_tokens // SSM_CHUNK * CHUNK_PITCH


def _store_padded_chunks(ref, slab, rows):
    pad = jnp.zeros((CHUNK_PITCH - SSM_CHUNK, rows.shape[1]), rows.dtype)
    for ch in range(rows.shape[0] // SSM_CHUNK):
        base = ch * CHUNK_PITCH
        ref[slab, base:base + SSM_CHUNK, :] = rows[ch * SSM_CHUNK:(ch + 1) * SSM_CHUNK, :]
        ref[slab, base + SSM_CHUNK:base + CHUNK_PITCH, :] = pad


def _load_padded_chunks(ref, slab, n_tokens):
    return jnp.concatenate(
        [ref[slab, ch * CHUNK_PITCH:ch * CHUNK_PITCH + SSM_CHUNK, :]
         for ch in range(n_tokens // SSM_CHUNK)], axis=0)


def _inproj_kernel(x_ref, g_ref, w_hbm, gb_ref, *refs):
    qkv_refs = refs[:N_QKV_CHUNKS]
    u_ref, gate_ref, slab_ref, hop_ref, w_ref, w_stage, w_sem = refs[N_QKV_CHUNKS:]
    tm = x_ref.shape[0]

    @pl.when(pl.program_id(0) == 0)
    def _():
        half = D_MODEL // 2
        _fetch_cast([(w_hbm.at[pl.ds(s * half, half), :], w_ref.at[pl.ds(s * half, half), :],
                      w_stage.at[s], w_sem.at[s]) for s in range(2)])

    h = _rms(x_ref[...], g_ref[...]).astype(BF16)
    strided_idx = 0
    hop_idx = 0
    heavy = list(range(N_QKV_CHUNKS + N_U_CHUNKS))
    light = list(range(N_QKV_CHUNKS + N_U_CHUNKS, IN_WIDTH // MXU_WIDTH))
    order = []
    while heavy or light:
        if heavy:
            order.append(heavy.pop(0))
        if light:
            order.append(light.pop(0))
    for c in order:
        z = jnp.dot(h, w_ref[:, c * MXU_WIDTH:(c + 1) * MXU_WIDTH], preferred_element_type=F32)
        if c < N_QKV_CHUNKS:
            kind, grp = divmod(c, N_GROUPS)
            if kind == 0:
                z = z * (HEAD_DIM ** -0.5)
            d = DILATIONS[grp]
            o_ref = qkv_refs[c]
            if d == 1:
                o_ref[0, 0] = z.astype(BF16)
            else:
                base = 2 * strided_idx
                strided_idx += 1
                for s in range(2):
                    slab_ref[base + s] = z[:, s * LANES:(s + 1) * LANES]
                lanes = [slice(s * LANES, (s + 1) * LANES) for s in range(2)]
                if d <= SCATTER_STRIDE:
                    for r in range(d):
                        for s in range(2):
                            o_ref[0, r, :, lanes[s]] = (
                                slab_ref[base + s, pl.ds(r, tm // d, stride=d), :].astype(BF16))
                else:
                    hop = tm // SCATTER_STRIDE
                    hop_base = 2 * hop_idx
                    hop_idx += 1
                    for s in range(2):
                        for low in range(SCATTER_STRIDE):
                            hop_ref[hop_base + s, pl.ds(low * hop, hop), :] = (
                                slab_ref[base + s, pl.ds(low, hop, stride=SCATTER_STRIDE), :])
                    inner = d // SCATTER_STRIDE
                    for r in range(d):
                        low, high = r % SCATTER_STRIDE, r // SCATTER_STRIDE
                        for s in range(2):
                            o_ref[0, r, :, lanes[s]] = hop_ref[
                                hop_base + s, pl.ds(low * hop + high, tm // d, stride=inner), :
                            ].astype(BF16)
        elif c < N_QKV_CHUNKS + N_U_CHUNKS:
            j = c - N_QKV_CHUNKS
            for s in range(2):
                _store_padded_chunks(u_ref, 2 * j + s, z[:, s * LANES:(s + 1) * LANES])
        else:
            j = c - N_QKV_CHUNKS - N_U_CHUNKS
            sl = slice(j * MXU_WIDTH, (j + 1) * MXU_WIDTH)
            gate_ref[:, sl] = jax.nn.sigmoid(z + gb_ref[:, sl]).astype(BF16)


def _in_proj(x, g, w_in, gate_bias, batch, seq):
    n_tok = x.shape[0]
    tm = PROJ_TILE
    tiles_per_seq = seq // tm
    out_shapes, out_specs = [], []
    for _ in range(3):
        for d in DILATIONS:
            out_shapes.append(jax.ShapeDtypeStruct((batch, d, seq // d, GROUP_WIDTH), BF16))
            out_specs.append(pl.BlockSpec(
                (1, d, tm // d, GROUP_WIDTH),
                lambda i: (i // tiles_per_seq, 0, i % tiles_per_seq, 0)))
    out_shapes.append(jax.ShapeDtypeStruct((N_OCTETS, _padded_rows(n_tok), LANES), F32))
    out_specs.append(pl.BlockSpec((N_OCTETS, _padded_rows(tm), LANES), lambda i: (0, i, 0)))
    out_shapes.append(jax.ShapeDtypeStruct((n_tok, 2 * D_MODEL), BF16))
    out_specs.append(pl.BlockSpec((tm, 2 * D_MODEL), lambda i: (i, 0)))
    outs = pl.pallas_call(
        _inproj_kernel,
        grid=(n_tok // tm,),
        in_specs=[pl.BlockSpec((tm, D_MODEL), lambda i: (i, 0)),
                  _const_spec((1, D_MODEL)),
                  pl.BlockSpec(memory_space=pl.ANY),
                  _const_spec((1, 2 * D_MODEL))],
        out_specs=out_specs,
        out_shape=out_shapes,
        scratch_shapes=[pltpu.VMEM((2 * N_STRIDED, tm, LANES), F32),
                        pltpu.VMEM((2 * N_TWO_HOP, tm, LANES), F32),
                        pltpu.VMEM((D_MODEL, IN_WIDTH), BF16),
                        pltpu.VMEM((2, 2, D_MODEL // (2 * WEIGHT_STAGE_STEPS), IN_WIDTH), F32),
                        pltpu.SemaphoreType.DMA((2, 2))],
        compiler_params=pltpu.CompilerParams(
            dimension_semantics=("arbitrary",), vmem_limit_bytes=VMEM_LIMIT_BYTES),
        name="in_proj",
    )(x, g.reshape(1, D_MODEL), w_in, gate_bias.reshape(1, 2 * D_MODEL))
    q, k, v = outs[0:3], outs[3:6], outs[6:9]
    return q, k, v, outs[9], outs[10]


def _t5_bucket_np(dist):
    max_exact = N_BUCKETS // 2
    d = np.maximum(dist, 1).astype(np.float32)
    ratio = np.log(d / np.float32(max_exact)) / np.float32(math.log(MAX_DISTANCE / max_exact))
    large = max_exact + (ratio * np.float32(N_BUCKETS - max_exact)).astype(np.int32)
    large = np.minimum(large, N_BUCKETS - 1)
    return np.where(dist < max_exact, dist, large)


def _bucket_tables():
    qi = np.arange(BLOCK)[:, None]
    kj = np.arange(2 * BLOCK)[None, :]
    steps = qi + BLOCK - kj
    tables = []
    for window, d in zip(WINDOWS, DILATIONS):
        band = (steps >= 0) & (steps <= window // d)
        bucket = _t5_bucket_np(np.maximum(steps, 0) * d)
        tables.append(np.where(band, bucket, -1).astype(np.int32))
    return jnp.asarray(np.stack(tables))


def _attn_block(q_blk, k_blk, v_blk, bias):
    lane_head = lax.broadcasted_iota(jnp.int32, (BLOCK, GROUP_WIDTH), 1) // HEAD_DIM
    zero = jnp.zeros_like(q_blk)
    qs = jnp.concatenate(
        [jnp.where(lane_head == h, q_blk, zero) for h in range(HEADS_PER_GROUP)], axis=0)
    logits = lax.dot_general(qs, k_blk, (((1,), (1,)), ((), ())), preferred_element_type=F32) + bias
    m = jnp.max(logits, axis=-1, keepdims=True)
    p = jnp.exp(logits - m)
    l = jnp.sum(p, axis=-1, keepdims=True)
    pv = jnp.dot(p.astype(BF16), v_blk, preferred_element_type=F32)
    o = jnp.zeros((BLOCK, GROUP_WIDTH), F32)
    den = jnp.ones((BLOCK, GROUP_WIDTH), F32)
    mx = jnp.zeros((BLOCK, GROUP_WIDTH), F32)
    for h in range(HEADS_PER_GROUP):
        rows = slice(h * BLOCK, (h + 1) * BLOCK)
        sel = lane_head == h
        o = jnp.where(sel, pv[rows], o)
        den = jnp.where(sel, l[rows], den)
        mx = jnp.where(sel, m[rows], mx)
    return o / den, mx + jnp.log(den)


def _build_bias(tab_ref, bucket_ref, bias_scr):
    own_block = lax.broadcasted_iota(jnp.int32, (BLOCK, 2 * BLOCK), 1) >= BLOCK
    for grp in range(N_GROUPS):
        bucket = bucket_ref[grp]
        for h in range(HEADS_PER_GROUP):
            col = grp * HEADS_PER_GROUP + h

            def pick(b, acc, bucket=bucket, col=col):
                return jnp.where(bucket == b, tab_ref[b, col], acc)

            bias = lax.fori_loop(0, N_BUCKETS, pick, jnp.full((BLOCK, 2 * BLOCK), NEG_INF, F32))
            rows = slice(h * BLOCK, (h + 1) * BLOCK)
            bias_scr[grp, 0, rows, :] = bias
            bias_scr[grp, 1, rows, :] = jnp.where(own_block, bias, NEG_INF)


def _block_aligned(row):
    return row if isinstance(row, int) else pl.multiple_of(row, BLOCK)


def _attn_kernel(*refs):
    n = N_GROUPS
    tab_ref, bucket_ref = refs[0:2]
    refs = refs[2:]
    q_refs, kc_refs, kp_refs = refs[0:n], refs[n:2 * n], refs[2 * n:3 * n]
    vc_refs, vp_refs = refs[3 * n:4 * n], refs[4 * n:5 * n]
    o_ref, bias_scr, o_scr, l_scr, t_scr = refs[5 * n:]
    tile = pl.program_id(1)

    @pl.when((pl.program_id(0) == 0) & (tile == 0))
    def _():
        _build_bias(tab_ref, bucket_ref, bias_scr)

    for grp, d in enumerate(DILATIONS):
        q_ref, kc_ref, kp_ref = q_refs[grp], kc_refs[grp], kp_refs[grp]
        vc_ref, vp_ref = vc_refs[grp], vp_refs[grp]
        blocks_per_residue = ATTN_TILE // d // BLOCK

        def run_block(r, blk, grp=grp, d=d, q_ref=q_ref, kc_ref=kc_ref, kp_ref=kp_ref,
                      vc_ref=vc_ref, vp_ref=vp_ref):
            if isinstance(blk, int) and blk == 0:
                k_blk = jnp.concatenate([kp_ref[0, r], kc_ref[0, r, 0:BLOCK, :]], axis=0)
                v_blk = jnp.concatenate([vp_ref[0, r], vc_ref[0, r, 0:BLOCK, :]], axis=0)
                bias = jnp.where(tile == 0, bias_scr[grp, 1], bias_scr[grp, 0])
                q_blk = q_ref[0, r, 0:BLOCK, :]
            else:
                kv_rows = pl.ds(_block_aligned((blk - 1) * BLOCK), 2 * BLOCK)
                k_blk = kc_ref[0, r, kv_rows, :]
                v_blk = vc_ref[0, r, kv_rows, :]
                bias = bias_scr[grp, 0]
                q_blk = q_ref[0, r, pl.ds(_block_aligned(blk * BLOCK), BLOCK), :]
            o, lse = _attn_block(q_blk, k_blk, v_blk, bias)
            if d <= SCATTER_STRIDE:
                start = blk * (BLOCK * d) + r
                rows = pl.ds(start, BLOCK, stride=d) if d > 1 else pl.ds(start, BLOCK)
                for s in range(2):
                    o_scr[grp, s, rows, :] = o[:, s * LANES:(s + 1) * LANES]
                    l_scr[grp, s, rows, :] = lse[:, s * LANES:(s + 1) * LANES]
            else:
                inner = d // SCATTER_STRIDE
                low, high = lax.rem(r, SCATTER_STRIDE), r // SCATTER_STRIDE
                rows = pl.ds(blk * (BLOCK * inner) + high, BLOCK, stride=inner)
                for s in range(2):
                    t_scr[0, s, low, rows, :] = o[:, s * LANES:(s + 1) * LANES]
                    t_scr[1, s, low, rows, :] = lse[:, s * LANES:(s + 1) * LANES]

        if blocks_per_residue == 1:
            def residue_set(i, c, run_block=run_block):
                for k in range(ATTN_INTERLEAVE):
                    run_block(ATTN_INTERLEAVE * i + k, 0)
                return c
            lax.fori_loop(0, d // ATTN_INTERLEAVE, residue_set, 0)
        elif d > 1:
            residues_per_set = max(1, ATTN_INTERLEAVE // blocks_per_residue)

            def residue_set(i, c, run_block=run_block, blocks_per_residue=blocks_per_residue,
                            residues_per_set=residues_per_set):
                for k in range(residues_per_set):
                    for blk in range(blocks_per_residue):
                        run_block(residues_per_set * i + k, blk)
                return c
            lax.fori_loop(0, d // residues_per_set, residue_set, 0)
        else:
            for blk in range(ATTN_INTERLEAVE):
                run_block(0, blk)

            def block_set(i, c, run_block=run_block):
                for k in range(ATTN_INTERLEAVE):
                    run_block(0, ATTN_INTERLEAVE * i + k)
                return c
            lax.fori_loop(1, blocks_per_residue // ATTN_INTERLEAVE, block_set, 0)

        if d > SCATTER_STRIDE:
            def unstage(low, c, grp=grp):
                rows = pl.ds(low, ATTN_TILE // SCATTER_STRIDE, stride=SCATTER_STRIDE)
                for s in range(2):
                    o_scr[grp, s, rows, :] = t_scr[0, s, low]
                    l_scr[grp, s, rows, :] = t_scr[1, s, low]
                return c
            lax.fori_loop(0, SCATTER_STRIDE, unstage, 0)

    merge_rows = 256

    def merge(c, carry):
        rows = pl.ds(pl.multiple_of(c * merge_rows, merge_rows), merge_rows)
        for s in range(2):
            lses = [l_scr[grp, s, rows, :] for grp in range(N_GROUPS)]
            mx = functools.reduce(jnp.maximum, lses)
            es = [jnp.exp(x - mx) for x in lses]
            num = sum(e * o_scr[grp, s, rows, :] for grp, e in enumerate(es))
            o_ref[rows, s * LANES:(s + 1) * LANES] = (num / sum(es)).astype(BF16)
        return carry

    lax.fori_loop(0, ATTN_TILE // merge_rows, merge, 0)


def _attention(q, k, v, rel_bias_table, batch, seq):
    tiles = seq // ATTN_TILE

    def cur_spec(d):
        return pl.BlockSpec((1, d, ATTN_TILE // d, GROUP_WIDTH), lambda b, j: (b, 0, j, 0))

    def prev_spec(d):
        per_tile = ATTN_TILE // d // BLOCK
        return pl.BlockSpec((1, d, BLOCK, GROUP_WIDTH),
                            lambda b, j: (b, 0, jnp.maximum(j * per_tile - 1, 0), 0))

    in_specs = ([pl.BlockSpec(memory_space=pltpu.SMEM),
                 _const_spec((N_GROUPS, BLOCK, 2 * BLOCK))]
                + [cur_spec(d) for d in DILATIONS] + [cur_spec(d) for d in DILATIONS]
                + [prev_spec(d) for d in DILATIONS] + [cur_spec(d) for d in DILATIONS]
                + [prev_spec(d) for d in DILATIONS])
    return pl.pallas_call(
        _attn_kernel,
        grid=(batch, tiles),
        in_specs=in_specs,
        out_specs=pl.BlockSpec((ATTN_TILE, GROUP_WIDTH), lambda b, j: (b * tiles + j, 0)),
        out_shape=jax.ShapeDtypeStruct((batch * seq, GROUP_WIDTH), BF16),
        scratch_shapes=[pltpu.VMEM((N_GROUPS, 2, HEADS_PER_GROUP * BLOCK, 2 * BLOCK), F32),
                        pltpu.VMEM((N_GROUPS, 2, ATTN_TILE, LANES), F32),
                        pltpu.VMEM((N_GROUPS, 2, ATTN_TILE, LANES), F32),
                        pltpu.VMEM((2, 2, SCATTER_STRIDE, ATTN_TILE // SCATTER_STRIDE, LANES), F32)],
        compiler_params=pltpu.CompilerParams(
            dimension_semantics=("arbitrary", "arbitrary"), vmem_limit_bytes=VMEM_LIMIT_BYTES),
        name="dilated_attn",
    )(rel_bias_table.astype(F32), _bucket_tables(), *q, *k, *k, *v, *v)


def _octet_mask(row_group, col_group, rows, cols):
    r = np.arange(rows)[:, None] // row_group
    c = np.arange(cols)[None, :] // col_group
    return jnp.asarray(r == c)


def _s5_matrices(a_re, a_im, log_dt, b_re, b_im, c_re, c_im, d_skip):
    hi = lax.Precision.HIGHEST
    tc = SSM_CHUNK
    lam_re = a_re.astype(F32)
    lam_im = a_im.astype(F32)
    dt = jnp.exp(log_dt.astype(F32))[:, None]
    mag = jnp.exp(lam_re * dt)
    ab_re = mag * jnp.cos(lam_im * dt)
    ab_im = mag * jnp.sin(lam_im * dt)
    den = lam_re * lam_re + lam_im * lam_im
    xr = ab_re - 1.0
    coef_re = (xr * lam_re + ab_im * lam_im) / den
    coef_im = (ab_im * lam_re - xr * lam_im) / den
    br = b_re.astype(F32)
    bi = b_im.astype(F32)
    bb_re = coef_re[..., None] * br - coef_im[..., None] * bi
    bb_im = coef_re[..., None] * bi + coef_im[..., None] * br
    cr = c_re.astype(F32)
    ci = c_im.astype(F32)

    def a_pow(k):
        kk = k.astype(F32)[:, None, None]
        pm = jnp.exp(lam_re * dt * kk)
        return pm * jnp.cos(lam_im * dt * kk), pm * jnp.sin(lam_im * dt * kk)

    pw_re, pw_im = a_pow(jnp.arange(tc + 1))
    pr, pi = pw_re[:tc], pw_im[:tc]
    abr = pr[..., None] * bb_re - pi[..., None] * bb_im
    abi = pr[..., None] * bb_im + pi[..., None] * bb_re
    kern = (jnp.einsum('gon,lgnc->lgoc', cr, abr, precision=hi)
            - jnp.einsum('gon,lgnc->lgoc', ci, abi, precision=hi))
    skip = d_skip.astype(F32).reshape(SSM_GROUPS, SSM_GROUP)
    kern = kern.at[0].add(skip[:, :, None] * jnp.eye(SSM_GROUP, dtype=F32))

    kt = jnp.transpose(kern, (0, 1, 3, 2)).reshape(tc, N_OCTETS, LANES, SSM_GROUP)
    kt = jnp.transpose(kt, (1, 0, 2, 3))
    kblk = jnp.where(_octet_mask(SSM_GROUP, SSM_GROUP, LANES, LANES),
                     jnp.tile(kt, (1, 1, 1, SSM_OCTET)), 0.0)

    def by_octet(x):
        return x.reshape(x.shape[0], N_OCTETS, 1, SSM_OCTET, SSM_STATE)

    def channel_major(x):
        x = x.reshape(N_OCTETS, SSM_OCTET, SSM_GROUP, SSM_STATE)
        return jnp.transpose(x, (0, 2, 1, 3))[None]

    def compact(re, im):
        both = jnp.stack([re, im], axis=1)
        return both.reshape(tc, 2, N_OCTETS, SSM_GROUP, OCTET_STATE).astype(BF16)

    lr, li = by_octet(pr[::-1]), by_octet(pi[::-1])
    bt_re = channel_major(jnp.transpose(bb_re, (0, 2, 1)))
    bt_im = channel_major(jnp.transpose(bb_im, (0, 2, 1)))
    p_src = compact(lr * bt_re - li * bt_im, lr * bt_im + li * bt_re)

    qr, qi = by_octet(pw_re[1:]), by_octet(pw_im[1:])
    ct_re, ct_im = channel_major(cr), channel_major(ci)
    q_src = compact(ct_re * qr - ct_im * qi, -(ct_re * qi + ct_im * qr))

    a_chunk = jnp.concatenate([pw_re[tc].reshape(N_OCTETS, 1, OCTET_STATE),
                               pw_im[tc].reshape(N_OCTETS, 1, OCTET_STATE)], axis=-1)
    return kblk.astype(BF16), p_src, q_src, a_chunk


def _s5_kernel(u_ref, kblk_ref, p_ref, q_ref, a_ref, o_ref, k_scr, p_scr, qt_scr, x_scr, s_scr, z_scr):
    n_chunks = x_scr.shape[0]
    tc = SSM_CHUNK

    @pl.when(pl.program_id(1) == 0)
    def _():
        k_scr[...] = jnp.zeros(k_scr.shape, BF16)
        for s in range(tc):
            for t in range(s, tc):
                k_scr[s * LANES:(s + 1) * LANES, t * LANES:(t + 1) * LANES] = kblk_ref[0, t - s]
        row_group = lax.broadcasted_iota(jnp.int32, (LANES, OCTET_STATE), 0) // SSM_GROUP
        lane_group = lax.broadcasted_iota(jnp.int32, (LANES, OCTET_STATE), 1) // SSM_STATE
        same_group = row_group == lane_group
        zeros = jnp.zeros((LANES, OCTET_STATE), BF16)
        for src_ref, dst in ((p_ref, p_scr), (q_ref, qt_scr)):
            for t in range(tc):
                for part in range(2):
                    src = src_ref[t, part, 0]
                    tiled = jnp.concatenate([src] * SSM_OCTET, axis=0)
                    dst[t * LANES:(t + 1) * LANES, part * OCTET_STATE:(part + 1) * OCTET_STATE] = (
                        jnp.where(same_group, tiled, zeros))

    inc = None
    for pair in range(tc // 2):
        for t in (2 * pair, 2 * pair + 1):
            x_scr[:, t * LANES:(t + 1) * LANES] = (
                u_ref[0, pl.ds(t, n_chunks, stride=CHUNK_PITCH), :].astype(BF16))
        rows = slice(pair * MXU_WIDTH, (pair + 1) * MXU_WIDTH)
        part = jnp.dot(x_scr[:, rows], p_scr[rows, :], preferred_element_type=F32)
        inc = part if inc is None else inc + part
    s_scr[...] = inc

    a_re = a_ref[0, :, 0:OCTET_STATE]
    a_im = a_ref[0, :, OCTET_STATE:2 * OCTET_STATE]

    def step(j, carry):
        c_re, c_im = carry
        row = pl.ds(j, 1)
        z_scr[row, 0:OCTET_STATE] = c_re
        z_scr[row, OCTET_STATE:2 * OCTET_STATE] = c_im
        inc_re = s_scr[row, 0:OCTET_STATE]
        inc_im = s_scr[row, OCTET_STATE:2 * OCTET_STATE]
        return (a_re * c_re - a_im * c_im + inc_re, a_re * c_im + a_im * c_re + inc_im)

    zero = jnp.zeros((1, OCTET_STATE), F32)
    lax.fori_loop(0, n_chunks, step, (zero, zero), unroll=8)

    zb = z_scr[...].astype(BF16)
    for pair in range(tc // 2):
        cols = slice(pair * MXU_WIDTH, (pair + 1) * MXU_WIDTH)
        live = (2 * pair + 2) * LANES
        y = (jnp.dot(x_scr[:, 0:live], k_scr[0:live, cols], preferred_element_type=F32)
             + lax.dot_general(zb, qt_scr[cols, :], (((1,), (1,)), ((), ())),
                               preferred_element_type=F32))
        y = jax.nn.gelu(y)
        o_ref[0, pl.ds(2 * pair, n_chunks, stride=CHUNK_PITCH), :] = y[:, 0:LANES]
        o_ref[0, pl.ds(2 * pair + 1, n_chunks, stride=CHUNK_PITCH), :] = y[:, LANES:2 * LANES]
    for t in range(tc, CHUNK_PITCH):
        o_ref[0, pl.ds(t, n_chunks, stride=CHUNK_PITCH), :] = jnp.zeros((n_chunks, LANES), F32)


def _s5(u, mats, batch, seq):
    kblk, p_src, q_src, a_chunk = mats
    n_chunks = seq // SSM_CHUNK
    width = SSM_CHUNK * LANES

    def per_octet(shape):
        return pl.BlockSpec((1,) + shape, lambda o, b: (o,) + (0,) * len(shape),
                            pipeline_mode=pl.Buffered(1))

    compact_spec = pl.BlockSpec((SSM_CHUNK, 2, 1, SSM_GROUP, OCTET_STATE),
                                lambda o, b: (0, 0, o, 0, 0))
    return pl.pallas_call(
        _s5_kernel,
        grid=(N_OCTETS, batch),
        in_specs=[pl.BlockSpec((1, _padded_rows(seq), LANES), lambda o, b: (o, b, 0)),
                  per_octet((SSM_CHUNK, LANES, LANES)),
                  compact_spec,
                  compact_spec,
                  per_octet((1, 2 * OCTET_STATE))],
        out_specs=pl.BlockSpec((1, _padded_rows(seq), LANES), lambda o, b: (o, b, 0)),
        out_shape=jax.ShapeDtypeStruct((N_OCTETS, _padded_rows(batch * seq), LANES), F32),
        scratch_shapes=[pltpu.VMEM((width, width), BF16),
                        pltpu.VMEM((width, 2 * OCTET_STATE), BF16),
                        pltpu.VMEM((width, 2 * OCTET_STATE), BF16),
                        pltpu.VMEM((n_chunks, width), BF16),
                        pltpu.VMEM((n_chunks, 2 * OCTET_STATE), F32),
                        pltpu.VMEM((n_chunks, 2 * OCTET_STATE), F32)],
        compiler_params=pltpu.CompilerParams(
            dimension_semantics=("arbitrary", "arbitrary"), vmem_limit_bytes=VMEM_LIMIT_BYTES),
        name="s5_mixer",
    )(u, kblk, p_src, q_src, a_chunk)


def _merge_ffn_kernel(y_ref, oa_ref, gate_ref, x_ref, wglu_ref, wssm_ref, wattn_ref, wout_ref,
                      g_ref, wg_hbm, wu_hbm, wd_hbm, fn_ref, o_ref, *weight_scratch, final):
    @pl.when(pl.program_id(0) == 0)
    def _():
        _fetch_ffn_weights(wg_hbm, wu_hbm, wd_hbm, *weight_scratch)

    wg_scr, wu_scr, wd_scr = weight_scratch[0:3]
    y_gelu = jnp.concatenate(
        [_load_padded_chunks(y_ref, o, x_ref.shape[0]).astype(BF16) for o in range(N_OCTETS)], axis=1)
    glu = jnp.dot(y_gelu, wglu_ref[...], preferred_element_type=F32)
    ys = (glu[:, 0:SSM_WIDTH] * jax.nn.sigmoid(glu[:, SSM_WIDTH:2 * SSM_WIDTH])).astype(BF16)
    y_ssm = jnp.dot(ys, wssm_ref[...], preferred_element_type=F32)
    y_attn = jnp.dot(oa_ref[...], wattn_ref[...], preferred_element_type=F32)
    g_attn = gate_ref[:, 0:D_MODEL].astype(F32)
    g_ssm = gate_ref[:, D_MODEL:2 * D_MODEL].astype(F32)
    mixed = (g_attn * y_attn + g_ssm * y_ssm).astype(BF16)
    x = x_ref[...] + jnp.dot(mixed, wout_ref[...], preferred_element_type=F32)
    y = _swiglu_residual(x, g_ref[...], wg_scr, wu_scr, wd_scr)
    o_ref[...] = _rms(y, fn_ref[...]) if final else y


def _merge_ffn(y_gelu, o_attn, gates, x, w_glu, w_ssm, w_attn, w_out, g, wg, wu, wd, final_g, final):
    n_tok = x.shape[0]
    tm = MERGE_TILE

    def rows(width):
        return pl.BlockSpec((tm, width), lambda i: (i, 0))

    return pl.pallas_call(
        functools.partial(_merge_ffn_kernel, final=final),
        grid=(n_tok // tm,),
        in_specs=[pl.BlockSpec((N_OCTETS, _padded_rows(tm), LANES), lambda i: (0, i, 0)),
                  rows(GROUP_WIDTH), rows(2 * D_MODEL), rows(D_MODEL),
                  _const_spec((SSM_WIDTH, 2 * SSM_WIDTH)), _const_spec((SSM_WIDTH, D_MODEL)),
                  _const_spec((GROUP_WIDTH, D_MODEL)), _const_spec((D_MODEL, D_MODEL))]
                 + _ffn_weight_specs() + [_const_spec((1, D_MODEL))],
        out_specs=rows(D_MODEL),
        out_shape=jax.ShapeDtypeStruct((n_tok, D_MODEL), F32),
        scratch_shapes=_ffn_weight_scratch(),
        compiler_params=pltpu.CompilerParams(
            dimension_semantics=("arbitrary",), vmem_limit_bytes=VMEM_LIMIT_BYTES),
        name="merge_ffn",
    )(y_gelu, o_attn, gates, x, w_glu, w_ssm, w_attn, w_out,
      g.reshape(1, D_MODEL), wg, wu, wd, final_g.reshape(1, D_MODEL))


def kernel(x, ffn1_norm, ffn1_w_gate, ffn1_w_up, ffn1_w_down, mix_norm, w_in, gate_bias,
           rel_bias_table, ssm_a_re, ssm_a_im, ssm_log_dt, ssm_b_re, ssm_b_im, ssm_c_re,
           ssm_c_im, ssm_d, ssm_w_glu, w_attn_branch, w_ssm_branch, w_out, ffn2_norm,
           ffn2_w_gate, ffn2_w_up, ffn2_w_down, final_norm):
    batch, seq, _ = x.shape
    depth = ffn1_norm.shape[0]
    h = x.reshape(batch * seq, D_MODEL)
    for l in range(depth):
        h = _ffn(h, ffn1_norm[l], ffn1_w_gate[l].astype(F32), ffn1_w_up[l].astype(F32),
                 ffn1_w_down[l].astype(F32))
        q, k, v, u, gates = _in_proj(h, mix_norm[l], w_in[l].astype(F32), gate_bias[l], batch, seq)
        o_attn = _attention(q, k, v, rel_bias_table, batch, seq)
        mats = _s5_matrices(ssm_a_re[l], ssm_a_im[l], ssm_log_dt[l], ssm_b_re[l], ssm_b_im[l],
                            ssm_c_re[l], ssm_c_im[l], ssm_d[l])
        y_gelu = _s5(u, mats, batch, seq)
        h = _merge_ffn(y_gelu, o_attn, gates, h, ssm_w_glu[l].astype(BF16),
                       w_ssm_branch[l].astype(BF16), w_attn_branch[l].astype(BF16),
                       w_out[l].astype(BF16), ffn2_norm[l], ffn2_w_gate[l].astype(F32),
                       ffn2_w_up[l].astype(F32), ffn2_w_down[l].astype(F32), final_norm,
                       final=(l == depth - 1))
    return h.reshape(batch, seq, D_MODEL)
```

```python
import functools
import math

import jax
import jax.numpy as jnp
import numpy as np
from jax import lax
from jax.experimental import pallas as pl
from jax.experimental.pallas import tpu as pltpu

F32 = jnp.float32
BF16 = jnp.bfloat16

D_MODEL = 1024
D_FF = 2816
EPS = 1e-6
HEAD_DIM = 64
HEADS_PER_GROUP = 4
GROUP_WIDTH = HEADS_PER_GROUP * HEAD_DIM
DILATIONS = (1, 4, 16)
WINDOWS = (128, 512, 2048)
N_GROUPS = len(DILATIONS)
ATTN_WIDTH = N_GROUPS * GROUP_WIDTH
BLOCK = 128
N_BUCKETS = 32
MAX_DISTANCE = 2048
NEG_INF = -1e30
SSM_GROUP = 16
SSM_WIDTH = 512
SSM_GROUPS = SSM_WIDTH // SSM_GROUP
SSM_STATE = 64
IN_WIDTH = 3 * ATTN_WIDTH + SSM_WIDTH + 2 * D_MODEL

LANES = 128
MXU_WIDTH = 256
VMEM_LIMIT_BYTES = 56 * 1024 * 1024

FFN_TILE = 1024
PROJ_TILE = 1024
MERGE_TILE = 512
WEIGHT_STAGE_STEPS = 8
ATTN_TILE = BLOCK * DILATIONS[-1]
ATTN_INTERLEAVE = 8
SCATTER_STRIDE = 4
SSM_CHUNK = 16
CHUNK_PITCH = SSM_CHUNK + 4
SSM_OCTET = LANES // SSM_GROUP
N_OCTETS = SSM_GROUPS // SSM_OCTET
OCTET_STATE = SSM_OCTET * SSM_STATE


def _rms(x, g):
    return x * lax.rsqrt(jnp.mean(x * x, axis=-1, keepdims=True) + EPS) * g


def _const_spec(shape):
    return pl.BlockSpec(shape, lambda *_: (0,) * len(shape), pipeline_mode=pl.Buffered(1))


def _swiglu_residual(x, g, wg_ref, wu_ref, wd_ref):
    h = _rms(x, g).astype(BF16)
    acc = jnp.zeros(x.shape, F32)
    for c in range(D_FF // MXU_WIDTH):
        sl = slice(c * MXU_WIDTH, (c + 1) * MXU_WIDTH)
        a = jnp.dot(h, wg_ref[:, sl], preferred_element_type=F32)
        b = jnp.dot(h, wu_ref[:, sl], preferred_element_type=F32)
        t = (a * jax.nn.sigmoid(a) * b).astype(BF16)
        acc = acc + jnp.dot(t, wd_ref[sl, :], preferred_element_type=F32)
    return x + 0.5 * acc


def _weight_copy(w_hbm, stage, sems, first_sem, i):
    rows = stage.shape[1]
    return pltpu.make_async_copy(w_hbm.at[pl.ds(i * rows, rows), :], stage.at[i % 2],
                                 sems.at[first_sem + i % 2])


def _fetch_cast(streams, sems):
    steps = {w_hbm.shape[0] // stage.shape[1] for w_hbm, _, stage in streams}
    (n,) = steps
    for k, (w_hbm, _, stage) in enumerate(streams):
        _weight_copy(w_hbm, stage, sems, 2 * k, 0).start()
    for i in range(n):
        for k, (w_hbm, w_scr, stage) in enumerate(streams):
            if i + 1 < n:
                _weight_copy(w_hbm, stage, sems, 2 * k, i + 1).start()
            _weight_copy(w_hbm, stage, sems, 2 * k, i).wait()
            rows = stage.shape[1]
            w_scr[i * rows:(i + 1) * rows, :] = stage[i % 2].astype(BF16)


def _fetch_ffn_weights(wg_hbm, wu_hbm, wd_hbm, wg_scr, wu_scr, wd_scr, gate_stage, up_stage, down_stage, sems):
    _fetch_cast([(wg_hbm, wg_scr, gate_stage), (wu_hbm, wu_scr, up_stage), (wd_hbm, wd_scr, down_stage)],
                sems)


def _ffn_kernel(x_ref, g_ref, wg_hbm, wu_hbm, wd_hbm, o_ref, *weight_scratch):
    @pl.when(pl.program_id(0) == 0)
    def _():
        _fetch_ffn_weights(wg_hbm, wu_hbm, wd_hbm, *weight_scratch)

    wg_scr, wu_scr, wd_scr = weight_scratch[0:3]
    o_ref[...] = _swiglu_residual(x_ref[...], g_ref[...], wg_scr, wu_scr, wd_scr)


def _ffn_weight_specs():
    hbm = pl.BlockSpec(memory_space=pl.ANY)
    return [_const_spec((1, D_MODEL)), hbm, hbm, hbm]


def _ffn_weight_scratch():
    return [pltpu.VMEM((D_MODEL, D_FF), BF16), pltpu.VMEM((D_MODEL, D_FF), BF16),
            pltpu.VMEM((D_FF, D_MODEL), BF16),
            pltpu.VMEM((2, D_MODEL // WEIGHT_STAGE_STEPS, D_FF), F32),
            pltpu.VMEM((2, D_MODEL // WEIGHT_STAGE_STEPS, D_FF), F32),
            pltpu.VMEM((2, D_FF // WEIGHT_STAGE_STEPS, D_MODEL), F32),
            pltpu.SemaphoreType.DMA((6,))]


def _ffn(x, g, wg, wu, wd):
    n_tok = x.shape[0]
    tm = FFN_TILE
    tile = pl.BlockSpec((tm, D_MODEL), lambda i: (i, 0))
    return pl.pallas_call(
        _ffn_kernel,
        grid=(n_tok // tm,),
        in_specs=[tile] + _ffn_weight_specs(),
        out_specs=tile,
        out_shape=jax.ShapeDtypeStruct((n_tok, D_MODEL), F32),
        scratch_shapes=_ffn_weight_scratch(),
        compiler_params=pltpu.CompilerParams(
            dimension_semantics=("arbitrary",), vmem_limit_bytes=VMEM_LIMIT_BYTES),
        name="ffn",
    )(x, g.reshape(1, D_MODEL), wg, wu, wd)


N_QKV_CHUNKS = 3 * N_GROUPS
N_U_CHUNKS = SSM_WIDTH // MXU_WIDTH
N_GATE_CHUNKS = 2 * D_MODEL // MXU_WIDTH
N_STRIDED = 3 * (N_GROUPS - 1)
N_TWO_HOP = 3 * sum(d > SCATTER_STRIDE for d in DILATIONS)


def _padded_rows(n_tokens):
    return n_tokens // SSM_CHUNK * CHUNK_PITCH


def _store_padded_chunks(ref, slab, rows):
    pad = jnp.zeros((CHUNK_PITCH - SSM_CHUNK, rows.shape[1]), rows.dtype)
    for ch in range(rows.shape[0] // SSM_CHUNK):
        base = ch * CHUNK_PITCH
        ref[slab, base:base + SSM_CHUNK, :] = rows[ch * SSM_CHUNK:(ch + 1) * SSM_CHUNK, :]
        ref[slab, base + SSM_CHUNK:base + CHUNK_PITCH, :] = pad


def _load_padded_chunks(ref, slab, n_tokens):
    return jnp.concatenate(
        [ref[slab, ch * CHUNK_PITCH:ch * CHUNK_PITCH + SSM_CHUNK, :]
         for ch in range(n_tokens // SSM_CHUNK)], axis=0)


def _inproj_kernel(x_ref, g_ref, w_hbm, gb_ref, *refs):
    qkv_refs = refs[:N_QKV_CHUNKS]
    u_ref, gate_ref, slab_ref, hop_ref, w_ref, w_stage, w_sem = refs[N_QKV_CHUNKS:]
    tm = x_ref.shape[0]

    @pl.when(pl.program_id(0) == 0)
    def _():
        half = D_MODEL // 2
        _fetch_cast([(w_hbm.at[pl.ds(s * half, half), :], w_ref.at[pl.ds(s * half, half), :],
                      w_stage.at[s]) for s in range(2)], w_sem)

    h = _rms(x_ref[...], g_ref[...]).astype(BF16)
    strided_idx = 0
    hop_idx = 0
    heavy = list(range(N_QKV_CHUNKS + N_U_CHUNKS))
    light = list(range(N_QKV_CHUNKS + N_U_CHUNKS, IN_WIDTH // MXU_WIDTH))
    order = []
    while heavy or light:
        if heavy:
            order.append(heavy.pop(0))
        if light:
            order.append(light.pop(0))
    for c in order:
        z = jnp.dot(h, w_ref[:, c * MXU_WIDTH:(c + 1) * MXU_WIDTH], preferred_element_type=F32)
        if c < N_QKV_CHUNKS:
            kind, grp = divmod(c, N_GROUPS)
            if kind == 0:
                z = z * (HEAD_DIM ** -0.5)
            d = DILATIONS[grp]
            o_ref = qkv_refs[c]
            if d == 1:
                o_ref[0, 0] = z.astype(BF16)
            else:
                base = 2 * strided_idx
                strided_idx += 1
                for s in range(2):
                    slab_ref[base + s] = z[:, s * LANES:(s + 1) * LANES]
                lanes = [slice(s * LANES, (s + 1) * LANES) for s in range(2)]
                if d <= SCATTER_STRIDE:
                    for r in range(d):
                        for s in range(2):
                            o_ref[0, r, :, lanes[s]] = (
                                slab_ref[base + s, pl.ds(r, tm // d, stride=d), :].astype(BF16))
                else:
                    hop = tm // SCATTER_STRIDE
                    hop_base = 2 * hop_idx
                    hop_idx += 1
                    for s in range(2):
                        for low in range(SCATTER_STRIDE):
                            hop_ref[hop_base + s, pl.ds(low * hop, hop), :] = (
                                slab_ref[base + s, pl.ds(low, hop, stride=SCATTER_STRIDE), :])
                    inner = d // SCATTER_STRIDE
                    for r in range(d):
                        low, high = r % SCATTER_STRIDE, r // SCATTER_STRIDE
                        for s in range(2):
                            o_ref[0, r, :, lanes[s]] = hop_ref[
                                hop_base + s, pl.ds(low * hop + high, tm // d, stride=inner), :
                            ].astype(BF16)
        elif c < N_QKV_CHUNKS + N_U_CHUNKS:
            j = c - N_QKV_CHUNKS
            for s in range(2):
                _store_padded_chunks(u_ref, 2 * j + s, z[:, s * LANES:(s + 1) * LANES])
        else:
            j = c - N_QKV_CHUNKS - N_U_CHUNKS
            sl = slice(j * MXU_WIDTH, (j + 1) * MXU_WIDTH)
            gate_ref[:, sl] = jax.nn.sigmoid(z + gb_ref[:, sl]).astype(BF16)


def _in_proj(x, g, w_in, gate_bias, batch, seq):
    n_tok = x.shape[0]
    tm = PROJ_TILE
    tiles_per_seq = seq // tm
    out_shapes, out_specs = [], []
    for _ in range(3):
        for d in DILATIONS:
            out_shapes.append(jax.ShapeDtypeStruct((batch, d, seq // d, GROUP_WIDTH), BF16))
            out_specs.append(pl.BlockSpec(
                (1, d, tm // d, GROUP_WIDTH),
                lambda i: (i // tiles_per_seq, 0, i % tiles_per_seq, 0)))
    out_shapes.append(jax.ShapeDtypeStruct((N_OCTETS, _padded_rows(n_tok), LANES), F32))
    out_specs.append(pl.BlockSpec((N_OCTETS, _padded_rows(tm), LANES), lambda i: (0, i, 0)))
    out_shapes.append(jax.ShapeDtypeStruct((n_tok, 2 * D_MODEL), BF16))
    out_specs.append(pl.BlockSpec((tm, 2 * D_MODEL), lambda i: (i, 0)))
    outs = pl.pallas_call(
        _inproj_kernel,
        grid=(n_tok // tm,),
        in_specs=[pl.BlockSpec((tm, D_MODEL), lambda i: (i, 0)),
                  _const_spec((1, D_MODEL)),
                  pl.BlockSpec(memory_space=pl.ANY),
                  _const_spec((1, 2 * D_MODEL))],
        out_specs=out_specs,
        out_shape=out_shapes,
        scratch_shapes=[pltpu.VMEM((2 * N_STRIDED, tm, LANES), F32),
                        pltpu.VMEM((2 * N_TWO_HOP, tm, LANES), F32),
                        pltpu.VMEM((D_MODEL, IN_WIDTH), BF16),
                        pltpu.VMEM((2, 2, D_MODEL // (2 * WEIGHT_STAGE_STEPS), IN_WIDTH), F32),
                        pltpu.SemaphoreType.DMA((4,))],
        compiler_params=pltpu.CompilerParams(
            dimension_semantics=("arbitrary",), vmem_limit_bytes=VMEM_LIMIT_BYTES),
        name="in_proj",
    )(x, g.reshape(1, D_MODEL), w_in, gate_bias.reshape(1, 2 * D_MODEL))
    q, k, v = outs[0:3], outs[3:6], outs[6:9]
    return q, k, v, outs[9], outs[10]


def _t5_bucket_np(dist):
    max_exact = N_BUCKETS // 2
    d = np.maximum(dist, 1).astype(np.float32)
    ratio = np.log(d / np.float32(max_exact)) / np.float32(math.log(MAX_DISTANCE / max_exact))
    large = max_exact + (ratio * np.float32(N_BUCKETS - max_exact)).astype(np.int32)
    large = np.minimum(large, N_BUCKETS - 1)
    return np.where(dist < max_exact, dist, large)


def _bucket_tables():
    qi = np.arange(BLOCK)[:, None]
    kj = np.arange(2 * BLOCK)[None, :]
    steps = qi + BLOCK - kj
    tables = []
    for window, d in zip(WINDOWS, DILATIONS):
        band = (steps >= 0) & (steps <= window // d)
        bucket = _t5_bucket_np(np.maximum(steps, 0) * d)
        tables.append(np.where(band, bucket, -1).astype(np.int32))
    return jnp.asarray(np.stack(tables))


def _attn_block(q_blk, k_blk, v_blk, bias):
    lane_head = lax.broadcasted_iota(jnp.int32, (BLOCK, GROUP_WIDTH), 1) // HEAD_DIM
    zero = jnp.zeros_like(q_blk)
    qs = jnp.concatenate(
        [jnp.where(lane_head == h, q_blk, zero) for h in range(HEADS_PER_GROUP)], axis=0)
    logits = lax.dot_general(qs, k_blk, (((1,), (1,)), ((), ())), preferred_element_type=F32) + bias
    m = jnp.max(logits, axis=-1, keepdims=True)
    p = jnp.exp(logits - m)
    l = jnp.sum(p, axis=-1, keepdims=True)
    pv = jnp.dot(p.astype(BF16), v_blk, preferred_element_type=F32)
    o = jnp.zeros((BLOCK, GROUP_WIDTH), F32)
    den = jnp.ones((BLOCK, GROUP_WIDTH), F32)
    mx = jnp.zeros((BLOCK, GROUP_WIDTH), F32)
    for h in range(HEADS_PER_GROUP):
        rows = slice(h * BLOCK, (h + 1) * BLOCK)
        sel = lane_head == h
        o = jnp.where(sel, pv[rows], o)
        den = jnp.where(sel, l[rows], den)
        mx = jnp.where(sel, m[rows], mx)
    return o / den, mx + jnp.log(den)


def _build_bias(tab_ref, bucket_ref, bias_scr):
    own_block = lax.broadcasted_iota(jnp.int32, (BLOCK, 2 * BLOCK), 1) >= BLOCK
    for grp in range(N_GROUPS):
        bucket = bucket_ref[grp]
        for h in range(HEADS_PER_GROUP):
            col = grp * HEADS_PER_GROUP + h

            def pick(b, acc, bucket=bucket, col=col):
                return jnp.where(bucket == b, tab_ref[b, col], acc)

            bias = lax.fori_loop(0, N_BUCKETS, pick, jnp.full((BLOCK, 2 * BLOCK), NEG_INF, F32))
            rows = slice(h * BLOCK, (h + 1) * BLOCK)
            bias_scr[grp, 0, rows, :] = bias
            bias_scr[grp, 1, rows, :] = jnp.where(own_block, bias, NEG_INF)


def _block_aligned(row):
    return row if isinstance(row, int) else pl.multiple_of(row, BLOCK)


def _attn_kernel(*refs):
    n = N_GROUPS
    tab_ref, bucket_ref = refs[0:2]
    refs = refs[2:]
    q_refs, kc_refs, kp_refs = refs[0:n], refs[n:2 * n], refs[2 * n:3 * n]
    vc_refs, vp_refs = refs[3 * n:4 * n], refs[4 * n:5 * n]
    o_ref, bias_scr, o_scr, l_scr, t_scr = refs[5 * n:]
    tile = pl.program_id(1)

    @pl.when((pl.program_id(0) == 0) & (tile == 0))
    def _():
        _build_bias(tab_ref, bucket_ref, bias_scr)

    for grp, d in enumerate(DILATIONS):
        q_ref, kc_ref, kp_ref = q_refs[grp], kc_refs[grp], kp_refs[grp]
        vc_ref, vp_ref = vc_refs[grp], vp_refs[grp]
        blocks_per_residue = ATTN_TILE // d // BLOCK

        def run_block(r, blk, grp=grp, d=d, q_ref=q_ref, kc_ref=kc_ref, kp_ref=kp_ref,
                      vc_ref=vc_ref, vp_ref=vp_ref):
            if isinstance(blk, int) and blk == 0:
                k_blk = jnp.concatenate([kp_ref[0, r], kc_ref[0, r, 0:BLOCK, :]], axis=0)
                v_blk = jnp.concatenate([vp_ref[0, r], vc_ref[0, r, 0:BLOCK, :]], axis=0)
                bias = jnp.where(tile == 0, bias_scr[grp, 1], bias_scr[grp, 0])
                q_blk = q_ref[0, r, 0:BLOCK, :]
            else:
                kv_rows = pl.ds(_block_aligned((blk - 1) * BLOCK), 2 * BLOCK)
                k_blk = kc_ref[0, r, kv_rows, :]
                v_blk = vc_ref[0, r, kv_rows, :]
                bias = bias_scr[grp, 0]
                q_blk = q_ref[0, r, pl.ds(_block_aligned(blk * BLOCK), BLOCK), :]
            o, lse = _attn_block(q_blk, k_blk, v_blk, bias)
            if d <= SCATTER_STRIDE:
                start = blk * (BLOCK * d) + r
                rows = pl.ds(start, BLOCK, stride=d) if d > 1 else pl.ds(start, BLOCK)
                for s in range(2):
                    o_scr[grp, s, rows, :] = o[:, s * LANES:(s + 1) * LANES]
                    l_scr[grp, s, rows, :] = lse[:, s * LANES:(s + 1) * LANES]
            else:
                inner = d // SCATTER_STRIDE
                low, high = lax.rem(r, SCATTER_STRIDE), r // SCATTER_STRIDE
                rows = pl.ds(blk * (BLOCK * inner) + high, BLOCK, stride=inner)
                for s in range(2):
                    t_scr[0, s, low, rows, :] = o[:, s * LANES:(s + 1) * LANES]
                    t_scr[1, s, low, rows, :] = lse[:, s * LANES:(s + 1) * LANES]

        if blocks_per_residue == 1:
            def residue_set(i, c, run_block=run_block):
                for k in range(ATTN_INTERLEAVE):
                    run_block(ATTN_INTERLEAVE * i + k, 0)
                return c
            lax.fori_loop(0, d // ATTN_INTERLEAVE, residue_set, 0)
        elif d > 1:
            residues_per_set = max(1, ATTN_INTERLEAVE // blocks_per_residue)

            def residue_set(i, c, run_block=run_block, blocks_per_residue=blocks_per_residue,
                            residues_per_set=residues_per_set):
                for k in range(residues_per_set):
                    for blk in range(blocks_per_residue):
                        run_block(residues_per_set * i + k, blk)
                return c
            lax.fori_loop(0, d // residues_per_set, residue_set, 0)
        else:
            for blk in range(ATTN_INTERLEAVE):
                run_block(0, blk)

            def block_set(i, c, run_block=run_block):
                for k in range(ATTN_INTERLEAVE):
                    run_block(0, ATTN_INTERLEAVE * i + k)
                return c
            lax.fori_loop(1, blocks_per_residue // ATTN_INTERLEAVE, block_set, 0)

        if d > SCATTER_STRIDE:
            def unstage(low, c, grp=grp):
                rows = pl.ds(low, ATTN_TILE // SCATTER_STRIDE, stride=SCATTER_STRIDE)
                for s in range(2):
                    o_scr[grp, s, rows, :] = t_scr[0, s, low]
                    l_scr[grp, s, rows, :] = t_scr[1, s, low]
                return c
            lax.fori_loop(0, SCATTER_STRIDE, unstage, 0)

    merge_rows = 256

    def merge(c, carry):
        rows = pl.ds(pl.multiple_of(c * merge_rows, merge_rows), merge_rows)
        for s in range(2):
            lses = [l_scr[grp, s, rows, :] for grp in range(N_GROUPS)]
            mx = functools.reduce(jnp.maximum, lses)
            es = [jnp.exp(x - mx) for x in lses]
            num = sum(e * o_scr[grp, s, rows, :] for grp, e in enumerate(es))
            o_ref[rows, s * LANES:(s + 1) * LANES] = (num / sum(es)).astype(BF16)
        return carry

    lax.fori_loop(0, ATTN_TILE // merge_rows, merge, 0)


def _attention(q, k, v, rel_bias_table, batch, seq):
    tiles = seq // ATTN_TILE

    def cur_spec(d):
        return pl.BlockSpec((1, d, ATTN_TILE // d, GROUP_WIDTH), lambda b, j: (b, 0, j, 0))

    def prev_spec(d):
        per_tile = ATTN_TILE // d // BLOCK
        return pl.BlockSpec((1, d, BLOCK, GROUP_WIDTH),
                            lambda b, j: (b, 0, jnp.maximum(j * per_tile - 1, 0), 0))

    in_specs = ([pl.BlockSpec(memory_space=pltpu.SMEM),
                 _const_spec((N_GROUPS, BLOCK, 2 * BLOCK))]
                + [cur_spec(d) for d in DILATIONS] + [cur_spec(d) for d in DILATIONS]
                + [prev_spec(d) for d in DILATIONS] + [cur_spec(d) for d in DILATIONS]
                + [prev_spec(d) for d in DILATIONS])
    return pl.pallas_call(
        _attn_kernel,
        grid=(batch, tiles),
        in_specs=in_specs,
        out_specs=pl.BlockSpec((ATTN_TILE, GROUP_WIDTH), lambda b, j: (b * tiles + j, 0)),
        out_shape=jax.ShapeDtypeStruct((batch * seq, GROUP_WIDTH), BF16),
        scratch_shapes=[pltpu.VMEM((N_GROUPS, 2, HEADS_PER_GROUP * BLOCK, 2 * BLOCK), F32),
                        pltpu.VMEM((N_GROUPS, 2, ATTN_TILE, LANES), F32),
                        pltpu.VMEM((N_GROUPS, 2, ATTN_TILE, LANES), F32),
                        pltpu.VMEM((2, 2, SCATTER_STRIDE, ATTN_TILE // SCATTER_STRIDE, LANES), F32)],
        compiler_params=pltpu.CompilerParams(
            dimension_semantics=("arbitrary", "arbitrary"), vmem_limit_bytes=VMEM_LIMIT_BYTES),
        name="dilated_attn",
    )(rel_bias_table.astype(F32), _bucket_tables(), *q, *k, *k, *v, *v)


def _octet_mask(row_group, col_group, rows, cols):
    r = np.arange(rows)[:, None] // row_group
    c = np.arange(cols)[None, :] // col_group
    return jnp.asarray(r == c)


def _s5_matrices(a_re, a_im, log_dt, b_re, b_im, c_re, c_im, d_skip):
    hi = lax.Precision.HIGHEST
    tc = SSM_CHUNK
    lam_re = a_re.astype(F32)
    lam_im = a_im.astype(F32)
    dt = jnp.exp(log_dt.astype(F32))[:, None]
    mag = jnp.exp(lam_re * dt)
    ab_re = mag * jnp.cos(lam_im * dt)
    ab_im = mag * jnp.sin(lam_im * dt)
    den = lam_re * lam_re + lam_im * lam_im
    xr = ab_re - 1.0
    coef_re = (xr * lam_re + ab_im * lam_im) / den
    coef_im = (ab_im * lam_re - xr * lam_im) / den
    br = b_re.astype(F32)
    bi = b_im.astype(F32)
    bb_re = coef_re[..., None] * br - coef_im[..., None] * bi
    bb_im = coef_re[..., None] * bi + coef_im[..., None] * br
    cr = c_re.astype(F32)
    ci = c_im.astype(F32)

    def a_pow(k):
        kk = k.astype(F32)[:, None, None]
        pm = jnp.exp(lam_re * dt * kk)
        return pm * jnp.cos(lam_im * dt * kk), pm * jnp.sin(lam_im * dt * kk)

    pw_re, pw_im = a_pow(jnp.arange(tc + 1))
    pr, pi = pw_re[:tc], pw_im[:tc]
    abr = pr[..., None] * bb_re - pi[..., None] * bb_im
    abi = pr[..., None] * bb_im + pi[..., None] * bb_re
    kern = (jnp.einsum('gon,lgnc->lgoc', cr, abr, precision=hi)
            - jnp.einsum('gon,lgnc->lgoc', ci, abi, precision=hi))
    skip = d_skip.astype(F32).reshape(SSM_GROUPS, SSM_GROUP)
    kern = kern.at[0].add(skip[:, :, None] * jnp.eye(SSM_GROUP, dtype=F32))

    kt = jnp.transpose(kern, (0, 1, 3, 2)).reshape(tc, N_OCTETS, LANES, SSM_GROUP)
    kt = jnp.transpose(kt, (1, 0, 2, 3))
    kblk = jnp.where(_octet_mask(SSM_GROUP, SSM_GROUP, LANES, LANES),
                     jnp.tile(kt, (1, 1, 1, SSM_OCTET)), 0.0)

    def by_octet(x):
        return x.reshape(x.shape[0], N_OCTETS, 1, SSM_OCTET, SSM_STATE)

    def channel_major(x):
        x = x.reshape(N_OCTETS, SSM_OCTET, SSM_GROUP, SSM_STATE)
        return jnp.transpose(x, (0, 2, 1, 3))[None]

    def compact(re, im):
        both = jnp.stack([re, im], axis=1)
        return both.reshape(tc, 2, N_OCTETS, SSM_GROUP, OCTET_STATE).astype(BF16)

    lr, li = by_octet(pr[::-1]), by_octet(pi[::-1])
    bt_re = channel_major(jnp.transpose(bb_re, (0, 2, 1)))
    bt_im = channel_major(jnp.transpose(bb_im, (0, 2, 1)))
    p_src = compact(lr * bt_re - li * bt_im, lr * bt_im + li * bt_re)

    qr, qi = by_octet(pw_re[1:]), by_octet(pw_im[1:])
    ct_re, ct_im = channel_major(cr), channel_major(ci)
    q_src = compact(ct_re * qr - ct_im * qi, -(ct_re * qi + ct_im * qr))

    a_chunk = jnp.concatenate([pw_re[tc].reshape(N_OCTETS, 1, OCTET_STATE),
                               pw_im[tc].reshape(N_OCTETS, 1, OCTET_STATE)], axis=-1)
    return kblk.astype(BF16), p_src, q_src, a_chunk


def _s5_kernel(u_ref, kblk_ref, p_ref, q_ref, a_ref, o_ref, k_scr, p_scr, qt_scr, x_scr, s_scr, z_scr):
    n_chunks = x_scr.shape[0]
    tc = SSM_CHUNK

    @pl.when(pl.program_id(1) == 0)
    def _():
        k_scr[...] = jnp.zeros(k_scr.shape, BF16)
        for s in range(tc):
            for t in range(s, tc):
                k_scr[s * LANES:(s + 1) * LANES, t * LANES:(t + 1) * LANES] = kblk_ref[0, t - s]
        row_group = lax.broadcasted_iota(jnp.int32, (LANES, OCTET_STATE), 0) // SSM_GROUP
        lane_group = lax.broadcasted_iota(jnp.int32, (LANES, OCTET_STATE), 1) // SSM_STATE
        same_group = row_group == lane_group
        zeros = jnp.zeros((LANES, OCTET_STATE), BF16)
        for src_ref, dst in ((p_ref, p_scr), (q_ref, qt_scr)):
            for t in range(tc):
                for part in range(2):
                    src = src_ref[t, part, 0]
                    tiled = jnp.concatenate([src] * SSM_OCTET, axis=0)
                    dst[t * LANES:(t + 1) * LANES, part * OCTET_STATE:(part + 1) * OCTET_STATE] = (
                        jnp.where(same_group, tiled, zeros))

    inc = None
    for pair in range(tc // 2):
        for t in (2 * pair, 2 * pair + 1):
            x_scr[:, t * LANES:(t + 1) * LANES] = (
                u_ref[0, pl.ds(t, n_chunks, stride=CHUNK_PITCH), :].astype(BF16))
        rows = slice(pair * MXU_WIDTH, (pair + 1) * MXU_WIDTH)
        part = jnp.dot(x_scr[:, rows], p_scr[rows, :], preferred_element_type=F32)
        inc = part if inc is None else inc + part
    s_scr[...] = inc

    a_re = a_ref[0, :, 0:OCTET_STATE]
    a_im = a_ref[0, :, OCTET_STATE:2 * OCTET_STATE]

    def step(j, carry):
        c_re, c_im = carry
        row = pl.ds(j, 1)
        z_scr[row, 0:OCTET_STATE] = c_re
        z_scr[row, OCTET_STATE:2 * OCTET_STATE] = c_im
        inc_re = s_scr[row, 0:OCTET_STATE]
        inc_im = s_scr[row, OCTET_STATE:2 * OCTET_STATE]
        return (a_re * c_re - a_im * c_im + inc_re, a_re * c_im + a_im * c_re + inc_im)

    zero = jnp.zeros((1, OCTET_STATE), F32)
    lax.fori_loop(0, n_chunks, step, (zero, zero), unroll=8)

    zb = z_scr[...].astype(BF16)
    for pair in range(tc // 2):
        cols = slice(pair * MXU_WIDTH, (pair + 1) * MXU_WIDTH)
        live = (2 * pair + 2) * LANES
        y = (jnp.dot(x_scr[:, 0:live], k_scr[0:live, cols], preferred_element_type=F32)
             + lax.dot_general(zb, qt_scr[cols, :], (((1,), (1,)), ((), ())),
                               preferred_element_type=F32))
        y = jax.nn.gelu(y)
        o_ref[0, pl.ds(2 * pair, n_chunks, stride=CHUNK_PITCH), :] = y[:, 0:LANES]
        o_ref[0, pl.ds(2 * pair + 1, n_chunks, stride=CHUNK_PITCH), :] = y[:, LANES:2 * LANES]
    for t in range(tc, CHUNK_PITCH):
        o_ref[0, pl.ds(t, n_chunks, stride=CHUNK_PITCH), :] = jnp.zeros((n_chunks, LANES), F32)


def _s5(u, mats, batch, seq):
    kblk, p_src, q_src, a_chunk = mats
    n_chunks = seq // SSM_CHUNK
    width = SSM_CHUNK * LANES

    def per_octet(shape):
        return pl.BlockSpec((1,) + shape, lambda o, b: (o,) + (0,) * len(shape),
                            pipeline_mode=pl.Buffered(1))

    compact_spec = pl.BlockSpec((SSM_CHUNK, 2, 1, SSM_GROUP, OCTET_STATE),
                                lambda o, b: (0, 0, o, 0, 0))
    return pl.pallas_call(
        _s5_kernel,
        grid=(N_OCTETS, batch),
        in_specs=[pl.BlockSpec((1, _padded_rows(seq), LANES), lambda o, b: (o, b, 0)),
                  per_octet((SSM_CHUNK, LANES, LANES)),
                  compact_spec,
                  compact_spec,
                  per_octet((1, 2 * OCTET_STATE))],
        out_specs=pl.BlockSpec((1, _padded_rows(seq), LANES), lambda o, b: (o, b, 0)),
        out_shape=jax.ShapeDtypeStruct((N_OCTETS, _padded_rows(batch * seq), LANES), F32),
        scratch_shapes=[pltpu.VMEM((width, width), BF16),
                        pltpu.VMEM((width, 2 * OCTET_STATE), BF16),
                        pltpu.VMEM((width, 2 * OCTET_STATE), BF16),
                        pltpu.VMEM((n_chunks, width), BF16),
                        pltpu.VMEM((n_chunks, 2 * OCTET_STATE), F32),
                        pltpu.VMEM((n_chunks, 2 * OCTET_STATE), F32)],
        compiler_params=pltpu.CompilerParams(
            dimension_semantics=("arbitrary", "arbitrary"), vmem_limit_bytes=VMEM_LIMIT_BYTES),
        name="s5_mixer",
    )(u, kblk, p_src, q_src, a_chunk)


def _merge_ffn_kernel(y_ref, oa_ref, gate_ref, x_ref, wglu_ref, wssm_ref, wattn_ref, wout_ref,
                      g_ref, wg_hbm, wu_hbm, wd_hbm, fn_ref, o_ref, *weight_scratch, final):
    @pl.when(pl.program_id(0) == 0)
    def _():
        _fetch_ffn_weights(wg_hbm, wu_hbm, wd_hbm, *weight_scratch)

    wg_scr, wu_scr, wd_scr = weight_scratch[0:3]
    y_gelu = jnp.concatenate(
        [_load_padded_chunks(y_ref, o, x_ref.shape[0]).astype(BF16) for o in range(N_OCTETS)], axis=1)
    glu = jnp.dot(y_gelu, wglu_ref[...], preferred_element_type=F32)
    ys = (glu[:, 0:SSM_WIDTH] * jax.nn.sigmoid(glu[:, SSM_WIDTH:2 * SSM_WIDTH])).astype(BF16)
    y_ssm = jnp.dot(ys, wssm_ref[...], preferred_element_type=F32)
    y_attn = jnp.dot(oa_ref[...], wattn_ref[...], preferred_element_type=F32)
    g_attn = gate_ref[:, 0:D_MODEL].astype(F32)
    g_ssm = gate_ref[:, D_MODEL:2 * D_MODEL].astype(F32)
    mixed = (g_attn * y_attn + g_ssm * y_ssm).astype(BF16)
    x = x_ref[...] + jnp.dot(mixed, wout_ref[...], preferred_element_type=F32)
    y = _swiglu_residual(x, g_ref[...], wg_scr, wu_scr, wd_scr)
    o_ref[...] = _rms(y, fn_ref[...]) if final else y


def _merge_ffn(y_gelu, o_attn, gates, x, w_glu, w_ssm, w_attn, w_out, g, wg, wu, wd, final_g, final):
    n_tok = x.shape[0]
    tm = MERGE_TILE

    def rows(width):
        return pl.BlockSpec((tm, width), lambda i: (i, 0))

    return pl.pallas_call(
        functools.partial(_merge_ffn_kernel, final=final),
        grid=(n_tok // tm,),
        in_specs=[pl.BlockSpec((N_OCTETS, _padded_rows(tm), LANES), lambda i: (0, i, 0)),
                  rows(GROUP_WIDTH), rows(2 * D_MODEL), rows(D_MODEL),
                  _const_spec((SSM_WIDTH, 2 * SSM_WIDTH)), _const_spec((SSM_WIDTH, D_MODEL)),
                  _const_spec((GROUP_WIDTH, D_MODEL)), _const_spec((D_MODEL, D_MODEL))]
                 + _ffn_weight_specs() + [_const_spec((1, D_MODEL))],
        out_specs=rows(D_MODEL),
        out_shape=jax.ShapeDtypeStruct((n_tok, D_MODEL), F32),
        scratch_shapes=_ffn_weight_scratch(),
        compiler_params=pltpu.CompilerParams(
            dimension_semantics=("arbitrary",), vmem_limit_bytes=VMEM_LIMIT_BYTES),
        name="merge_ffn",
    )(y_gelu, o_attn, gates, x, w_glu, w_ssm, w_attn, w_out,
      g.reshape(1, D_MODEL), wg, wu, wd, final_g.reshape(1, D_MODEL))


def kernel(x, ffn1_norm, ffn1_w_gate, ffn1_w_up, ffn1_w_down, mix_norm, w_in, gate_bias,
           rel_bias_table, ssm_a_re, ssm_a_im, ssm_log_dt, ssm_b_re, ssm_b_im, ssm_c_re,
           ssm_c_im, ssm_d, ssm_w_glu, w_attn_branch, w_ssm_branch, w_out, ffn2_norm,
           ffn2_w_gate, ffn2_w_up, ffn2_w_down, final_norm):
    batch, seq, _ = x.shape
    depth = ffn1_norm.shape[0]
    h = x.reshape(batch * seq, D_MODEL)
    for l in range(depth):
        h = _ffn(h, ffn1_norm[l], ffn1_w_gate[l].astype(F32), ffn1_w_up[l].astype(F32),
                 ffn1_w_down[l].astype(F32))
        q, k, v, u, gates = _in_proj(h, mix_norm[l], w_in[l].astype(F32), gate_bias[l], batch, seq)
        o_attn = _attention(q, k, v, rel_bias_table, batch, seq)
        mats = _s5_matrices(ssm_a_re[l], ssm_a_im[l], ssm_log_dt[l], ssm_b_re[l], ssm_b_im[l],
                            ssm_c_re[l], ssm_c_im[l], ssm_d[l])
        y_gelu = _s5(u, mats, batch, seq)
        h = _merge_ffn(y_gelu, o_attn, gates, h, ssm_w_glu[l].astype(BF16),
                       w_ssm_branch[l].astype(BF16), w_attn_branch[l].astype(BF16),
                       w_out[l].astype(BF16), ffn2_norm[l], ffn2_w_gate[l].astype(F32),
                       ffn2_w_up[l].astype(F32), ffn2_w_down[l].astype(F32), final_norm,
                       final=(l == depth - 1))
    return h.reshape(batch, seq, D_MODEL)
```

```python
import functools
import math

import jax
import jax.numpy as jnp
import numpy as np
from jax import lax
from jax.experimental import pallas as pl
from jax.experimental.pallas import tpu as pltpu

F32 = jnp.float32
BF16 = jnp.bfloat16

D_MODEL = 1024
D_FF = 2816
EPS = 1e-6
HEAD_DIM = 64
HEADS_PER_GROUP = 4
GROUP_WIDTH = HEADS_PER_GROUP * HEAD_DIM
DILATIONS = (1, 4, 16)
WINDOWS = (128, 512, 2048)
N_GROUPS = len(DILATIONS)
ATTN_WIDTH = N_GROUPS * GROUP_WIDTH
BLOCK = 128
N_BUCKETS = 32
MAX_DISTANCE = 2048
NEG_INF = -1e30
SSM_GROUP = 16
SSM_WIDTH = 512
SSM_GROUPS = SSM_WIDTH // SSM_GROUP
SSM_STATE = 64
IN_WIDTH = 3 * ATTN_WIDTH + SSM_WIDTH + 2 * D_MODEL

LANES = 128
MXU_WIDTH = 256
VMEM_LIMIT_BYTES = 56 * 1024 * 1024

FFN_TILE = 1024
PROJ_TILE = 1024
MERGE_TILE = 512
WEIGHT_STAGE_STEPS = 8
ATTN_TILE = BLOCK * DILATIONS[-1]
ATTN_INTERLEAVE = 8
SCATTER_STRIDE = 4
SSM_CHUNK = 16
CHUNK_PITCH = SSM_CHUNK + 4
SSM_OCTET = LANES // SSM_GROUP
N_OCTETS = SSM_GROUPS // SSM_OCTET
OCTET_STATE = SSM_OCTET * SSM_STATE


def _rms(x, g):
    return x * lax.rsqrt(jnp.mean(x * x, axis=-1, keepdims=True) + EPS) * g


def _const_spec(shape):
    return pl.BlockSpec(shape, lambda *_: (0,) * len(shape), pipeline_mode=pl.Buffered(1))


def _swiglu_residual(x, g, wg_ref, wu_ref, wd_ref):
    h = _rms(x, g).astype(BF16)
    acc = jnp.zeros(x.shape, F32)
    for c in range(D_FF // MXU_WIDTH):
        sl = slice(c * MXU_WIDTH, (c + 1) * MXU_WIDTH)
        a = jnp.dot(h, wg_ref[:, sl], preferred_element_type=F32)
        b = jnp.dot(h, wu_ref[:, sl], preferred_element_type=F32)
        t = (a * jax.nn.sigmoid(a) * b).astype(BF16)
        acc = acc + jnp.dot(t, wd_ref[sl, :], preferred_element_type=F32)
    return x + 0.5 * acc


def _weight_copy(w_hbm, stage, sems, first_sem, i):
    rows = stage.shape[1]
    return pltpu.make_async_copy(w_hbm.at[pl.ds(i * rows, rows), :], stage.at[i % 2],
                                 sems.at[first_sem + i % 2])


def _fetch_cast(streams, sems):
    steps = {w_hbm.shape[0] // stage.shape[1] for w_hbm, _, stage in streams}
    (n,) = steps
    for k, (w_hbm, _, stage) in enumerate(streams):
        _weight_copy(w_hbm, stage, sems, 2 * k, 0).start()
    for i in range(n):
        for k, (w_hbm, w_scr, stage) in enumerate(streams):
            if i + 1 < n:
                _weight_copy(w_hbm, stage, sems, 2 * k, i + 1).start()
            _weight_copy(w_hbm, stage, sems, 2 * k, i).wait()
            rows = stage.shape[1]
            w_scr[i * rows:(i + 1) * rows, :] = stage[i % 2].astype(BF16)


def _fetch_ffn_weights(wg_hbm, wu_hbm, wd_hbm, wg_scr, wu_scr, wd_scr, gate_stage, up_stage, down_stage, sems):
    _fetch_cast([(wg_hbm, wg_scr, gate_stage), (wu_hbm, wu_scr, up_stage), (wd_hbm, wd_scr, down_stage)],
                sems)


def _ffn_kernel(x_ref, g_ref, wg_hbm, wu_hbm, wd_hbm, o_ref, *weight_scratch):
    @pl.when(pl.program_id(0) == 0)
    def _():
        _fetch_ffn_weights(wg_hbm, wu_hbm, wd_hbm, *weight_scratch)

    wg_scr, wu_scr, wd_scr = weight_scratch[0:3]
    o_ref[...] = _swiglu_residual(x_ref[...], g_ref[...], wg_scr, wu_scr, wd_scr)


def _ffn_weight_specs():
    hbm = pl.BlockSpec(memory_space=pl.ANY)
    return [_const_spec((1, D_MODEL)), hbm, hbm, hbm]


def _ffn_weight_scratch():
    return [pltpu.VMEM((D_MODEL, D_FF), BF16), pltpu.VMEM((D_MODEL, D_FF), BF16),
            pltpu.VMEM((D_FF, D_MODEL), BF16),
            pltpu.VMEM((2, D_MODEL // WEIGHT_STAGE_STEPS, D_FF), F32),
            pltpu.VMEM((2, D_MODEL // WEIGHT_STAGE_STEPS, D_FF), F32),
            pltpu.VMEM((2, D_FF // WEIGHT_STAGE_STEPS, D_MODEL), F32),
            pltpu.SemaphoreType.DMA((6,))]


def _ffn(x, g, wg, wu, wd):
    n_tok = x.shape[0]
    tm = FFN_TILE
    tile = pl.BlockSpec((tm, D_MODEL), lambda i: (i, 0))
    return pl.pallas_call(
        _ffn_kernel,
        grid=(n_tok // tm,),
        in_specs=[tile] + _ffn_weight_specs(),
        out_specs=tile,
        out_shape=jax.ShapeDtypeStruct((n_tok, D_MODEL), F32),
        scratch_shapes=_ffn_weight_scratch(),
        compiler_params=pltpu.CompilerParams(
            dimension_semantics=("arbitrary",), vmem_limit_bytes=VMEM_LIMIT_BYTES),
        name="ffn",
    )(x, g.reshape(1, D_MODEL), wg, wu, wd)


N_QKV_CHUNKS = 3 * N_GROUPS
N_U_CHUNKS = SSM_WIDTH // MXU_WIDTH
N_GATE_CHUNKS = 2 * D_MODEL // MXU_WIDTH
N_STRIDED = 3 * (N_GROUPS - 1)
N_TWO_HOP = 3 * sum(d > SCATTER_STRIDE for d in DILATIONS)


def _padded_rows(n_tokens):
    return n_tokens // SSM_CHUNK * CHUNK_PITCH


def _store_padded_chunks(ref, slab, rows):
    pad = jnp.zeros((CHUNK_PITCH - SSM_CHUNK, rows.shape[1]), rows.dtype)
    for ch in range(rows.shape[0] // SSM_CHUNK):
        base = ch * CHUNK_PITCH
        ref[slab, base:base + SSM_CHUNK, :] = rows[ch * SSM_CHUNK:(ch + 1) * SSM_CHUNK, :]
        ref[slab, base + SSM_CHUNK:base + CHUNK_PITCH, :] = pad


def _load_padded_chunks(ref, slab, n_tokens):
    return jnp.concatenate(
        [ref[slab, ch * CHUNK_PITCH:ch * CHUNK_PITCH + SSM_CHUNK, :]
         for ch in range(n_tokens // SSM_CHUNK)], axis=0)


def _inproj_kernel(x_ref, g_ref, w_hbm, gb_ref, *refs):
    qkv_refs = refs[:N_QKV_CHUNKS]
    u_ref, gate_ref, slab_ref, hop_ref, w_ref, w_stage, w_sem = refs[N_QKV_CHUNKS:]
    tm = x_ref.shape[0]

    @pl.when(pl.program_id(0) == 0)
    def _():
        half = D_MODEL // 2
        _fetch_cast([(w_hbm.at[pl.ds(s * half, half), :], w_ref.at[pl.ds(s * half, half), :],
                      w_stage.at[s]) for s in range(2)], w_sem)

    h = _rms(x_ref[...], g_ref[...]).astype(BF16)
    strided_idx = 0
    hop_idx = 0
    heavy = list(range(N_QKV_CHUNKS + N_U_CHUNKS))
    light = list(range(N_QKV_CHUNKS + N_U_CHUNKS, IN_WIDTH // MXU_WIDTH))
    order = []
    while heavy or light:
        if heavy:
            order.append(heavy.pop(0))
        if light:
            order.append(light.pop(0))
    for c in order:
        z = jnp.dot(h, w_ref[:, c * MXU_WIDTH:(c + 1) * MXU_WIDTH], preferred_element_type=F32)
        if c < N_QKV_CHUNKS:
            kind, grp = divmod(c, N_GROUPS)
            if kind == 0:
                z = z * (HEAD_DIM ** -0.5)
            d = DILATIONS[grp]
            o_ref = qkv_refs[c]
            if d == 1:
                o_ref[0, 0] = z.astype(BF16)
            else:
                base = 2 * strided_idx
                strided_idx += 1
                for s in range(2):
                    slab_ref[base + s] = z[:, s * LANES:(s + 1) * LANES]
                lanes = [slice(s * LANES, (s + 1) * LANES) for s in range(2)]
                if d <= SCATTER_STRIDE:
                    for r in range(d):
                        for s in range(2):
                            o_ref[0, r, :, lanes[s]] = (
                                slab_ref[base + s, pl.ds(r, tm // d, stride=d), :].astype(BF16))
                else:
                    hop = tm // SCATTER_STRIDE
                    hop_base = 2 * hop_idx
                    hop_idx += 1
                    for s in range(2):
                        for low in range(SCATTER_STRIDE):
                            hop_ref[hop_base + s, pl.ds(low * hop, hop), :] = (
                                slab_ref[base + s, pl.ds(low, hop, stride=SCATTER_STRIDE), :])
                    inner = d // SCATTER_STRIDE
                    for r in range(d):
                        low, high = r % SCATTER_STRIDE, r // SCATTER_STRIDE
                        for s in range(2):
                            o_ref[0, r, :, lanes[s]] = hop_ref[
                                hop_base + s, pl.ds(low * hop + high, tm // d, stride=inner), :
                            ].astype(BF16)
        elif c < N_QKV_CHUNKS + N_U_CHUNKS:
            j = c - N_QKV_CHUNKS
            for s in range(2):
                _store_padded_chunks(u_ref, 2 * j + s, z[:, s * LANES:(s + 1) * LANES])
        else:
            j = c - N_QKV_CHUNKS - N_U_CHUNKS
            sl = slice(j * MXU_WIDTH, (j + 1) * MXU_WIDTH)
            gate_ref[:, sl] = jax.nn.sigmoid(z + gb_ref[:, sl]).astype(BF16)


def _in_proj(x, g, w_in, gate_bias, batch, seq):
    n_tok = x.shape[0]
    tm = PROJ_TILE
    tiles_per_seq = seq // tm
    out_shapes, out_specs = [], []
    for _ in range(3):
        for d in DILATIONS:
            out_shapes.append(jax.ShapeDtypeStruct((batch, d, seq // d, GROUP_WIDTH), BF16))
            out_specs.append(pl.BlockSpec(
                (1, d, tm // d, GROUP_WIDTH),
                lambda i: (i // tiles_per_seq, 0, i % tiles_per_seq, 0)))
    out_shapes.append(jax.ShapeDtypeStruct((N_OCTETS, _padded_rows(n_tok), LANES), F32))
    out_specs.append(pl.BlockSpec((N_OCTETS, _padded_rows(tm), LANES), lambda i: (0, i, 0)))
    out_shapes.append(jax.ShapeDtypeStruct((n_tok, 2 * D_MODEL), BF16))
    out_specs.append(pl.BlockSpec((tm, 2 * D_MODEL), lambda i: (i, 0)))
    outs = pl.pallas_call(
        _inproj_kernel,
        grid=(n_tok // tm,),
        in_specs=[pl.BlockSpec((tm, D_MODEL), lambda i: (i, 0)),
                  _const_spec((1, D_MODEL)),
                  pl.BlockSpec(memory_space=pl.ANY),
                  _const_spec((1, 2 * D_MODEL))],
        out_specs=out_specs,
        out_shape=out_shapes,
        scratch_shapes=[pltpu.VMEM((2 * N_STRIDED, tm, LANES), F32),
                        pltpu.VMEM((2 * N_TWO_HOP, tm, LANES), F32),
                        pltpu.VMEM((D_MODEL, IN_WIDTH), BF16),
                        pltpu.VMEM((2, 2, D_MODEL // (2 * WEIGHT_STAGE_STEPS), IN_WIDTH), F32),
                        pltpu.SemaphoreType.DMA((4,))],
        compiler_params=pltpu.CompilerParams(
            dimension_semantics=("arbitrary",), vmem_limit_bytes=VMEM_LIMIT_BYTES),
        name="in_proj",
    )(x, g.reshape(1, D_MODEL), w_in, gate_bias.reshape(1, 2 * D_MODEL))
    q, k, v = outs[0:3], outs[3:6], outs[6:9]
    return q, k, v, outs[9], outs[10]


def _t5_bucket_np(dist):
    max_exact = N_BUCKETS // 2
    d = np.maximum(dist, 1).astype(np.float32)
    ratio = np.log(d / np.float32(max_exact)) / np.float32(math.log(MAX_DISTANCE / max_exact))
    large = max_exact + (ratio * np.float32(N_BUCKETS - max_exact)).astype(np.int32)
    large = np.minimum(large, N_BUCKETS - 1)
    return np.where(dist < max_exact, dist, large)


def _bucket_tables():
    qi = np.arange(BLOCK)[:, None]
    kj = np.arange(2 * BLOCK)[None, :]
    steps = qi + BLOCK - kj
    tables = []
    for window, d in zip(WINDOWS, DILATIONS):
        band = (steps >= 0) & (steps <= window // d)
        bucket = _t5_bucket_np(np.maximum(steps, 0) * d)
        tables.append(np.where(band, bucket, -1).astype(np.int32))
    return jnp.asarray(np.stack(tables))


def _attn_block(q_blk, k_blk, v_blk, bias):
    lane_head = lax.broadcasted_iota(jnp.int32, (BLOCK, GROUP_WIDTH), 1) // HEAD_DIM
    zero = jnp.zeros_like(q_blk)
    qs = jnp.concatenate(
        [jnp.where(lane_head == h, q_blk, zero) for h in range(HEADS_PER_GROUP)], axis=0)
    logits = lax.dot_general(qs, k_blk, (((1,), (1,)), ((), ())), preferred_element_type=F32) + bias
    m = jnp.max(logits, axis=-1, keepdims=True)
    p = jnp.exp(logits - m)
    l = jnp.sum(p, axis=-1, keepdims=True)
    pv = jnp.dot(p.astype(BF16), v_blk, preferred_element_type=F32)
    o = jnp.zeros((BLOCK, GROUP_WIDTH), F32)
    den = jnp.ones((BLOCK, GROUP_WIDTH), F32)
    mx = jnp.zeros((BLOCK, GROUP_WIDTH), F32)
    for h in range(HEADS_PER_GROUP):
        rows = slice(h * BLOCK, (h + 1) * BLOCK)
        sel = lane_head == h
        o = jnp.where(sel, pv[rows], o)
        den = jnp.where(sel, l[rows], den)
        mx = jnp.where(sel, m[rows], mx)
    return o / den, mx + jnp.log(den)


def _build_bias(tab_ref, bucket_ref, bias_scr):
    own_block = lax.broadcasted_iota(jnp.int32, (BLOCK, 2 * BLOCK), 1) >= BLOCK
    for grp in range(N_GROUPS):
        bucket = bucket_ref[grp]
        for h in range(HEADS_PER_GROUP):
            col = grp * HEADS_PER_GROUP + h

            def pick(b, acc, bucket=bucket, col=col):
                return jnp.where(bucket == b, tab_ref[b, col], acc)

            bias = lax.fori_loop(0, N_BUCKETS, pick, jnp.full((BLOCK, 2 * BLOCK), NEG_INF, F32))
            rows = slice(h * BLOCK, (h + 1) * BLOCK)
            bias_scr[grp, 0, rows, :] = bias
            bias_scr[grp, 1, rows, :] = jnp.where(own_block, bias, NEG_INF)


def _block_aligned(row):
    return row if isinstance(row, int) else pl.multiple_of(row, BLOCK)


def _attn_kernel(*refs):
    n = N_GROUPS
    tab_ref, bucket_ref = refs[0:2]
    refs = refs[2:]
    q_refs, kc_refs, kp_refs = refs[0:n], refs[n:2 * n], refs[2 * n:3 * n]
    vc_refs, vp_refs = refs[3 * n:4 * n], refs[4 * n:5 * n]
    o_ref, bias_scr, o_scr, l_scr, t_scr = refs[5 * n:]
    tile = pl.program_id(1)

    @pl.when((pl.program_id(0) == 0) & (tile == 0))
    def _():
        _build_bias(tab_ref, bucket_ref, bias_scr)

    for grp, d in enumerate(DILATIONS):
        q_ref, kc_ref, kp_ref = q_refs[grp], kc_refs[grp], kp_refs[grp]
        vc_ref, vp_ref = vc_refs[grp], vp_refs[grp]
        blocks_per_residue = ATTN_TILE // d // BLOCK

        def run_block(r, blk, grp=grp, d=d, q_ref=q_ref, kc_ref=kc_ref, kp_ref=kp_ref,
                      vc_ref=vc_ref, vp_ref=vp_ref):
            if isinstance(blk, int) and blk == 0:
                k_blk = jnp.concatenate([kp_ref[0, r], kc_ref[0, r, 0:BLOCK, :]], axis=0)
                v_blk = jnp.concatenate([vp_ref[0, r], vc_ref[0, r, 0:BLOCK, :]], axis=0)
                bias = jnp.where(tile == 0, bias_scr[grp, 1], bias_scr[grp, 0])
                q_blk = q_ref[0, r, 0:BLOCK, :]
            else:
                kv_rows = pl.ds(_block_aligned((blk - 1) * BLOCK), 2 * BLOCK)
                k_blk = kc_ref[0, r, kv_rows, :]
                v_blk = vc_ref[0, r, kv_rows, :]
                bias = bias_scr[grp, 0]
                q_blk = q_ref[0, r, pl.ds(_block_aligned(blk * BLOCK), BLOCK), :]
            o, lse = _attn_block(q_blk, k_blk, v_blk, bias)
            if d <= SCATTER_STRIDE:
                start = blk * (BLOCK * d) + r
                rows = pl.ds(start, BLOCK, stride=d) if d > 1 else pl.ds(start, BLOCK)
                for s in range(2):
                    o_scr[grp, s, rows, :] = o[:, s * LANES:(s + 1) * LANES]
                    l_scr[grp, s, rows, :] = lse[:, s * LANES:(s + 1) * LANES]
            else:
                inner = d // SCATTER_STRIDE
                low, high = lax.rem(r, SCATTER_STRIDE), r // SCATTER_STRIDE
                rows = pl.ds(blk * (BLOCK * inner) + high, BLOCK, stride=inner)
                for s in range(2):
                    t_scr[0, s, low, rows, :] = o[:, s * LANES:(s + 1) * LANES]
                    t_scr[1, s, low, rows, :] = lse[:, s * LANES:(s + 1) * LANES]

        if blocks_per_residue == 1:
            def residue_set(i, c, run_block=run_block):
                for k in range(ATTN_INTERLEAVE):
                    run_block(ATTN_INTERLEAVE * i + k, 0)
                return c
            lax.fori_loop(0, d // ATTN_INTERLEAVE, residue_set, 0)
        elif d > 1:
            residues_per_set = max(1, ATTN_INTERLEAVE // blocks_per_residue)

            def residue_set(i, c, run_block=run_block, blocks_per_residue=blocks_per_residue,
                            residues_per_set=residues_per_set):
                for k in range(residues_per_set):
                    for blk in range(blocks_per_residue):
                        run_block(residues_per_set * i + k, blk)
                return c
            lax.fori_loop(0, d // residues_per_set, residue_set, 0)
        else:
            for blk in range(ATTN_INTERLEAVE):
                run_block(0, blk)

            def block_set(i, c, run_block=run_block):
                for k in range(ATTN_INTERLEAVE):
                    run_block(0, ATTN_INTERLEAVE * i + k)
                return c
            lax.fori_loop(1, blocks_per_residue // ATTN_INTERLEAVE, block_set, 0)

        if d > SCATTER_STRIDE:
            def unstage(low, c, grp=grp):
                rows = pl.ds(low, ATTN_TILE // SCATTER_STRIDE, stride=SCATTER_STRIDE)
                for s in range(2):
                    o_scr[grp, s, rows, :] = t_scr[0, s, low]
                    l_scr[grp, s, rows, :] = t_scr[1, s, low]
                return c
            lax.fori_loop(0, SCATTER_STRIDE, unstage, 0)

    merge_rows = 256

    def merge(c, carry):
        rows = pl.ds(pl.multiple_of(c * merge_rows, merge_rows), merge_rows)
        for s in range(2):
            lses = [l_scr[grp, s, rows, :] for grp in range(N_GROUPS)]
            mx = functools.reduce(jnp.maximum, lses)
            es = [jnp.exp(x - mx) for x in lses]
            num = sum(e * o_scr[grp, s, rows, :] for grp, e in enumerate(es))
            o_ref[rows, s * LANES:(s + 1) * LANES] = (num / sum(es)).astype(BF16)
        return carry

    lax.fori_loop(0, ATTN_TILE // merge_rows, merge, 0)


def _attention(q, k, v, rel_bias_table, batch, seq):
    tiles = seq // ATTN_TILE

    def cur_spec(d):
        return pl.BlockSpec((1, d, ATTN_TILE // d, GROUP_WIDTH), lambda b, j: (b, 0, j, 0))

    def prev_spec(d):
        per_tile = ATTN_TILE // d // BLOCK
        return pl.BlockSpec((1, d, BLOCK, GROUP_WIDTH),
                            lambda b, j: (b, 0, jnp.maximum(j * per_tile - 1, 0), 0))

    in_specs = ([pl.BlockSpec(memory_space=pltpu.SMEM),
                 _const_spec((N_GROUPS, BLOCK, 2 * BLOCK))]
                + [cur_spec(d) for d in DILATIONS] + [cur_spec(d) for d in DILATIONS]
                + [prev_spec(d) for d in DILATIONS] + [cur_spec(d) for d in DILATIONS]
                + [prev_spec(d) for d in DILATIONS])
    return pl.pallas_call(
        _attn_kernel,
        grid=(batch, tiles),
        in_specs=in_specs,
        out_specs=pl.BlockSpec((ATTN_TILE, GROUP_WIDTH), lambda b, j: (b * tiles + j, 0)),
        out_shape=jax.ShapeDtypeStruct((batch * seq, GROUP_WIDTH), BF16),
        scratch_shapes=[pltpu.VMEM((N_GROUPS, 2, HEADS_PER_GROUP * BLOCK, 2 * BLOCK), F32),
                        pltpu.VMEM((N_GROUPS, 2, ATTN_TILE, LANES), F32),
                        pltpu.VMEM((N_GROUPS, 2, ATTN_TILE, LANES), F32),
                        pltpu.VMEM((2, 2, SCATTER_STRIDE, ATTN_TILE // SCATTER_STRIDE, LANES), F32)],
        compiler_params=pltpu.CompilerParams(
            dimension_semantics=("arbitrary", "arbitrary"), vmem_limit_bytes=VMEM_LIMIT_BYTES),
        name="dilated_attn",
    )(rel_bias_table.astype(F32), _bucket_tables(), *q, *k, *k, *v, *v)


def _s5_matrices(a_re, a_im, log_dt, b_re, b_im, c_re, c_im, d_skip):
    tc = SSM_CHUNK
    lam_re = a_re.astype(F32)
    lam_im = a_im.astype(F32)
    dt = jnp.exp(log_dt.astype(F32))[:, None]
    mag = jnp.exp(lam_re * dt)
    ab_re = mag * jnp.cos(lam_im * dt)
    ab_im = mag * jnp.sin(lam_im * dt)
    den = lam_re * lam_re + lam_im * lam_im
    xr = ab_re - 1.0
    coef_re = (xr * lam_re + ab_im * lam_im) / den
    coef_im = (ab_im * lam_re - xr * lam_im) / den
    br = b_re.astype(F32)
    bi = b_im.astype(F32)
    bb_re = coef_re[..., None] * br - coef_im[..., None] * bi
    bb_im = coef_re[..., None] * bi + coef_im[..., None] * br
    cr = c_re.astype(F32)
    ci = c_im.astype(F32)

    def a_pow(k):
        kk = k.astype(F32)[:, None, None]
        pm = jnp.exp(lam_re * dt * kk)
        return pm * jnp.cos(lam_im * dt * kk), pm * jnp.sin(lam_im * dt * kk)

    pw_re, pw_im = a_pow(jnp.arange(tc + 1))
    pr, pi = pw_re[:tc], pw_im[:tc]

    def by_octet(x):
        return x.reshape(x.shape[0], N_OCTETS, 1, SSM_OCTET, SSM_STATE)

    def channel_major(x):
        x = x.reshape(N_OCTETS, SSM_OCTET, SSM_GROUP, SSM_STATE)
        return jnp.transpose(x, (0, 2, 1, 3))[None]

    def compact(re, im):
        both = jnp.stack([re, im], axis=1)
        return both.reshape(re.shape[0], 2, N_OCTETS, SSM_GROUP, OCTET_STATE).astype(BF16)

    lr, li = by_octet(pr[::-1]), by_octet(pi[::-1])
    bt_re = channel_major(jnp.transpose(bb_re, (0, 2, 1)))
    bt_im = channel_major(jnp.transpose(bb_im, (0, 2, 1)))
    p_src = compact(lr * bt_re - li * bt_im, lr * bt_im + li * bt_re)

    qr, qi = by_octet(pw_re), by_octet(pw_im)
    ct_re, ct_im = channel_major(cr), channel_major(ci)
    q_src = compact(ct_re * qr - ct_im * qi, -(ct_re * qi + ct_im * qr))

    a_chunk = jnp.concatenate([pw_re[tc].reshape(N_OCTETS, 1, OCTET_STATE),
                               pw_im[tc].reshape(N_OCTETS, 1, OCTET_STATE)], axis=-1)
    skip = d_skip.astype(F32).reshape(N_OCTETS, 1, LANES)
    return p_src, q_src, a_chunk, skip


def _s5_kernel(u_ref, p_ref, q_ref, a_ref, d_ref, o_ref, k_scr, p_scr, qt_scr, x_scr, s_scr, z_scr):
    n_chunks = x_scr.shape[0]
    tc = SSM_CHUNK

    @pl.when(pl.program_id(1) == 0)
    def _():
        row_group = lax.broadcasted_iota(jnp.int32, (LANES, OCTET_STATE), 0) // SSM_GROUP
        lane_group = lax.broadcasted_iota(jnp.int32, (LANES, OCTET_STATE), 1) // SSM_STATE
        same_group = row_group == lane_group
        zeros = jnp.zeros((LANES, OCTET_STATE), BF16)

        def block_diagonal(src_ref, t):
            halves = []
            for part in range(2):
                tiled = jnp.concatenate([src_ref[t, part, 0]] * SSM_OCTET, axis=0)
                halves.append(jnp.where(same_group, tiled, zeros))
            return jnp.concatenate(halves, axis=1)

        for t in range(tc):
            p_scr[t * LANES:(t + 1) * LANES, :] = block_diagonal(p_ref, t)
            qt_scr[t * LANES:(t + 1) * LANES, :] = block_diagonal(q_ref, t + 1)
        q_now = block_diagonal(q_ref, 0)
        diagonal = (lax.broadcasted_iota(jnp.int32, (LANES, LANES), 0)
                    == lax.broadcasted_iota(jnp.int32, (LANES, LANES), 1))
        k_scr[...] = jnp.zeros(k_scr.shape, BF16)
        for lag in range(tc):
            sigma = tc - 1 - lag
            resp = lax.dot_general(p_scr[sigma * LANES:(sigma + 1) * LANES, :], q_now,
                                   (((1,), (1,)), ((), ())), preferred_element_type=F32)
            if lag == 0:
                resp = resp + jnp.where(diagonal, d_ref[0], 0.0)
            resp = resp.astype(BF16)
            for s in range(tc - lag):
                t = s + lag
                k_scr[s * LANES:(s + 1) * LANES, t * LANES:(t + 1) * LANES] = resp

    inc = None
    for pair in range(tc // 2):
        for t in (2 * pair, 2 * pair + 1):
            x_scr[:, t * LANES:(t + 1) * LANES] = (
                u_ref[0, pl.ds(t, n_chunks, stride=CHUNK_PITCH), :].astype(BF16))
        rows = slice(pair * MXU_WIDTH, (pair + 1) * MXU_WIDTH)
        part = jnp.dot(x_scr[:, rows], p_scr[rows, :], preferred_element_type=F32)
        inc = part if inc is None else inc + part
    s_scr[...] = inc

    a_re = a_ref[0, :, 0:OCTET_STATE]
    a_im = a_ref[0, :, OCTET_STATE:2 * OCTET_STATE]

    def step(j, carry):
        c_re, c_im = carry
        row = pl.ds(j, 1)
        z_scr[row, 0:OCTET_STATE] = c_re
        z_scr[row, OCTET_STATE:2 * OCTET_STATE] = c_im
        inc_re = s_scr[row, 0:OCTET_STATE]
        inc_im = s_scr[row, OCTET_STATE:2 * OCTET_STATE]
        return (a_re * c_re - a_im * c_im + inc_re, a_re * c_im + a_im * c_re + inc_im)

    zero = jnp.zeros((1, OCTET_STATE), F32)
    lax.fori_loop(0, n_chunks, step, (zero, zero), unroll=8)

    zb = z_scr[...].astype(BF16)
    for pair in range(tc // 2):
        cols = slice(pair * MXU_WIDTH, (pair + 1) * MXU_WIDTH)
        live = (2 * pair + 2) * LANES
        y = (jnp.dot(x_scr[:, 0:live], k_scr[0:live, cols], preferred_element_type=F32)
             + lax.dot_general(zb, qt_scr[cols, :], (((1,), (1,)), ((), ())),
                               preferred_element_type=F32))
        y = jax.nn.gelu(y)
        o_ref[0, pl.ds(2 * pair, n_chunks, stride=CHUNK_PITCH), :] = y[:, 0:LANES]
        o_ref[0, pl.ds(2 * pair + 1, n_chunks, stride=CHUNK_PITCH), :] = y[:, LANES:2 * LANES]
    for t in range(tc, CHUNK_PITCH):
        o_ref[0, pl.ds(t, n_chunks, stride=CHUNK_PITCH), :] = jnp.zeros((n_chunks, LANES), F32)


def _s5(u, mats, batch, seq):
    p_src, q_src, a_chunk, skip = mats
    n_chunks = seq // SSM_CHUNK
    width = SSM_CHUNK * LANES

    def per_octet(shape):
        return pl.BlockSpec((1,) + shape, lambda o, b: (o,) + (0,) * len(shape),
                            pipeline_mode=pl.Buffered(1))

    def compact_spec(entries):
        return pl.BlockSpec((entries, 2, 1, SSM_GROUP, OCTET_STATE), lambda o, b: (0, 0, o, 0, 0))

    return pl.pallas_call(
        _s5_kernel,
        grid=(N_OCTETS, batch),
        in_specs=[pl.BlockSpec((1, _padded_rows(seq), LANES), lambda o, b: (o, b, 0)),
                  compact_spec(SSM_CHUNK),
                  compact_spec(SSM_CHUNK + 1),
                  per_octet((1, 2 * OCTET_STATE)),
                  per_octet((1, LANES))],
        out_specs=pl.BlockSpec((1, _padded_rows(seq), LANES), lambda o, b: (o, b, 0)),
        out_shape=jax.ShapeDtypeStruct((N_OCTETS, _padded_rows(batch * seq), LANES), F32),
        scratch_shapes=[pltpu.VMEM((width, width), BF16),
                        pltpu.VMEM((width, 2 * OCTET_STATE), BF16),
                        pltpu.VMEM((width, 2 * OCTET_STATE), BF16),
                        pltpu.VMEM((n_chunks, width), BF16),
                        pltpu.VMEM((n_chunks, 2 * OCTET_STATE), F32),
                        pltpu.VMEM((n_chunks, 2 * OCTET_STATE), F32)],
        compiler_params=pltpu.CompilerParams(
            dimension_semantics=("arbitrary", "arbitrary"), vmem_limit_bytes=VMEM_LIMIT_BYTES),
        name="s5_mixer",
    )(u, p_src, q_src, a_chunk, skip)


def _merge_ffn_kernel(y_ref, oa_ref, gate_ref, x_ref, wglu_ref, wssm_ref, wattn_ref, wout_ref,
                      g_ref, wg_hbm, wu_hbm, wd_hbm, fn_ref, o_ref, *weight_scratch, final):
    @pl.when(pl.program_id(0) == 0)
    def _():
        _fetch_ffn_weights(wg_hbm, wu_hbm, wd_hbm, *weight_scratch)

    wg_scr, wu_scr, wd_scr = weight_scratch[0:3]
    y_gelu = jnp.concatenate(
        [_load_padded_chunks(y_ref, o, x_ref.shape[0]).astype(BF16) for o in range(N_OCTETS)], axis=1)
    glu = jnp.dot(y_gelu, wglu_ref[...], preferred_element_type=F32)
    ys = (glu[:, 0:SSM_WIDTH] * jax.nn.sigmoid(glu[:, SSM_WIDTH:2 * SSM_WIDTH])).astype(BF16)
    y_ssm = jnp.dot(ys, wssm_ref[...], preferred_element_type=F32)
    y_attn = jnp.dot(oa_ref[...], wattn_ref[...], preferred_element_type=F32)
    g_attn = gate_ref[:, 0:D_MODEL].astype(F32)
    g_ssm = gate_ref[:, D_MODEL:2 * D_MODEL].astype(F32)
    mixed = (g_attn * y_attn + g_ssm * y_ssm).astype(BF16)
    x = x_ref[...] + jnp.dot(mixed, wout_ref[...], preferred_element_type=F32)
    y = _swiglu_residual(x, g_ref[...], wg_scr, wu_scr, wd_scr)
    o_ref[...] = _rms(y, fn_ref[...]) if final else y


def _merge_ffn(y_gelu, o_attn, gates, x, w_glu, w_ssm, w_attn, w_out, g, wg, wu, wd, final_g, final):
    n_tok = x.shape[0]
    tm = MERGE_TILE

    def rows(width):
        return pl.BlockSpec((tm, width), lambda i: (i, 0))

    return pl.pallas_call(
        functools.partial(_merge_ffn_kernel, final=final),
        grid=(n_tok // tm,),
        in_specs=[pl.BlockSpec((N_OCTETS, _padded_rows(tm), LANES), lambda i: (0, i, 0)),
                  rows(GROUP_WIDTH), rows(2 * D_MODEL), rows(D_MODEL),
                  _const_spec((SSM_WIDTH, 2 * SSM_WIDTH)), _const_spec((SSM_WIDTH, D_MODEL)),
                  _const_spec((GROUP_WIDTH, D_MODEL)), _const_spec((D_MODEL, D_MODEL))]
                 + _ffn_weight_specs() + [_const_spec((1, D_MODEL))],
        out_specs=rows(D_MODEL),
        out_shape=jax.ShapeDtypeStruct((n_tok, D_MODEL), F32),
        scratch_shapes=_ffn_weight_scratch(),
        compiler_params=pltpu.CompilerParams(
            dimension_semantics=("arbitrary",), vmem_limit_bytes=VMEM_LIMIT_BYTES),
        name="merge_ffn",
    )(y_gelu, o_attn, gates, x, w_glu, w_ssm, w_attn, w_out,
      g.reshape(1, D_MODEL), wg, wu, wd, final_g.reshape(1, D_MODEL))


def kernel(x, ffn1_norm, ffn1_w_gate, ffn1_w_up, ffn1_w_down, mix_norm, w_in, gate_bias,
           rel_bias_table, ssm_a_re, ssm_a_im, ssm_log_dt, ssm_b_re, ssm_b_im, ssm_c_re,
           ssm_c_im, ssm_d, ssm_w_glu, w_attn_branch, w_ssm_branch, w_out, ffn2_norm,
           ffn2_w_gate, ffn2_w_up, ffn2_w_down, final_norm):
    batch, seq, _ = x.shape
    depth = ffn1_norm.shape[0]
    h = x.reshape(batch * seq, D_MODEL)
    for l in range(depth):
        h = _ffn(h, ffn1_norm[l], ffn1_w_gate[l].astype(F32), ffn1_w_up[l].astype(F32),
                 ffn1_w_down[l].astype(F32))
        q, k, v, u, gates = _in_proj(h, mix_norm[l], w_in[l].astype(F32), gate_bias[l], batch, seq)
        o_attn = _attention(q, k, v, rel_bias_table, batch, seq)
        mats = _s5_matrices(ssm_a_re[l], ssm_a_im[l], ssm_log_dt[l], ssm_b_re[l], ssm_b_im[l],
                            ssm_c_re[l], ssm_c_im[l], ssm_d[l])
        y_gelu = _s5(u, mats, batch, seq)
        h = _merge_ffn(y_gelu, o_attn, gates, h, ssm_w_glu[l].astype(BF16),
                       w_ssm_branch[l].astype(BF16), w_attn_branch[l].astype(BF16),
                       w_out[l].astype(BF16), ffn2_norm[l], ffn2_w_gate[l].astype(F32),
                       ffn2_w_up[l].astype(F32), ffn2_w_down[l].astype(F32), final_norm,
                       final=(l == depth - 1))
    return h.reshape(batch, seq, D_MODEL)
```

```python
import functools
import math

import jax
import jax.numpy as jnp
import numpy as np
from jax import lax
from jax.experimental import pallas as pl
from jax.experimental.pallas import tpu as pltpu

F32 = jnp.float32
BF16 = jnp.bfloat16

D_MODEL = 1024
D_FF = 2816
EPS = 1e-6
HEAD_DIM = 64
HEADS_PER_GROUP = 4
GROUP_WIDTH = HEADS_PER_GROUP * HEAD_DIM
DILATIONS = (1, 4, 16)
WINDOWS = (128, 512, 2048)
N_GROUPS = len(DILATIONS)
ATTN_WIDTH = N_GROUPS * GROUP_WIDTH
BLOCK = 128
N_BUCKETS = 32
MAX_DISTANCE = 2048
NEG_INF = -1e30
SSM_GROUP = 16
SSM_WIDTH = 512
SSM_GROUPS = SSM_WIDTH // SSM_GROUP
SSM_STATE = 64
IN_WIDTH = 3 * ATTN_WIDTH + SSM_WIDTH + 2 * D_MODEL

LANES = 128
MXU_WIDTH = 256
VMEM_LIMIT_BYTES = 56 * 1024 * 1024

FFN_TILE = 1024
PROJ_TILE = 1024
MERGE_TILE = 512
WEIGHT_STAGE_STEPS = 8
ATTN_TILE = BLOCK * DILATIONS[-1]
ATTN_INTERLEAVE = 8
SCATTER_STRIDE = 4
SSM_CHUNK = 16
SCAN_SEGMENTS = 4
CHUNK_PITCH = SSM_CHUNK + 4
SSM_OCTET = LANES // SSM_GROUP
N_OCTETS = SSM_GROUPS // SSM_OCTET
OCTET_STATE = SSM_OCTET * SSM_STATE


def _rms(x, g):
    return x * lax.rsqrt(jnp.mean(x * x, axis=-1, keepdims=True) + EPS) * g


def _const_spec(shape):
    return pl.BlockSpec(shape, lambda *_: (0,) * len(shape), pipeline_mode=pl.Buffered(1))


def _swiglu_residual(x, g, wg_ref, wu_ref, wd_ref):
    h = _rms(x, g).astype(BF16)
    acc = jnp.zeros(x.shape, F32)
    for c in range(D_FF // MXU_WIDTH):
        sl = slice(c * MXU_WIDTH, (c + 1) * MXU_WIDTH)
        a = jnp.dot(h, wg_ref[:, sl], preferred_element_type=F32)
        b = jnp.dot(h, wu_ref[:, sl], preferred_element_type=F32)
        t = (a * jax.nn.sigmoid(a) * b).astype(BF16)
        acc = acc + jnp.dot(t, wd_ref[sl, :], preferred_element_type=F32)
    return x + 0.5 * acc


def _weight_copy(w_hbm, stage, sems, first_sem, i):
    rows = stage.shape[1]
    return pltpu.make_async_copy(w_hbm.at[pl.ds(i * rows, rows), :], stage.at[i % 2],
                                 sems.at[first_sem + i % 2])


def _fetch_cast(streams, sems):
    steps = {w_hbm.shape[0] // stage.shape[1] for w_hbm, _, stage in streams}
    (n,) = steps
    for k, (w_hbm, _, stage) in enumerate(streams):
        _weight_copy(w_hbm, stage, sems, 2 * k, 0).start()
    for i in range(n):
        for k, (w_hbm, w_scr, stage) in enumerate(streams):
            if i + 1 < n:
                _weight_copy(w_hbm, stage, sems, 2 * k, i + 1).start()
            _weight_copy(w_hbm, stage, sems, 2 * k, i).wait()
            rows = stage.shape[1]
            w_scr[i * rows:(i + 1) * rows, :] = stage[i % 2].astype(BF16)


def _fetch_ffn_weights(wg_hbm, wu_hbm, wd_hbm, wg_scr, wu_scr, wd_scr, gate_stage, up_stage, down_stage, sems):
    _fetch_cast([(wg_hbm, wg_scr, gate_stage), (wu_hbm, wu_scr, up_stage), (wd_hbm, wd_scr, down_stage)],
                sems)


def _ffn_kernel(x_ref, g_ref, wg_hbm, wu_hbm, wd_hbm, o_ref, *weight_scratch):
    @pl.when(pl.program_id(0) == 0)
    def _():
        _fetch_ffn_weights(wg_hbm, wu_hbm, wd_hbm, *weight_scratch)

    wg_scr, wu_scr, wd_scr = weight_scratch[0:3]
    o_ref[...] = _swiglu_residual(x_ref[...], g_ref[...], wg_scr, wu_scr, wd_scr)


def _ffn_weight_specs():
    hbm = pl.BlockSpec(memory_space=pl.ANY)
    return [_const_spec((1, D_MODEL)), hbm, hbm, hbm]


def _ffn_weight_scratch():
    return [pltpu.VMEM((D_MODEL, D_FF), BF16), pltpu.VMEM((D_MODEL, D_FF), BF16),
            pltpu.VMEM((D_FF, D_MODEL), BF16),
            pltpu.VMEM((2, D_MODEL // WEIGHT_STAGE_STEPS, D_FF), F32),
            pltpu.VMEM((2, D_MODEL // WEIGHT_STAGE_STEPS, D_FF), F32),
            pltpu.VMEM((2, D_FF // WEIGHT_STAGE_STEPS, D_MODEL), F32),
            pltpu.SemaphoreType.DMA((6,))]


def _ffn(x, g, wg, wu, wd):
    n_tok = x.shape[0]
    tm = FFN_TILE
    tile = pl.BlockSpec((tm, D_MODEL), lambda i: (i, 0))
    return pl.pallas_call(
        _ffn_kernel,
        grid=(n_tok // tm,),
        in_specs=[tile] + _ffn_weight_specs(),
        out_specs=tile,
        out_shape=jax.ShapeDtypeStruct((n_tok, D_MODEL), F32),
        scratch_shapes=_ffn_weight_scratch(),
        compiler_params=pltpu.CompilerParams(
            dimension_semantics=("arbitrary",), vmem_limit_bytes=VMEM_LIMIT_BYTES),
        name="ffn",
    )(x, g.reshape(1, D_MODEL), wg, wu, wd)


N_QKV_CHUNKS = 3 * N_GROUPS
N_U_CHUNKS = SSM_WIDTH // MXU_WIDTH
N_GATE_CHUNKS = 2 * D_MODEL // MXU_WIDTH
N_STRIDED = 3 * (N_GROUPS - 1)
N_TWO_HOP = 3 * sum(d > SCATTER_STRIDE for d in DILATIONS)


def _padded_rows(n_tokens):
    return n_tokens // SSM_CHUNK * CHUNK_PITCH


def _store_padded_chunks(ref, slab, rows):
    pad = jnp.zeros((CHUNK_PITCH - SSM_CHUNK, rows.shape[1]), rows.dtype)
    for ch in range(rows.shape[0] // SSM_CHUNK):
        base = ch * CHUNK_PITCH
        ref[slab, base:base + SSM_CHUNK, :] = rows[ch * SSM_CHUNK:(ch + 1) * SSM_CHUNK, :]
        ref[slab, base + SSM_CHUNK:base + CHUNK_PITCH, :] = pad


def _load_padded_chunks(ref, slab, n_tokens):
    return jnp.concatenate(
        [ref[slab, ch * CHUNK_PITCH:ch * CHUNK_PITCH + SSM_CHUNK, :]
         for ch in range(n_tokens // SSM_CHUNK)], axis=0)


def _inproj_kernel(x_ref, g_ref, w_hbm, gb_ref, *refs):
    qkv_refs = refs[:N_QKV_CHUNKS]
    u_ref, gate_ref, slab_ref, hop_ref, w_ref, w_stage, w_sem = refs[N_QKV_CHUNKS:]
    tm = x_ref.shape[0]

    @pl.when(pl.program_id(0) == 0)
    def _():
        half = D_MODEL // 2
        _fetch_cast([(w_hbm.at[pl.ds(s * half, half), :], w_ref.at[pl.ds(s * half, half), :],
                      w_stage.at[s]) for s in range(2)], w_sem)

    h = _rms(x_ref[...], g_ref[...]).astype(BF16)
    strided_idx = 0
    hop_idx = 0
    heavy = list(range(N_QKV_CHUNKS + N_U_CHUNKS))
    light = list(range(N_QKV_CHUNKS + N_U_CHUNKS, IN_WIDTH // MXU_WIDTH))
    order = []
    while heavy or light:
        if heavy:
            order.append(heavy.pop(0))
        if light:
            order.append(light.pop(0))
    for c in order:
        z = jnp.dot(h, w_ref[:, c * MXU_WIDTH:(c + 1) * MXU_WIDTH], preferred_element_type=F32)
        if c < N_QKV_CHUNKS:
            kind, grp = divmod(c, N_GROUPS)
            if kind == 0:
                z = z * (HEAD_DIM ** -0.5)
            d = DILATIONS[grp]
            o_ref = qkv_refs[c]
            if d == 1:
                o_ref[0, 0] = z.astype(BF16)
            else:
                base = 2 * strided_idx
                strided_idx += 1
                for s in range(2):
                    slab_ref[base + s] = z[:, s * LANES:(s + 1) * LANES]
                lanes = [slice(s * LANES, (s + 1) * LANES) for s in range(2)]
                if d <= SCATTER_STRIDE:
                    for r in range(d):
                        for s in range(2):
                            o_ref[0, r, :, lanes[s]] = (
                                slab_ref[base + s, pl.ds(r, tm // d, stride=d), :].astype(BF16))
                else:
                    hop = tm // SCATTER_STRIDE
                    hop_base = 2 * hop_idx
                    hop_idx += 1
                    for s in range(2):
                        for low in range(SCATTER_STRIDE):
                            hop_ref[hop_base + s, pl.ds(low * hop, hop), :] = (
                                slab_ref[base + s, pl.ds(low, hop, stride=SCATTER_STRIDE), :])
                    inner = d // SCATTER_STRIDE
                    for r in range(d):
                        low, high = r % SCATTER_STRIDE, r // SCATTER_STRIDE
                        for s in range(2):
                            o_ref[0, r, :, lanes[s]] = hop_ref[
                                hop_base + s, pl.ds(low * hop + high, tm // d, stride=inner), :
                            ].astype(BF16)
        elif c < N_QKV_CHUNKS + N_U_CHUNKS:
            j = c - N_QKV_CHUNKS
            for s in range(2):
                _store_padded_chunks(u_ref, 2 * j + s, z[:, s * LANES:(s + 1) * LANES])
        else:
            j = c - N_QKV_CHUNKS - N_U_CHUNKS
            sl = slice(j * MXU_WIDTH, (j + 1) * MXU_WIDTH)
            gate_ref[:, sl] = jax.nn.sigmoid(z + gb_ref[:, sl]).astype(BF16)


def _in_proj(x, g, w_in, gate_bias, batch, seq):
    n_tok = x.shape[0]
    tm = PROJ_TILE
    tiles_per_seq = seq // tm
    out_shapes, out_specs = [], []
    for _ in range(3):
        for d in DILATIONS:
            out_shapes.append(jax.ShapeDtypeStruct((batch, d, seq // d, GROUP_WIDTH), BF16))
            out_specs.append(pl.BlockSpec(
                (1, d, tm // d, GROUP_WIDTH),
                lambda i: (i // tiles_per_seq, 0, i % tiles_per_seq, 0)))
    out_shapes.append(jax.ShapeDtypeStruct((N_OCTETS, _padded_rows(n_tok), LANES), F32))
    out_specs.append(pl.BlockSpec((N_OCTETS, _padded_rows(tm), LANES), lambda i: (0, i, 0)))
    out_shapes.append(jax.ShapeDtypeStruct((n_tok, 2 * D_MODEL), BF16))
    out_specs.append(pl.BlockSpec((tm, 2 * D_MODEL), lambda i: (i, 0)))
    outs = pl.pallas_call(
        _inproj_kernel,
        grid=(n_tok // tm,),
        in_specs=[pl.BlockSpec((tm, D_MODEL), lambda i: (i, 0)),
                  _const_spec((1, D_MODEL)),
                  pl.BlockSpec(memory_space=pl.ANY),
                  _const_spec((1, 2 * D_MODEL))],
        out_specs=out_specs,
        out_shape=out_shapes,
        scratch_shapes=[pltpu.VMEM((2 * N_STRIDED, tm, LANES), F32),
                        pltpu.VMEM((2 * N_TWO_HOP, tm, LANES), F32),
                        pltpu.VMEM((D_MODEL, IN_WIDTH), BF16),
                        pltpu.VMEM((2, 2, D_MODEL // (2 * WEIGHT_STAGE_STEPS), IN_WIDTH), F32),
                        pltpu.SemaphoreType.DMA((4,))],
        compiler_params=pltpu.CompilerParams(
            dimension_semantics=("arbitrary",), vmem_limit_bytes=VMEM_LIMIT_BYTES),
        name="in_proj",
    )(x, g.reshape(1, D_MODEL), w_in, gate_bias.reshape(1, 2 * D_MODEL))
    q, k, v = outs[0:3], outs[3:6], outs[6:9]
    return q, k, v, outs[9], outs[10]


def _t5_bucket_np(dist):
    max_exact = N_BUCKETS // 2
    d = np.maximum(dist, 1).astype(np.float32)
    ratio = np.log(d / np.float32(max_exact)) / np.float32(math.log(MAX_DISTANCE / max_exact))
    large = max_exact + (ratio * np.float32(N_BUCKETS - max_exact)).astype(np.int32)
    large = np.minimum(large, N_BUCKETS - 1)
    return np.where(dist < max_exact, dist, large)


def _bucket_tables():
    qi = np.arange(BLOCK)[:, None]
    kj = np.arange(2 * BLOCK)[None, :]
    steps = qi + BLOCK - kj
    tables = []
    for window, d in zip(WINDOWS, DILATIONS):
        band = (steps >= 0) & (steps <= window // d)
        bucket = _t5_bucket_np(np.maximum(steps, 0) * d)
        tables.append(np.where(band, bucket, -1).astype(np.int32))
    return jnp.asarray(np.stack(tables))


def _attn_block(q_blk, k_blk, v_blk, bias):
    lane_head = lax.broadcasted_iota(jnp.int32, (BLOCK, GROUP_WIDTH), 1) // HEAD_DIM
    zero = jnp.zeros_like(q_blk)
    qs = jnp.concatenate(
        [jnp.where(lane_head == h, q_blk, zero) for h in range(HEADS_PER_GROUP)], axis=0)
    logits = lax.dot_general(qs, k_blk, (((1,), (1,)), ((), ())), preferred_element_type=F32) + bias
    m = jnp.max(logits, axis=-1, keepdims=True)
    p = jnp.exp(logits - m)
    l = jnp.sum(p, axis=-1, keepdims=True)
    pv = jnp.dot(p.astype(BF16), v_blk, preferred_element_type=F32)
    o = jnp.zeros((BLOCK, GROUP_WIDTH), F32)
    den = jnp.ones((BLOCK, GROUP_WIDTH), F32)
    mx = jnp.zeros((BLOCK, GROUP_WIDTH), F32)
    for h in range(HEADS_PER_GROUP):
        rows = slice(h * BLOCK, (h + 1) * BLOCK)
        sel = lane_head == h
        o = jnp.where(sel, pv[rows], o)
        den = jnp.where(sel, l[rows], den)
        mx = jnp.where(sel, m[rows], mx)
    return o / den, mx + jnp.log(den)


def _build_bias(tab_ref, bucket_ref, bias_scr):
    own_block = lax.broadcasted_iota(jnp.int32, (BLOCK, 2 * BLOCK), 1) >= BLOCK
    for grp in range(N_GROUPS):
        bucket = bucket_ref[grp]
        for h in range(HEADS_PER_GROUP):
            col = grp * HEADS_PER_GROUP + h

            def pick(b, acc, bucket=bucket, col=col):
                return jnp.where(bucket == b, tab_ref[b, col], acc)

            bias = lax.fori_loop(0, N_BUCKETS, pick, jnp.full((BLOCK, 2 * BLOCK), NEG_INF, F32))
            rows = slice(h * BLOCK, (h + 1) * BLOCK)
            bias_scr[grp, 0, rows, :] = bias
            bias_scr[grp, 1, rows, :] = jnp.where(own_block, bias, NEG_INF)


def _block_aligned(row):
    return row if isinstance(row, int) else pl.multiple_of(row, BLOCK)


def _attn_kernel(*refs):
    n = N_GROUPS
    tab_ref, bucket_ref = refs[0:2]
    refs = refs[2:]
    q_refs, kc_refs, kp_refs = refs[0:n], refs[n:2 * n], refs[2 * n:3 * n]
    vc_refs, vp_refs = refs[3 * n:4 * n], refs[4 * n:5 * n]
    o_ref, bias_scr, o_scr, l_scr, t_scr = refs[5 * n:]
    tile = pl.program_id(1)

    @pl.when((pl.program_id(0) == 0) & (tile == 0))
    def _():
        _build_bias(tab_ref, bucket_ref, bias_scr)

    for grp, d in enumerate(DILATIONS):
        q_ref, kc_ref, kp_ref = q_refs[grp], kc_refs[grp], kp_refs[grp]
        vc_ref, vp_ref = vc_refs[grp], vp_refs[grp]
        blocks_per_residue = ATTN_TILE // d // BLOCK

        def run_block(r, blk, grp=grp, d=d, q_ref=q_ref, kc_ref=kc_ref, kp_ref=kp_ref,
                      vc_ref=vc_ref, vp_ref=vp_ref):
            if isinstance(blk, int) and blk == 0:
                k_blk = jnp.concatenate([kp_ref[0, r], kc_ref[0, r, 0:BLOCK, :]], axis=0)
                v_blk = jnp.concatenate([vp_ref[0, r], vc_ref[0, r, 0:BLOCK, :]], axis=0)
                bias = jnp.where(tile == 0, bias_scr[grp, 1], bias_scr[grp, 0])
                q_blk = q_ref[0, r, 0:BLOCK, :]
            else:
                kv_rows = pl.ds(_block_aligned((blk - 1) * BLOCK), 2 * BLOCK)
                k_blk = kc_ref[0, r, kv_rows, :]
                v_blk = vc_ref[0, r, kv_rows, :]
                bias = bias_scr[grp, 0]
                q_blk = q_ref[0, r, pl.ds(_block_aligned(blk * BLOCK), BLOCK), :]
            o, lse = _attn_block(q_blk, k_blk, v_blk, bias)
            if d <= SCATTER_STRIDE:
                start = blk * (BLOCK * d) + r
                rows = pl.ds(start, BLOCK, stride=d) if d > 1 else pl.ds(start, BLOCK)
                for s in range(2):
                    o_scr[grp, s, rows, :] = o[:, s * LANES:(s + 1) * LANES]
                    l_scr[grp, s, rows, :] = lse[:, s * LANES:(s + 1) * LANES]
            else:
                inner = d // SCATTER_STRIDE
                low, high = lax.rem(r, SCATTER_STRIDE), r // SCATTER_STRIDE
                rows = pl.ds(blk * (BLOCK * inner) + high, BLOCK, stride=inner)
                for s in range(2):
                    t_scr[0, s, low, rows, :] = o[:, s * LANES:(s + 1) * LANES]
                    t_scr[1, s, low, rows, :] = lse[:, s * LANES:(s + 1) * LANES]

        if blocks_per_residue == 1:
            def residue_set(i, c, run_block=run_block):
                for k in range(ATTN_INTERLEAVE):
                    run_block(ATTN_INTERLEAVE * i + k, 0)
                return c
            lax.fori_loop(0, d // ATTN_INTERLEAVE, residue_set, 0)
        elif d > 1:
            residues_per_set = max(1, ATTN_INTERLEAVE // blocks_per_residue)

            def residue_set(i, c, run_block=run_block, blocks_per_residue=blocks_per_residue,
                            residues_per_set=residues_per_set):
                for k in range(residues_per_set):
                    for blk in range(blocks_per_residue):
                        run_block(residues_per_set * i + k, blk)
                return c
            lax.fori_loop(0, d // residues_per_set, residue_set, 0)
        else:
            for blk in range(ATTN_INTERLEAVE):
                run_block(0, blk)

            def block_set(i, c, run_block=run_block):
                for k in range(ATTN_INTERLEAVE):
                    run_block(0, ATTN_INTERLEAVE * i + k)
                return c
            lax.fori_loop(1, blocks_per_residue // ATTN_INTERLEAVE, block_set, 0)

        if d > SCATTER_STRIDE:
            def unstage(low, c, grp=grp):
                rows = pl.ds(low, ATTN_TILE // SCATTER_STRIDE, stride=SCATTER_STRIDE)
                for s in range(2):
                    o_scr[grp, s, rows, :] = t_scr[0, s, low]
                    l_scr[grp, s, rows, :] = t_scr[1, s, low]
                return c
            lax.fori_loop(0, SCATTER_STRIDE, unstage, 0)

    merge_rows = 256

    def merge(c, carry):
        rows = pl.ds(pl.multiple_of(c * merge_rows, merge_rows), merge_rows)
        for s in range(2):
            lses = [l_scr[grp, s, rows, :] for grp in range(N_GROUPS)]
            mx = functools.reduce(jnp.maximum, lses)
            es = [jnp.exp(x - mx) for x in lses]
            num = sum(e * o_scr[grp, s, rows, :] for grp, e in enumerate(es))
            o_ref[rows, s * LANES:(s + 1) * LANES] = (num / sum(es)).astype(BF16)
        return carry

    lax.fori_loop(0, ATTN_TILE // merge_rows, merge, 0)


def _attention(q, k, v, rel_bias_table, batch, seq):
    tiles = seq // ATTN_TILE

    def cur_spec(d):
        return pl.BlockSpec((1, d, ATTN_TILE // d, GROUP_WIDTH), lambda b, j: (b, 0, j, 0))

    def prev_spec(d):
        per_tile = ATTN_TILE // d // BLOCK
        return pl.BlockSpec((1, d, BLOCK, GROUP_WIDTH),
                            lambda b, j: (b, 0, jnp.maximum(j * per_tile - 1, 0), 0))

    in_specs = ([pl.BlockSpec(memory_space=pltpu.SMEM),
                 _const_spec((N_GROUPS, BLOCK, 2 * BLOCK))]
                + [cur_spec(d) for d in DILATIONS] + [cur_spec(d) for d in DILATIONS]
                + [prev_spec(d) for d in DILATIONS] + [cur_spec(d) for d in DILATIONS]
                + [prev_spec(d) for d in DILATIONS])
    return pl.pallas_call(
        _attn_kernel,
        grid=(batch, tiles),
        in_specs=in_specs,
        out_specs=pl.BlockSpec((ATTN_TILE, GROUP_WIDTH), lambda b, j: (b * tiles + j, 0)),
        out_shape=jax.ShapeDtypeStruct((batch * seq, GROUP_WIDTH), BF16),
        scratch_shapes=[pltpu.VMEM((N_GROUPS, 2, HEADS_PER_GROUP * BLOCK, 2 * BLOCK), F32),
                        pltpu.VMEM((N_GROUPS, 2, ATTN_TILE, LANES), F32),
                        pltpu.VMEM((N_GROUPS, 2, ATTN_TILE, LANES), F32),
                        pltpu.VMEM((2, 2, SCATTER_STRIDE, ATTN_TILE // SCATTER_STRIDE, LANES), F32)],
        compiler_params=pltpu.CompilerParams(
            dimension_semantics=("arbitrary", "arbitrary"), vmem_limit_bytes=VMEM_LIMIT_BYTES),
        name="dilated_attn",
    )(rel_bias_table.astype(F32), _bucket_tables(), *q, *k, *k, *v, *v)


def _s5_matrices(a_re, a_im, log_dt, b_re, b_im, c_re, c_im, d_skip, seq):
    tc = SSM_CHUNK
    lam_re = a_re.astype(F32)
    lam_im = a_im.astype(F32)
    dt = jnp.exp(log_dt.astype(F32))[:, None]
    mag = jnp.exp(lam_re * dt)
    ab_re = mag * jnp.cos(lam_im * dt)
    ab_im = mag * jnp.sin(lam_im * dt)
    den = lam_re * lam_re + lam_im * lam_im
    xr = ab_re - 1.0
    coef_re = (xr * lam_re + ab_im * lam_im) / den
    coef_im = (ab_im * lam_re - xr * lam_im) / den
    br = b_re.astype(F32)
    bi = b_im.astype(F32)
    bb_re = coef_re[..., None] * br - coef_im[..., None] * bi
    bb_im = coef_re[..., None] * bi + coef_im[..., None] * br
    cr = c_re.astype(F32)
    ci = c_im.astype(F32)

    def a_pow(k):
        kk = k.astype(F32)[:, None, None]
        pm = jnp.exp(lam_re * dt * kk)
        return pm * jnp.cos(lam_im * dt * kk), pm * jnp.sin(lam_im * dt * kk)

    pw_re, pw_im = a_pow(jnp.arange(tc + 1))
    pr, pi = pw_re[:tc], pw_im[:tc]

    def by_octet(x):
        return x.reshape(x.shape[0], N_OCTETS, 1, SSM_OCTET, SSM_STATE)

    def channel_major(x):
        x = x.reshape(N_OCTETS, SSM_OCTET, SSM_GROUP, SSM_STATE)
        return jnp.transpose(x, (0, 2, 1, 3))[None]

    def compact(re, im):
        both = jnp.stack([re, im], axis=1)
        return both.reshape(re.shape[0], 2, N_OCTETS, SSM_GROUP, OCTET_STATE).astype(BF16)

    lr, li = by_octet(pr[::-1]), by_octet(pi[::-1])
    bt_re = channel_major(jnp.transpose(bb_re, (0, 2, 1)))
    bt_im = channel_major(jnp.transpose(bb_im, (0, 2, 1)))
    p_src = compact(lr * bt_re - li * bt_im, lr * bt_im + li * bt_re)

    qr, qi = by_octet(pw_re), by_octet(pw_im)
    ct_re, ct_im = channel_major(cr), channel_major(ci)
    q_src = compact(ct_re * qr - ct_im * qi, -(ct_re * qi + ct_im * qr))

    def lanes_by_octet(x_re, x_im):
        both = jnp.concatenate([x_re.reshape(-1, N_OCTETS, OCTET_STATE),
                                x_im.reshape(-1, N_OCTETS, OCTET_STATE)], axis=-1)
        return jnp.transpose(both, (1, 0, 2))

    seg = seq // tc // SCAN_SEGMENTS
    a_chunk = lanes_by_octet(*a_pow(jnp.array([tc, tc * seg])))
    chunk_powers = lanes_by_octet(*a_pow(tc * jnp.arange(seg)))
    skip = d_skip.astype(F32).reshape(N_OCTETS, 1, LANES)
    return p_src, q_src, a_chunk, chunk_powers, skip


def _s5_kernel(u_ref, p_ref, q_ref, a_ref, pw_ref, d_ref, o_ref, k_scr, p_scr, qt_scr, x_scr, s_scr, z_scr):
    n_chunks = x_scr.shape[0]
    tc = SSM_CHUNK

    @pl.when(pl.program_id(1) == 0)
    def _():
        row_group = lax.broadcasted_iota(jnp.int32, (LANES, OCTET_STATE), 0) // SSM_GROUP
        lane_group = lax.broadcasted_iota(jnp.int32, (LANES, OCTET_STATE), 1) // SSM_STATE
        same_group = row_group == lane_group
        zeros = jnp.zeros((LANES, OCTET_STATE), BF16)

        def block_diagonal(src_ref, t):
            halves = []
            for part in range(2):
                tiled = jnp.concatenate([src_ref[t, part, 0]] * SSM_OCTET, axis=0)
                halves.append(jnp.where(same_group, tiled, zeros))
            return jnp.concatenate(halves, axis=1)

        for t in range(tc):
            p_scr[t * LANES:(t + 1) * LANES, :] = block_diagonal(p_ref, t)
            qt_scr[t * LANES:(t + 1) * LANES, :] = block_diagonal(q_ref, t + 1)
        q_now = block_diagonal(q_ref, 0)
        diagonal = (lax.broadcasted_iota(jnp.int32, (LANES, LANES), 0)
                    == lax.broadcasted_iota(jnp.int32, (LANES, LANES), 1))
        k_scr[...] = jnp.zeros(k_scr.shape, BF16)
        for lag in range(tc):
            sigma = tc - 1 - lag
            resp = lax.dot_general(p_scr[sigma * LANES:(sigma + 1) * LANES, :], q_now,
                                   (((1,), (1,)), ((), ())), preferred_element_type=F32)
            if lag == 0:
                resp = resp + jnp.where(diagonal, d_ref[0], 0.0)
            resp = resp.astype(BF16)
            for s in range(tc - lag):
                t = s + lag
                k_scr[s * LANES:(s + 1) * LANES, t * LANES:(t + 1) * LANES] = resp

    inc = None
    for pair in range(tc // 2):
        for t in (2 * pair, 2 * pair + 1):
            x_scr[:, t * LANES:(t + 1) * LANES] = (
                u_ref[0, pl.ds(t, n_chunks, stride=CHUNK_PITCH), :].astype(BF16))
        rows = slice(pair * MXU_WIDTH, (pair + 1) * MXU_WIDTH)
        part = jnp.dot(x_scr[:, rows], p_scr[rows, :], preferred_element_type=F32)
        inc = part if inc is None else inc + part
    s_scr[...] = inc

    re, im = slice(0, OCTET_STATE), slice(OCTET_STATE, 2 * OCTET_STATE)

    def advance(state, a_row, inc_re, inc_im):
        a_re, a_im = a_ref[0, a_row:a_row + 1, re], a_ref[0, a_row:a_row + 1, im]
        s_re, s_im = state
        return (a_re * s_re - a_im * s_im + inc_re, a_re * s_im + a_im * s_re + inc_im)

    seg = n_chunks // SCAN_SEGMENTS
    steps_per_pair = seg // (tc // 2)
    zero = jnp.zeros((1, OCTET_STATE), F32)
    carries = [(zero, zero)] * SCAN_SEGMENTS
    within = []
    for pair in range(tc // 2):
        cols = slice(pair * MXU_WIDTH, (pair + 1) * MXU_WIDTH)
        live = (2 * pair + 2) * LANES
        within.append(jnp.dot(x_scr[:, 0:live], k_scr[0:live, cols], preferred_element_type=F32))
        for j in range(pair * steps_per_pair, (pair + 1) * steps_per_pair):
            for s in range(SCAN_SEGMENTS):
                row = slice(s * seg + j, s * seg + j + 1)
                z_scr[row, re], z_scr[row, im] = carries[s]
                carries[s] = advance(carries[s], 0, s_scr[row, re], s_scr[row, im])

    entering = (zero, zero)
    for s in range(1, SCAN_SEGMENTS):
        entering = advance(entering, 1, *carries[s - 1])
        rows = slice(s * seg, (s + 1) * seg)
        p_re, p_im = pw_ref[0, :, re], pw_ref[0, :, im]
        z_scr[rows, re] = z_scr[rows, re] + (p_re * entering[0] - p_im * entering[1])
        z_scr[rows, im] = z_scr[rows, im] + (p_re * entering[1] + p_im * entering[0])

    zb = z_scr[...].astype(BF16)
    for pair in range(tc // 2):
        cols = slice(pair * MXU_WIDTH, (pair + 1) * MXU_WIDTH)
        y = within[pair] + lax.dot_general(zb, qt_scr[cols, :], (((1,), (1,)), ((), ())),
                                           preferred_element_type=F32)
        y = jax.nn.gelu(y)
        o_ref[0, pl.ds(2 * pair, n_chunks, stride=CHUNK_PITCH), :] = y[:, 0:LANES]
        o_ref[0, pl.ds(2 * pair + 1, n_chunks, stride=CHUNK_PITCH), :] = y[:, LANES:2 * LANES]
    for t in range(tc, CHUNK_PITCH):
        o_ref[0, pl.ds(t, n_chunks, stride=CHUNK_PITCH), :] = jnp.zeros((n_chunks, LANES), F32)


def _s5(u, mats, batch, seq):
    p_src, q_src, a_chunk, chunk_powers, skip = mats
    n_chunks = seq // SSM_CHUNK
    width = SSM_CHUNK * LANES

    def per_octet(shape):
        return pl.BlockSpec((1,) + shape, lambda o, b: (o,) + (0,) * len(shape),
                            pipeline_mode=pl.Buffered(1))

    def compact_spec(entries):
        return pl.BlockSpec((entries, 2, 1, SSM_GROUP, OCTET_STATE), lambda o, b: (0, 0, o, 0, 0))

    return pl.pallas_call(
        _s5_kernel,
        grid=(N_OCTETS, batch),
        in_specs=[pl.BlockSpec((1, _padded_rows(seq), LANES), lambda o, b: (o, b, 0)),
                  compact_spec(SSM_CHUNK),
                  compact_spec(SSM_CHUNK + 1),
                  per_octet((2, 2 * OCTET_STATE)),
                  per_octet((n_chunks // SCAN_SEGMENTS, 2 * OCTET_STATE)),
                  per_octet((1, LANES))],
        out_specs=pl.BlockSpec((1, _padded_rows(seq), LANES), lambda o, b: (o, b, 0)),
        out_shape=jax.ShapeDtypeStruct((N_OCTETS, _padded_rows(batch * seq), LANES), F32),
        scratch_shapes=[pltpu.VMEM((width, width), BF16),
                        pltpu.VMEM((width, 2 * OCTET_STATE), BF16),
                        pltpu.VMEM((width, 2 * OCTET_STATE), BF16),
                        pltpu.VMEM((n_chunks, width), BF16),
                        pltpu.VMEM((n_chunks, 2 * OCTET_STATE), F32),
                        pltpu.VMEM((n_chunks, 2 * OCTET_STATE), F32)],
        compiler_params=pltpu.CompilerParams(
            dimension_semantics=("arbitrary", "arbitrary"), vmem_limit_bytes=VMEM_LIMIT_BYTES),
        name="s5_mixer",
    )(u, p_src, q_src, a_chunk, chunk_powers, skip)


def _merge_ffn_kernel(y_ref, oa_ref, gate_ref, x_ref, wglu_ref, wssm_ref, wattn_ref, wout_ref,
                      g_ref, wg_hbm, wu_hbm, wd_hbm, fn_ref, o_ref, *weight_scratch, final):
    @pl.when(pl.program_id(0) == 0)
    def _():
        _fetch_ffn_weights(wg_hbm, wu_hbm, wd_hbm, *weight_scratch)

    wg_scr, wu_scr, wd_scr = weight_scratch[0:3]
    y_gelu = jnp.concatenate(
        [_load_padded_chunks(y_ref, o, x_ref.shape[0]).astype(BF16) for o in range(N_OCTETS)], axis=1)
    glu = jnp.dot(y_gelu, wglu_ref[...], preferred_element_type=F32)
    ys = (glu[:, 0:SSM_WIDTH] * jax.nn.sigmoid(glu[:, SSM_WIDTH:2 * SSM_WIDTH])).astype(BF16)
    y_ssm = jnp.dot(ys, wssm_ref[...], preferred_element_type=F32)
    y_attn = jnp.dot(oa_ref[...], wattn_ref[...], preferred_element_type=F32)
    g_attn = gate_ref[:, 0:D_MODEL].astype(F32)
    g_ssm = gate_ref[:, D_MODEL:2 * D_MODEL].astype(F32)
    mixed = (g_attn * y_attn + g_ssm * y_ssm).astype(BF16)
    x = x_ref[...] + jnp.dot(mixed, wout_ref[...], preferred_element_type=F32)
    y = _swiglu_residual(x, g_ref[...], wg_scr, wu_scr, wd_scr)
    o_ref[...] = _rms(y, fn_ref[...]) if final else y


def _merge_ffn(y_gelu, o_attn, gates, x, w_glu, w_ssm, w_attn, w_out, g, wg, wu, wd, final_g, final):
    n_tok = x.shape[0]
    tm = MERGE_TILE

    def rows(width):
        return pl.BlockSpec((tm, width), lambda i: (i, 0))

    return pl.pallas_call(
        functools.partial(_merge_ffn_kernel, final=final),
        grid=(n_tok // tm,),
        in_specs=[pl.BlockSpec((N_OCTETS, _padded_rows(tm), LANES), lambda i: (0, i, 0)),
                  rows(GROUP_WIDTH), rows(2 * D_MODEL), rows(D_MODEL),
                  _const_spec((SSM_WIDTH, 2 * SSM_WIDTH)), _const_spec((SSM_WIDTH, D_MODEL)),
                  _const_spec((GROUP_WIDTH, D_MODEL)), _const_spec((D_MODEL, D_MODEL))]
                 + _ffn_weight_specs() + [_const_spec((1, D_MODEL))],
        out_specs=rows(D_MODEL),
        out_shape=jax.ShapeDtypeStruct((n_tok, D_MODEL), F32),
        scratch_shapes=_ffn_weight_scratch(),
        compiler_params=pltpu.CompilerParams(
            dimension_semantics=("arbitrary",), vmem_limit_bytes=VMEM_LIMIT_BYTES),
        name="merge_ffn",
    )(y_gelu, o_attn, gates, x, w_glu, w_ssm, w_attn, w_out,
      g.reshape(1, D_MODEL), wg, wu, wd, final_g.reshape(1, D_MODEL))


def kernel(x, ffn1_norm, ffn1_w_gate, ffn1_w_up, ffn1_w_down, mix_norm, w_in, gate_bias,
           rel_bias_table, ssm_a_re, ssm_a_im, ssm_log_dt, ssm_b_re, ssm_b_im, ssm_c_re,
           ssm_c_im, ssm_d, ssm_w_glu, w_attn_branch, w_ssm_branch, w_out, ffn2_norm,
           ffn2_w_gate, ffn2_w_up, ffn2_w_down, final_norm):
    batch, seq, _ = x.shape
    depth = ffn1_norm.shape[0]
    h = x.reshape(batch * seq, D_MODEL)
    for l in range(depth):
        h = _ffn(h, ffn1_norm[l], ffn1_w_gate[l].astype(F32), ffn1_w_up[l].astype(F32),
                 ffn1_w_down[l].astype(F32))
        q, k, v, u, gates = _in_proj(h, mix_norm[l], w_in[l].astype(F32), gate_bias[l], batch, seq)
        o_attn = _attention(q, k, v, rel_bias_table, batch, seq)
        mats = _s5_matrices(ssm_a_re[l], ssm_a_im[l], ssm_log_dt[l], ssm_b_re[l], ssm_b_im[l],
                            ssm_c_re[l], ssm_c_im[l], ssm_d[l], seq)
        y_gelu = _s5(u, mats, batch, seq)
        h = _merge_ffn(y_gelu, o_attn, gates, h, ssm_w_glu[l].astype(BF16),
                       w_ssm_branch[l].astype(BF16), w_attn_branch[l].astype(BF16),
                       w_out[l].astype(BF16), ffn2_norm[l], ffn2_w_gate[l].astype(F32),
                       ffn2_w_up[l].astype(F32), ffn2_w_down[l].astype(F32), final_norm,
                       final=(l == depth - 1))
    return h.reshape(batch, seq, D_MODEL)
```
